```python
import jax, jax.numpy as jnp
from jax import lax
import numpy as np

D_MODEL = 2048
BATCH = 4
SEQ = 8192
DEPTH = 1

CHUNK = 64
N_HEADS = 8
HEAD_DIM = 128
ATT_WIDTH = N_HEADS * HEAD_DIM
ROPE_THETA = 500000.0
ROT_DIM = HEAD_DIM // 4
IDX_HEADS = 16
IDX_DIM = 64
IDX_ROT_DIM = IDX_DIM // 4
TOPK_MAX = 256
Q_BLOCK = 128
SGU_LEN = 128
SGU_GROUPS = 8
SGU_GROUP_DIM = 128
SGU_WIDTH = SGU_GROUPS * SGU_GROUP_DIM
D_FF = 5632
EPS = 1e-6

IN_SIZES = [ATT_WIDTH, ATT_WIDTH, ATT_WIDTH,
            IDX_HEADS * IDX_DIM, IDX_DIM, IDX_HEADS,
            SGU_WIDTH, SGU_WIDTH,
            D_MODEL, D_MODEL]
IN_COLS = sum(IN_SIZES)
SPLIT_POINTS = [sum(IN_SIZES[:i + 1]) for i in range(len(IN_SIZES) - 1)]

kernel_name = "hybrid_dsa_gmlp_gated_macaron_block"


def rms_norm(t, g):
    tf = t.astype(jnp.float32)
    y = tf * lax.rsqrt(jnp.mean(tf * tf, axis=-1, keepdims=True) + EPS)
    return (y * g.astype(jnp.float32)).astype(t.dtype)


def rope_tables(positions, rot_dim):
    inv_freq = ROPE_THETA ** (-jnp.arange(0, rot_dim, 2, dtype=jnp.float32) / rot_dim)
    ang = positions.astype(jnp.float32)[..., None] * inv_freq
    return jnp.cos(ang), jnp.sin(ang)


def partial_rope(t, cos, sin):
    half = cos.shape[-1]
    cos = cos.astype(t.dtype)
    sin = sin.astype(t.dtype)
    x1 = t[..., :half]
    x2 = t[..., half:2 * half]
    return jnp.concatenate([x1 * cos - x2 * sin, x2 * cos + x1 * sin, t[..., 2 * half:]], axis=-1)


def swiglu(h, w_gate, w_up, w_down):
    return (jax.nn.silu(h @ w_gate) * (h @ w_up)) @ w_down


def dsa_attention(q, k, v, qi, ki, wi):
    b, s = q.shape[:2]
    topk = min(TOPK_MAX, s // 4)
    nb = s // Q_BLOCK
    key_chunk = jnp.arange(s) // CHUNK
    gather = jax.vmap(lambda t, i: t[i])

    def to_blocks(t):
        return jnp.swapaxes(t.reshape((b, nb, Q_BLOCK) + t.shape[2:]), 0, 1)

    def one_block(args):
        blk, qb, qib, wib = args
        q_chunk = (blk * Q_BLOCK + jnp.arange(Q_BLOCK)) // CHUNK
        admissible = key_chunk[None, :] <= q_chunk[:, None]
        dots = jnp.einsum('bqhd,bsd->bqhs', qib, ki)
        score = jnp.einsum('bqhs,bqh->bqs', jax.nn.relu(dots), wib).astype(jnp.float32)
        score = jnp.where(admissible[None], score, -jnp.inf)
        _, idx = lax.top_k(score, topk)
        valid = key_chunk[idx] <= q_chunk[None, :, None]
        kg = gather(k, idx)
        vg = gather(v, idx)
        logits = jnp.einsum('bqhd,bqkhd->bhqk', qb, kg).astype(jnp.float32) * (HEAD_DIM ** -0.5)
        logits = jnp.where(valid[:, None], logits, -jnp.inf)
        p = jax.nn.softmax(logits, axis=-1).astype(v.dtype)
        return jnp.einsum('bhqk,bqkhd->bqhd', p, vg)

    out = lax.map(one_block, (jnp.arange(nb), to_blocks(q), to_blocks(qi), to_blocks(wi)))
    return jnp.swapaxes(out, 0, 1).reshape(b, s, N_HEADS * HEAD_DIM)


def spatial_gating(u, v, g_v, w_s, b_s):
    b, s, _ = v.shape
    v = rms_norm(v, g_v)
    pos_chunk = jnp.arange(SGU_LEN) // CHUNK
    mask = pos_chunk[None, :] <= pos_chunk[:, None]
    w = jnp.where(mask[None], w_s, jnp.zeros_like(w_s))
    vb = v.reshape(b, s // SGU_LEN, SGU_LEN, SGU_GROUPS, SGU_GROUP_DIM)
    mixed = jnp.einsum('gts,bnsgc->bntgc', w, vb) + jnp.swapaxes(b_s, 0, 1)[:, :, None]
    return u * mixed.reshape(b, s, SGU_WIDTH)


def setup_inputs(seed: int = 0) -> dict:
    key = jax.random.key(seed)
    ks = jax.random.split(key, 24)
    f32 = jnp.float32

    def w(k, shape, fan_in):
        return jax.random.normal(k, shape, f32) * (fan_in ** -0.5)

    def gain(k, n):
        return 1.0 + 0.02 * jax.random.normal(k, (DEPTH, n), f32)

    x = jax.random.normal(ks[0], (BATCH, SEQ, D_MODEL), f32)
    offsets = jax.random.randint(ks[1], (BATCH, 1), 0, 100000, dtype=jnp.int32)
    positions = offsets + jnp.arange(SEQ, dtype=jnp.int32)[None, :]
    return {
        "x": x,
        "positions": positions,
        "norm_ffn1": gain(ks[2], D_MODEL),
        "ffn1_w_gate": w(ks[3], (DEPTH, D_MODEL, D_FF), D_MODEL),
        "ffn1_w_up": w(ks[4], (DEPTH, D_MODEL, D_FF), D_MODEL),
        "ffn1_w_down": w(ks[5], (DEPTH, D_FF, D_MODEL), D_FF),
        "norm_mix": gain(ks[6], D_MODEL),
        "w_in": w(ks[7], (DEPTH, D_MODEL, IN_COLS), D_MODEL),
        "q_norm": gain(ks[8], HEAD_DIM),
        "k_norm": gain(ks[9], HEAD_DIM),
        "idx_k_norm": gain(ks[10], IDX_DIM),
        "sgu_v_norm": gain(ks[11], SGU_WIDTH),
        "sgu_w_s": w(ks[12], (DEPTH, SGU_GROUPS, SGU_LEN, SGU_LEN), SGU_LEN),
        "sgu_b_s": 0.02 * jax.random.normal(ks[13], (DEPTH, SGU_GROUPS, SGU_LEN), f32),
        "w_up_attn": w(ks[14], (DEPTH, ATT_WIDTH, D_MODEL), ATT_WIDTH),
        "w_up_sgu": w(ks[15], (DEPTH, SGU_WIDTH, D_MODEL), SGU_WIDTH),
        "w_out": w(ks[16], (DEPTH, D_MODEL, D_MODEL), D_MODEL),
        "norm_ffn2": gain(ks[17], D_MODEL),
        "ffn2_w_gate": w(ks[18], (DEPTH, D_MODEL, D_FF), D_MODEL),
        "ffn2_w_up": w(ks[19], (DEPTH, D_MODEL, D_FF), D_MODEL),
        "ffn2_w_down": w(ks[20], (DEPTH, D_FF, D_MODEL), D_FF),
    }


def reference(x, positions, norm_ffn1, ffn1_w_gate, ffn1_w_up, ffn1_w_down, norm_mix, w_in,
              q_norm, k_norm, idx_k_norm, sgu_v_norm, sgu_w_s, sgu_b_s, w_up_attn, w_up_sgu,
              w_out, norm_ffn2, ffn2_w_gate, ffn2_w_up, ffn2_w_down):
    b, s, _ = x.shape
    cos_a, sin_a = rope_tables(positions, ROT_DIM)
    cos_i, sin_i = rope_tables(positions, IDX_ROT_DIM)
    idx_w_scale = (IDX_HEADS ** -0.5) * (IDX_DIM ** -0.5)
    for l in range(DEPTH):
        x = x + 0.5 * swiglu(rms_norm(x, norm_ffn1[l]), ffn1_w_gate[l], ffn1_w_up[l], ffn1_w_down[l])

        h = rms_norm(x, norm_mix[l])
        proj = h @ w_in[l]
        q, k, v, qi, ki, wi, u, vs, ga, gb = jnp.split(proj, SPLIT_POINTS, axis=-1)

        q = partial_rope(rms_norm(q.reshape(b, s, N_HEADS, HEAD_DIM), q_norm[l]),
                         cos_a[:, :, None], sin_a[:, :, None])
        k = partial_rope(rms_norm(k.reshape(b, s, N_HEADS, HEAD_DIM), k_norm[l]),
                         cos_a[:, :, None], sin_a[:, :, None])
        v = v.reshape(b, s, N_HEADS, HEAD_DIM)
        qi = partial_rope(qi.reshape(b, s, IDX_HEADS, IDX_DIM), cos_i[:, :, None], sin_i[:, :, None])
        ki = partial_rope(rms_norm(ki, idx_k_norm[l]), cos_i, sin_i)
        y_a = dsa_attention(q, k, v, qi, ki, wi * idx_w_scale)

        y_b = spatial_gating(jax.nn.gelu(u), jax.nn.gelu(vs), sgu_v_norm[l], sgu_w_s[l], sgu_b_s[l])

        merged = jax.nn.sigmoid(ga) * (y_a @ w_up_attn[l]) + jax.nn.sigmoid(gb) * (y_b @ w_up_sgu[l])
        x = x + merged @ w_out[l]

        x = x + 0.5 * swiglu(rms_norm(x, norm_ffn2[l]), ffn2_w_gate[l], ffn2_w_up[l], ffn2_w_down[l])
    return x
```

```python
import functools

import jax
import jax.numpy as jnp
from jax import lax
from jax.experimental import pallas as pl
from jax.experimental.pallas import tpu as pltpu

F32 = jnp.float32
BF16 = jnp.bfloat16
I32 = jnp.int32

EPS = 1e-6
CHUNK = 64
N_HEADS = 8
HEAD_DIM = 128
ATT_WIDTH = N_HEADS * HEAD_DIM
ROPE_THETA = 500000.0
ROT_DIM = HEAD_DIM // 4
IDX_HEADS = 16
IDX_DIM = 64
IDX_ROT_DIM = IDX_DIM // 4
TOPK_MAX = 256
SGU_LEN = 128
SGU_GROUPS = 8
SGU_GROUP_DIM = 128
SGU_WIDTH = SGU_GROUPS * SGU_GROUP_DIM

LANES = 128
MXU_DIM = 256
VMEM_LIMIT_BYTES = 56 * 1024 * 1024

SEL_TQ = 128
SEL_TK = MXU_DIM
ATT_T = 256
MASKED_LOGIT = -1e30
INT_MIN = -2 ** 31
NEG_INF_KEY = (0xFF800000 ^ 0x7FFFFFFF) - 2 ** 32


def _params(*sem):
    return pltpu.CompilerParams(dimension_semantics=sem,
                                vmem_limit_bytes=VMEM_LIMIT_BYTES)


def _rms(t, gain):
    ms = jnp.mean(t * t, axis=-1, keepdims=True)
    return t * lax.rsqrt(ms + EPS) * gain


def _rope_tables(pos_ref, invf_ref, half, period):
    ang = pos_ref[...].astype(F32) * invf_ref[...]
    c = jnp.cos(ang)
    s = jnp.sin(ang)
    lane = lax.broadcasted_iota(I32, ang.shape, 1) & (period - 1)
    return c, jnp.where(lane < half, -s, 0.0), jnp.where(lane >= half, s, 0.0)


def _rope(t, c, s1, s2, half):
    return (t * c + pltpu.roll(t, LANES - half, 1) * s1
            + pltpu.roll(t, half, 1) * s2)


def _ffn_kernel(x_ref, g_ref, wg_ref, wu_ref, wd_ref, g2_ref, o_ref, *rest,
                emit_norm):
    if emit_norm:
        h_out_ref, hn_ref = rest
    else:
        (hn_ref,) = rest
    j = pl.program_id(1)

    @pl.when(j == 0)
    def _():
        hn_ref[...] = _rms(x_ref[...], g_ref[...]).astype(BF16)
        o_ref[...] = jnp.zeros_like(o_ref)

    h = hn_ref[...]
    a = jnp.dot(h, wg_ref[...], preferred_element_type=F32)
    b = jnp.dot(h, wu_ref[...], preferred_element_type=F32)
    act = (a * jax.nn.sigmoid(a) * b).astype(BF16)
    o_ref[...] += jnp.dot(act, wd_ref[...], preferred_element_type=F32)

    @pl.when(j == pl.num_programs(1) - 1)
    def _():
        y = x_ref[...] + 0.5 * o_ref[...]
        o_ref[...] = y
        if emit_norm:
            h_out_ref[...] = _rms(y, g2_ref[...]).astype(BF16)


def _ffn(x, gain, w_gate, w_up, w_down, gain2, *, emit_norm, tm=512, tf=512):
    n, d = x.shape
    f = w_gate.shape[1]
    tm = min(tm, n)
    out_shape = [jax.ShapeDtypeStruct((n, d), F32)]
    out_specs = [pl.BlockSpec((tm, d), lambda i, j: (i, 0))]
    if emit_norm:
        out_shape.append(jax.ShapeDtypeStruct((n, d), BF16))
        out_specs.append(pl.BlockSpec((tm, d), lambda i, j: (i, 0)))
    res = pl.pallas_call(
        functools.partial(_ffn_kernel, emit_norm=emit_norm),
        grid=(n // tm, f // tf),
        in_specs=[
            pl.BlockSpec((tm, d), lambda i, j: (i, 0)),
            pl.BlockSpec((1, d), lambda i, j: (0, 0)),
            pl.BlockSpec((d, tf), lambda i, j: (0, j)),
            pl.BlockSpec((d, tf), lambda i, j: (0, j)),
            pl.BlockSpec((tf, d), lambda i, j: (j, 0)),
            pl.BlockSpec((1, d), lambda i, j: (0, 0)),
        ],
        out_specs=out_specs,
        out_shape=out_shape,
        scratch_shapes=[pltpu.VMEM((tm, d), BF16)],
        compiler_params=_params("parallel", "arbitrary"),
        name="ffn_norm" if emit_norm else "ffn",
    )(x, gain, w_gate, w_up, w_down, gain2)
    return res if emit_norm else res[0]


def _qkv_kernel(h_ref, w_ref, pos_ref, invf_ref, gains_ref, o_ref,
                c_ref, s1_ref, s2_ref, *, rope_tiles):
    j = pl.program_id(1)
    half = ROT_DIM // 2

    @pl.when(j == 0)
    def _():
        c, s1, s2 = _rope_tables(pos_ref, invf_ref, half, LANES)
        c_ref[...] = c
        s1_ref[...] = s1
        s2_ref[...] = s2

    res = jnp.dot(h_ref[...], w_ref[...], preferred_element_type=F32)

    @pl.when(j < rope_tiles)
    def _():
        gain = gains_ref[0]
        for hh in range(res.shape[1] // HEAD_DIM):
            sl = slice(hh * HEAD_DIM, (hh + 1) * HEAD_DIM)
            t = _rms(res[:, sl], gain)
            t = _rope(t, c_ref[...], s1_ref[...], s2_ref[...], half)
            o_ref[:, sl] = t.astype(BF16)

    @pl.when(j >= rope_tiles)
    def _():
        o_ref[...] = res.astype(BF16)


def _qkv(h, w_qkv, pos, invf, gains, *, tm=1024, tn=512):
    n, d = h.shape
    cols = w_qkv.shape[1]
    tm = min(tm, n)
    per = ATT_WIDTH // tn
    return pl.pallas_call(
        functools.partial(_qkv_kernel, rope_tiles=2 * per),
        grid=(n // tm, cols // tn),
        in_specs=[
            pl.BlockSpec((tm, d), lambda i, j: (i, 0)),
            pl.BlockSpec((d, tn), lambda i, j: (0, j)),
            pl.BlockSpec((tm, 1), lambda i, j: (i, 0)),
            pl.BlockSpec((1, LANES), lambda i, j: (0, 0)),
            pl.BlockSpec((1, 1, HEAD_DIM),
                         lambda i, j: (jnp.minimum(j // per, 1), 0, 0)),
        ],
        out_specs=pl.BlockSpec((tm, tn), lambda i, j: (i, j)),
        out_shape=jax.ShapeDtypeStruct((n, cols), BF16),
        scratch_shapes=[pltpu.VMEM((tm, LANES), F32)] * 3,
        compiler_params=_params("parallel", "arbitrary"),
        name="qkv_proj",
    )(h, w_qkv, pos, invf, gains)


def _idx_kernel(h_ref, w_ref, pos_ref, invf_ref, kgain_ref,
                qi_ref, kcat_ref, wi_ref, *, w_scale):
    half = IDX_ROT_DIM // 2
    qw = IDX_HEADS * IDX_DIM
    res = jnp.dot(h_ref[...], w_ref[...], preferred_element_type=F32)
    c, s1, s2 = _rope_tables(pos_ref, invf_ref, half, IDX_DIM)
    for t in range(qw // LANES):
        sl = slice(t * LANES, (t + 1) * LANES)
        qi_ref[:, sl] = _rope(res[:, sl], c, s1, s2, half)

    r = res[:, qw:qw + LANES]
    lane = lax.broadcasted_iota(I32, r.shape, 1)
    is_k = lane < IDX_DIM
    ms = jnp.sum(jnp.where(is_k, r * r, 0.0), axis=-1, keepdims=True) * (1.0 / IDX_DIM)
    kn = r * lax.rsqrt(ms + EPS) * kgain_ref[...]
    kr = _rope(kn, jnp.where(is_k, c, 1.0), jnp.where(is_k, s1, 0.0),
               jnp.where(is_k, s2, 0.0), half)
    hi = kr.astype(BF16).astype(F32)
    lo = kr - hi
    hi_lo = jnp.where(is_k, hi, pltpu.roll(lo, IDX_DIM, 1)).astype(BF16)
    kcat_ref[:, 0:LANES] = hi_lo
    kcat_ref[:, LANES:2 * LANES] = hi_lo
    is_w = (lane >= IDX_DIM) & (lane < IDX_DIM + IDX_HEADS)
    wi_ref[...] = pltpu.roll(jnp.where(is_w, r * w_scale, 0.0), LANES - IDX_DIM, 1)


def _idx(h, w_idx, pos, invf, kgain, *, tm=512):
    n, d = h.shape
    cols = w_idx.shape[1]
    qw = IDX_HEADS * IDX_DIM
    tm = min(tm, n)
    w_scale = (IDX_HEADS ** -0.5) * (IDX_DIM ** -0.5)
    return pl.pallas_call(
        functools.partial(_idx_kernel, w_scale=w_scale),
        grid=(n // tm,),
        in_specs=[
            pl.BlockSpec((tm, d), lambda i: (i, 0)),
            pl.BlockSpec((d, cols), lambda i: (0, 0)),
            pl.BlockSpec((tm, 1), lambda i: (i, 0)),
            pl.BlockSpec((1, LANES), lambda i: (0, 0)),
            pl.BlockSpec((1, LANES), lambda i: (0, 0)),
        ],
        out_specs=[
            pl.BlockSpec((tm, qw), lambda i: (i, 0)),
            pl.BlockSpec((tm, 2 * LANES), lambda i: (i, 0)),
            pl.BlockSpec((tm, LANES), lambda i: (i, 0)),
        ],
        out_shape=[
            jax.ShapeDtypeStruct((n, qw), F32),
            jax.ShapeDtypeStruct((n, 2 * LANES), BF16),
            jax.ShapeDtypeStruct((n, LANES), F32),
        ],
        compiler_params=_params("parallel"),
        name="idx_proj",
    )(h, w_idx, pos, invf, kgain)


def _sgu_kernel(h_ref, w_ref, gv_ref, ws_ref, bs_ref, o_ref):
    tm = h_ref.shape[0]
    groups = tm // SGU_LEN
    res = jnp.dot(h_ref[...], w_ref[...], preferred_element_type=F32)
    u = jax.nn.gelu(res[:, :SGU_WIDTH])
    v = _rms(jax.nn.gelu(res[:, SGU_WIDTH:]), gv_ref[...]).astype(BF16)
    row = lax.broadcasted_iota(I32, (SGU_LEN, SGU_LEN), 0)
    col = lax.broadcasted_iota(I32, (SGU_LEN, SGU_LEN), 1)
    causal = (col // CHUNK) <= (row // CHUNK)
    for g in range(SGU_GROUPS):
        cs = slice(g * SGU_GROUP_DIM, (g + 1) * SGU_GROUP_DIM)
        wg = jnp.where(causal, ws_ref[g], 0.0).astype(BF16)
        vg = jnp.concatenate(
            [v[n * SGU_LEN:(n + 1) * SGU_LEN, cs] for n in range(groups)], axis=1)
        mixed = jnp.dot(wg, vg, preferred_element_type=F32) + bs_ref[g]
        for n in range(groups):
            rs = slice(n * SGU_LEN, (n + 1) * SGU_LEN)
            m = mixed[:, n * SGU_GROUP_DIM:(n + 1) * SGU_GROUP_DIM]
            o_ref[rs, cs] = (u[rs, cs] * m).astype(BF16)


def _sgu(h, w_sgu, gv, w_s, b_s, *, tm=512):
    n, d = h.shape
    tm = min(tm, n)
    return pl.pallas_call(
        _sgu_kernel,
        grid=(n // tm,),
        in_specs=[
            pl.BlockSpec((tm, d), lambda i: (i, 0)),
            pl.BlockSpec((d, 2 * SGU_WIDTH), lambda i: (0, 0)),
            pl.BlockSpec((1, SGU_WIDTH), lambda i: (0, 0)),
            pl.BlockSpec((SGU_GROUPS, SGU_LEN, SGU_LEN), lambda i: (0, 0, 0)),
            pl.BlockSpec((SGU_GROUPS, SGU_LEN, 1), lambda i: (0, 0, 0)),
        ],
        out_specs=pl.BlockSpec((tm, SGU_WIDTH), lambda i: (i, 0)),
        out_shape=jax.ShapeDtypeStruct((n, SGU_WIDTH), BF16),
        compiler_params=_params("parallel"),
        name="sgu_branch",
    )(h, w_sgu, gv, w_s, b_s)


def _gates_kernel(h_ref, w_ref, o_ref):
    res = jnp.dot(h_ref[...], w_ref[...], preferred_element_type=F32)
    o_ref[...] = jax.nn.sigmoid(res).astype(BF16)


def _gates(h, w_g, *, tm=1024, tn=1024):
    n, d = h.shape
    cols = w_g.shape[1]
    tm = min(tm, n)
    return pl.pallas_call(
        _gates_kernel,
        grid=(n // tm, cols // tn),
        in_specs=[
            pl.BlockSpec((tm, d), lambda i, j: (i, 0)),
            pl.BlockSpec((d, tn), lambda i, j: (0, j)),
        ],
        out_specs=pl.BlockSpec((tm, tn), lambda i, j: (i, j)),
        out_shape=jax.ShapeDtypeStruct((n, cols), BF16),
        compiler_params=_params("parallel", "arbitrary"),
        name="gates_proj",
    )(h, w_g)


def _select_kernel(qi_ref, wi_ref, kt_ref, mask_ref, qcat_ref, wb_ref, key_ref,
                   *, topk):
    tq, tk = SEL_TQ, SEL_TK
    n_tiles = key_ref.shape[0]
    i = pl.program_id(1)
    n_live = ((i + 1) * tq + tk - 1) // tk

    lane = lax.broadcasted_iota(I32, (tq, LANES), 1)
    low = lane < IDX_DIM
    for pair in range(IDX_HEADS // 2):
        t = qi_ref[0, :, pair * LANES:(pair + 1) * LANES]
        hi = t.astype(BF16).astype(F32)
        lo = t - hi
        hi_r = pltpu.roll(hi, IDX_DIM, 1)
        lo_r = pltpu.roll(lo, IDX_DIM, 1)
        r0 = slice((2 * pair) * tq, (2 * pair + 1) * tq)
        r1 = slice((2 * pair + 1) * tq, (2 * pair + 2) * tq)
        qcat_ref[r0, 0:LANES] = jnp.where(low, hi, hi_r).astype(BF16)
        qcat_ref[r0, LANES:2 * LANES] = jnp.where(low, lo, lo_r).astype(BF16)
        qcat_ref[r1, 0:LANES] = jnp.where(low, hi_r, hi).astype(BF16)
        qcat_ref[r1, LANES:2 * LANES] = jnp.where(low, lo_r, lo).astype(BF16)
    w = wi_ref[0]
    for h in range(IDX_HEADS):
        wb_ref[h] = jnp.broadcast_to(w[:, h:h + 1], (tq, tk))

    q_chunk = (i * tq + lax.broadcasted_iota(I32, (tq, tk), 0)) // CHUNK
    k_lane = lax.broadcasted_iota(I32, (tq, tk), 1)

    def score_body(kt, carry):
        d = jnp.dot(qcat_ref[...], kt_ref[0, kt], preferred_element_type=F32)
        acc = jnp.zeros((tq, tk), F32)
        for h in range(IDX_HEADS):
            acc = acc + jnp.maximum(d[h * tq:(h + 1) * tq], 0.0) * wb_ref[h]
        admissible = ((kt * tk + k_lane) // CHUNK) <= q_chunk
        s = jnp.where(admissible, acc + 0.0, -jnp.inf)
        bits = pltpu.bitcast(s, I32)
        key_ref[kt] = bits ^ ((bits >> 31) & 0x7FFFFFFF)
        return carry

    lax.fori_loop(0, n_live, score_body, 0)

    def bit_body(b, t):
        cand = t | jnp.left_shift(jnp.int32(1), 31 - b)
        cand_b = jnp.broadcast_to(cand ^ INT_MIN, (tq, LANES))

        def count_body(kt, acc):
            k = key_ref[kt]
            for c in range(tk // LANES):
                acc = acc + jnp.where(k[:, c * LANES:(c + 1) * LANES] >= cand_b, 1.0, 0.0)
            return acc

        acc = lax.fori_loop(0, n_live, count_body, jnp.zeros((tq, LANES), F32))
        cnt = jnp.sum(acc, axis=1, keepdims=True)
        return jnp.where(cnt >= topk, cand, t)

    t = lax.fori_loop(0, 32, bit_body, jnp.zeros((tq, 1), I32))
    thr = jnp.broadcast_to(t ^ INT_MIN, (tq, tk))

    def mask_body(kt, carry):
        k = key_ref[kt]
        sel = (k >= thr) & (k > NEG_INF_KEY)
        mask_ref[0, 0, kt] = jnp.where(sel, 1, 0).astype(jnp.int8)
        return carry

    lax.fori_loop(0, n_live, mask_body, 0)

    def zero_body(kt, carry):
        mask_ref[0, 0, kt] = jnp.zeros((tq, tk), jnp.int8)
        return carry

    lax.fori_loop(n_live, n_tiles, zero_body, 0)


def _select(qi, wi, kcat_t, *, topk):
    b, s, qw = qi.shape
    tq, tk = SEL_TQ, SEL_TK
    nq, nk = s // tq, s // tk
    return pl.pallas_call(
        functools.partial(_select_kernel, topk=topk),
        grid=(b, nq),
        in_specs=[
            pl.BlockSpec((1, tq, qw), lambda bi, i: (bi, i, 0)),
            pl.BlockSpec((1, tq, LANES), lambda bi, i: (bi, i, 0)),
            pl.BlockSpec((1, nk, 4 * IDX_DIM, tk), lambda bi, i: (bi, 0, 0, 0)),
        ],
        out_specs=pl.BlockSpec((1, 1, nk, tq, tk), lambda bi, i: (bi, i, 0, 0, 0)),
        out_shape=jax.ShapeDtypeStruct((b, nq, nk, tq, tk), jnp.int8),
        scratch_shapes=[
            pltpu.VMEM((IDX_HEADS * tq, 4 * IDX_DIM), BF16),
            pltpu.VMEM((IDX_HEADS, tq, tk), F32),
            pltpu.VMEM((nk, tq, tk), I32),
        ],
        compiler_params=_params("parallel", "arbitrary"),
        name="index_select",
    )(qi, wi, kcat_t)


def _attn_kernel(q_ref, kt_ref, v_ref, mask_ref, o_ref,
                 m_ref, l_ref, acc_ref, v2_ref, *, scale):
    t = ATT_T
    i = pl.program_id(2)
    m_ref[...] = jnp.full(m_ref.shape, -jnp.inf, F32)
    l_ref[...] = jnp.zeros(l_ref.shape, F32)
    acc_ref[...] = jnp.zeros(acc_ref.shape, F32)
    v2_ref[...] = jnp.zeros(v2_ref.shape, BF16)
    q = q_ref[0]

    def body(kt, carry):
        k_t = kt_ref[0, kt]
        v = v_ref[0, pl.ds(pl.multiple_of(kt * t, t), t), :]
        sel = jnp.concatenate(
            [mask_ref[0, a, kt] for a in range(t // SEL_TQ)], axis=0).astype(I32) != 0
        ps, alphas = [], []
        for hh in range(2):
            hs = slice(hh * HEAD_DIM, (hh + 1) * HEAD_DIM)
            s = jnp.dot(q[:, hs], k_t[hs, :], preferred_element_type=F32) * scale
            s = jnp.where(sel, s, MASKED_LOGIT)
            m_prev = m_ref[hh]
            m_new = jnp.maximum(m_prev, jnp.max(s, axis=1, keepdims=True))
            p = jnp.exp(s - jnp.concatenate([m_new] * (t // LANES), axis=1))
            alpha = jnp.exp(m_prev - m_new)
            l_ref[hh] = alpha * l_ref[hh] + jnp.sum(p, axis=1, keepdims=True)
            m_ref[hh] = m_new
            ps.append(p.astype(BF16))
            alphas.append(alpha)
        v2_ref[0:t, 0:HEAD_DIM] = v[:, 0:HEAD_DIM]
        v2_ref[t:2 * t, HEAD_DIM:2 * HEAD_DIM] = v[:, HEAD_DIM:2 * HEAD_DIM]
        pv = jnp.dot(jnp.concatenate(ps, axis=1), v2_ref[...],
                     preferred_element_type=F32)
        acc_ref[...] = jnp.concatenate(alphas, axis=1) * acc_ref[...] + pv
        return carry

    lax.fori_loop(0, i + 1, body, 0)
    denom = jnp.concatenate([l_ref[0], l_ref[1]], axis=1)
    o_ref[0] = (acc_ref[...] / denom).astype(BF16)


def _attn(qkv, k_t, mask):
    b, s, _ = qkv.shape
    t = ATT_T
    nq = s // t
    pairs = N_HEADS // 2
    v_col0 = 2 * ATT_WIDTH // (2 * HEAD_DIM)
    return pl.pallas_call(
        functools.partial(_attn_kernel, scale=HEAD_DIM ** -0.5),
        grid=(b, pairs, nq),
        in_specs=[
            pl.BlockSpec((1, t, 2 * HEAD_DIM), lambda bi, hp, i: (bi, i, hp)),
            pl.BlockSpec((1, nq, 2 * HEAD_DIM, t), lambda bi, hp, i: (bi, 0, hp, 0)),
            pl.BlockSpec((1, s, 2 * HEAD_DIM), lambda bi, hp, i: (bi, 0, v_col0 + hp)),
            pl.BlockSpec((1, t // SEL_TQ, nq, SEL_TQ, SEL_TK),
                         lambda bi, hp, i: (bi, i, 0, 0, 0)),
        ],
        out_specs=pl.BlockSpec((1, t, 2 * HEAD_DIM), lambda bi, hp, i: (bi, i, hp)),
        out_shape=jax.ShapeDtypeStruct((b, s, ATT_WIDTH), BF16),
        scratch_shapes=[
            pltpu.VMEM((2, t, LANES), F32),
            pltpu.VMEM((2, t, LANES), F32),
            pltpu.VMEM((t, 2 * HEAD_DIM), F32),
            pltpu.VMEM((2 * t, 2 * HEAD_DIM), BF16),
        ],
        compiler_params=_params("parallel", "parallel", "arbitrary"),
        name="masked_attention",
    )(qkv, k_t, qkv, mask)


def _merge_kernel(ya_ref, yb_ref, g_ref, x_ref, wa_ref, wb_ref, wo_ref, o_ref):
    d = x_ref.shape[1]
    a = jnp.dot(ya_ref[...], wa_ref[...], preferred_element_type=F32)
    b = jnp.dot(yb_ref[...], wb_ref[...], preferred_element_type=F32)
    merged = g_ref[:, :d].astype(F32) * a + g_ref[:, d:].astype(F32) * b
    o_ref[...] = x_ref[...] + jnp.dot(merged.astype(BF16), wo_ref[...],
                                      preferred_element_type=F32)


def _merge(ya, yb, gates, x, wa, wb, wo, *, tm=256):
    n, d = x.shape
    tm = min(tm, n)
    return pl.pallas_call(
        _merge_kernel,
        grid=(n // tm,),
        in_specs=[
            pl.BlockSpec((tm, ya.shape[1]), lambda i: (i, 0)),
            pl.BlockSpec((tm, yb.shape[1]), lambda i: (i, 0)),
            pl.BlockSpec((tm, 2 * d), lambda i: (i, 0)),
            pl.BlockSpec((tm, d), lambda i: (i, 0)),
            pl.BlockSpec(wa.shape, lambda i: (0, 0)),
            pl.BlockSpec(wb.shape, lambda i: (0, 0)),
            pl.BlockSpec(wo.shape, lambda i: (0, 0)),
        ],
        out_specs=pl.BlockSpec((tm, d), lambda i: (i, 0)),
        out_shape=jax.ShapeDtypeStruct((n, d), F32),
        compiler_params=_params("parallel"),
        name="merge_out",
    )(ya, yb, gates, x, wa, wb, wo)


def _inv_freq_lanes(rot_dim, period, live):
    inv = ROPE_THETA ** (-jnp.arange(0, rot_dim, 2, dtype=F32) / rot_dim)
    lane = jnp.arange(LANES)
    j = lane % period
    vals = inv[j % (rot_dim // 2)]
    return jnp.where((j < rot_dim) & (lane < live), vals, 0.0).astype(F32)[None, :]


def _layer(x, pos, invf_att, invf_idx, p):
    b, s, d = x.shape
    n = b * s
    topk = min(TOPK_MAX, s // 4)
    x = x.reshape(n, d)

    x1, h = _ffn(x, p["norm_ffn1"][None], p["ffn1_w_gate"].astype(BF16),
                 p["ffn1_w_up"].astype(BF16), p["ffn1_w_down"].astype(BF16),
                 p["norm_mix"][None], emit_norm=True)

    w_in = p["w_in"]
    c_qkv = 3 * ATT_WIDTH
    c_qi = c_qkv + IDX_HEADS * IDX_DIM
    c_idx = c_qi + IDX_DIM + IDX_HEADS
    c_sgu = c_idx + 2 * SGU_WIDTH
    w_qkv = w_in[:, :c_qkv].astype(BF16)
    w_idx = jnp.pad(w_in[:, c_qkv:c_idx],
                    ((0, 0), (0, LANES - IDX_DIM - IDX_HEADS))).astype(BF16)
    w_sgu = w_in[:, c_idx:c_sgu].astype(BF16)
    w_gate = w_in[:, c_sgu:].astype(BF16)

    gains = jnp.stack([p["q_norm"], p["k_norm"]])[:, None, :]
    qkv = _qkv(h, w_qkv, pos, invf_att, gains)
    kgain = jnp.pad(p["idx_k_norm"], (0, LANES - IDX_DIM))[None]
    qi, kcat, wi = _idx(h, w_idx, pos, invf_idx, kgain)
    yb = _sgu(h, w_sgu, p["sgu_v_norm"][None], p["sgu_w_s"], p["sgu_b_s"][:, :, None])
    gates = _gates(h, w_gate)

    kcat_t = kcat.reshape(b, s // SEL_TK, SEL_TK, 4 * IDX_DIM).transpose(0, 1, 3, 2)
    mask = _select(qi.reshape(b, s, -1), wi.reshape(b, s, LANES), kcat_t, topk=topk)
    qkv3 = qkv.reshape(b, s, c_qkv)
    k_t = qkv3[:, :, ATT_WIDTH:2 * ATT_WIDTH].reshape(
        b, s // ATT_T, ATT_T, ATT_WIDTH).transpose(0, 1, 3, 2)
    ya = _attn(qkv3, k_t, mask).reshape(n, ATT_WIDTH)

    x2 = _merge(ya, yb, gates, x1, p["w_up_attn"].astype(BF16),
                p["w_up_sgu"].astype(BF16), p["w_out"].astype(BF16))
    out = _ffn(x2, p["norm_ffn2"][None], p["ffn2_w_gate"].astype(BF16),
               p["ffn2_w_up"].astype(BF16), p["ffn2_w_down"].astype(BF16),
               p["norm_ffn2"][None], emit_norm=False)
    return out.reshape(b, s, d)


def kernel(x, positions, norm_ffn1, ffn1_w_gate, ffn1_w_up, ffn1_w_down, norm_mix, w_in,
           q_norm, k_norm, idx_k_norm, sgu_v_norm, sgu_w_s, sgu_b_s, w_up_attn, w_up_sgu,
           w_out, norm_ffn2, ffn2_w_gate, ffn2_w_up, ffn2_w_down):
    params = dict(
        norm_ffn1=norm_ffn1, ffn1_w_gate=ffn1_w_gate, ffn1_w_up=ffn1_w_up,
        ffn1_w_down=ffn1_w_down, norm_mix=norm_mix, w_in=w_in, q_norm=q_norm,
        k_norm=k_norm, idx_k_norm=idx_k_norm, sgu_v_norm=sgu_v_norm, sgu_w_s=sgu_w_s,
        sgu_b_s=sgu_b_s, w_up_attn=w_up_attn, w_up_sgu=w_up_sgu, w_out=w_out,
        norm_ffn2=norm_ffn2, ffn2_w_gate=ffn2_w_gate, ffn2_w_up=ffn2_w_up,
        ffn2_w_down=ffn2_w_down)
    pos = positions.reshape(-1, 1).astype(I32)
    invf_att = _inv_freq_lanes(ROT_DIM, LANES, LANES)
    invf_idx = _inv_freq_lanes(IDX_ROT_DIM, IDX_DIM, LANES)
    for l in range(w_in.shape[0]):
        x = _layer(x, pos, invf_att, invf_idx, {k: v[l] for k, v in params.items()})
    return x
```

```python
import functools

import jax
import jax.numpy as jnp
from jax import lax
from jax.experimental import pallas as pl
from jax.experimental.pallas import tpu as pltpu

F32 = jnp.float32
BF16 = jnp.bfloat16
I32 = jnp.int32

EPS = 1e-6
CHUNK = 64
N_HEADS = 8
HEAD_DIM = 128
ATT_WIDTH = N_HEADS * HEAD_DIM
ROPE_THETA = 500000.0
ROT_DIM = HEAD_DIM // 4
IDX_HEADS = 16
IDX_DIM = 64
IDX_ROT_DIM = IDX_DIM // 4
TOPK_MAX = 256
SGU_LEN = 128
SGU_GROUPS = 8
SGU_GROUP_DIM = 128
SGU_WIDTH = SGU_GROUPS * SGU_GROUP_DIM

LANES = 128
MXU_DIM = 256
VMEM_LIMIT_BYTES = 56 * 1024 * 1024

SEL_T = MXU_DIM
MASK_ROWS = 128
COUNT_ROWS = 64
ATT_T = 256
MASKED_LOGIT = -1e30


def _params(*sem):
    return pltpu.CompilerParams(dimension_semantics=sem,
                                vmem_limit_bytes=VMEM_LIMIT_BYTES)


def _rms(t, gain):
    ms = jnp.mean(t * t, axis=-1, keepdims=True)
    return t * lax.rsqrt(ms + EPS) * gain


def _rope_tables(pos_ref, invf_ref, half, period):
    ang = pos_ref[...].astype(F32) * invf_ref[...]
    c = jnp.cos(ang)
    s = jnp.sin(ang)
    lane = lax.broadcasted_iota(I32, ang.shape, 1) & (period - 1)
    return c, jnp.where(lane < half, -s, 0.0), jnp.where(lane >= half, s, 0.0)


def _rope(t, c, s1, s2, half):
    return (t * c + pltpu.roll(t, LANES - half, 1) * s1
            + pltpu.roll(t, half, 1) * s2)


def _ffn_kernel(x_ref, g_ref, wg_ref, wu_ref, wd_ref, g2_ref, o_ref, *rest,
                emit_norm):
    if emit_norm:
        h_out_ref, hn_ref = rest
    else:
        (hn_ref,) = rest
    j = pl.program_id(1)

    @pl.when(j == 0)
    def _():
        hn_ref[...] = _rms(x_ref[...], g_ref[...]).astype(BF16)
        o_ref[...] = jnp.zeros_like(o_ref)

    h = hn_ref[...]
    a = jnp.dot(h, wg_ref[...], preferred_element_type=F32)
    b = jnp.dot(h, wu_ref[...], preferred_element_type=F32)
    act = (a * jax.nn.sigmoid(a) * b).astype(BF16)
    o_ref[...] += jnp.dot(act, wd_ref[...], preferred_element_type=F32)

    @pl.when(j == pl.num_programs(1) - 1)
    def _():
        y = x_ref[...] + 0.5 * o_ref[...]
        o_ref[...] = y
        if emit_norm:
            h_out_ref[...] = _rms(y, g2_ref[...]).astype(BF16)


def _ffn(x, gain, w_gate, w_up, w_down, gain2, *, emit_norm, tm=512, tf=512):
    n, d = x.shape
    f = w_gate.shape[1]
    tm = min(tm, n)
    out_shape = [jax.ShapeDtypeStruct((n, d), F32)]
    out_specs = [pl.BlockSpec((tm, d), lambda i, j: (i, 0))]
    if emit_norm:
        out_shape.append(jax.ShapeDtypeStruct((n, d), BF16))
        out_specs.append(pl.BlockSpec((tm, d), lambda i, j: (i, 0)))
    res = pl.pallas_call(
        functools.partial(_ffn_kernel, emit_norm=emit_norm),
        grid=(n // tm, f // tf),
        in_specs=[
            pl.BlockSpec((tm, d), lambda i, j: (i, 0)),
            pl.BlockSpec((1, d), lambda i, j: (0, 0)),
            pl.BlockSpec((d, tf), lambda i, j: (0, j)),
            pl.BlockSpec((d, tf), lambda i, j: (0, j)),
            pl.BlockSpec((tf, d), lambda i, j: (j, 0)),
            pl.BlockSpec((1, d), lambda i, j: (0, 0)),
        ],
        out_specs=out_specs,
        out_shape=out_shape,
        scratch_shapes=[pltpu.VMEM((tm, d), BF16)],
        compiler_params=_params("parallel", "arbitrary"),
        name="ffn_norm" if emit_norm else "ffn",
    )(x, gain, w_gate, w_up, w_down, gain2)
    return res if emit_norm else res[0]


def _qkv_kernel(h_ref, w_ref, pos_ref, invf_ref, gains_ref, o_ref,
                c_ref, s1_ref, s2_ref, *, rope_tiles):
    j = pl.program_id(1)
    half = ROT_DIM // 2

    @pl.when(j == 0)
    def _():
        c, s1, s2 = _rope_tables(pos_ref, invf_ref, half, LANES)
        c_ref[...] = c
        s1_ref[...] = s1
        s2_ref[...] = s2

    res = jnp.dot(h_ref[...], w_ref[...], preferred_element_type=F32)

    @pl.when(j < rope_tiles)
    def _():
        gain = gains_ref[0]
        for hh in range(res.shape[1] // HEAD_DIM):
            sl = slice(hh * HEAD_DIM, (hh + 1) * HEAD_DIM)
            t = _rms(res[:, sl], gain)
            t = _rope(t, c_ref[...], s1_ref[...], s2_ref[...], half)
            o_ref[:, sl] = t.astype(BF16)

    @pl.when(j >= rope_tiles)
    def _():
        o_ref[...] = res.astype(BF16)


def _qkv(h, w_qkv, pos, invf, gains, *, tm=1024, tn=512):
    n, d = h.shape
    cols = w_qkv.shape[1]
    tm = min(tm, n)
    per = ATT_WIDTH // tn
    return pl.pallas_call(
        functools.partial(_qkv_kernel, rope_tiles=2 * per),
        grid=(n // tm, cols // tn),
        in_specs=[
            pl.BlockSpec((tm, d), lambda i, j: (i, 0)),
            pl.BlockSpec((d, tn), lambda i, j: (0, j)),
            pl.BlockSpec((tm, 1), lambda i, j: (i, 0)),
            pl.BlockSpec((1, LANES), lambda i, j: (0, 0)),
            pl.BlockSpec((1, 1, HEAD_DIM),
                         lambda i, j: (jnp.minimum(j // per, 1), 0, 0)),
        ],
        out_specs=pl.BlockSpec((tm, tn), lambda i, j: (i, j)),
        out_shape=jax.ShapeDtypeStruct((n, cols), BF16),
        scratch_shapes=[pltpu.VMEM((tm, LANES), F32)] * 3,
        compiler_params=_params("parallel", "arbitrary"),
        name="qkv_proj",
    )(h, w_qkv, pos, invf, gains)


def _idx_kernel(h_ref, w_ref, pos_ref, invf_ref, kgain_ref,
                qi_ref, kcat_ref, wi_ref, *, w_scale):
    half = IDX_ROT_DIM // 2
    qw = IDX_HEADS * IDX_DIM
    res = jnp.dot(h_ref[...], w_ref[...], preferred_element_type=F32)
    c, s1, s2 = _rope_tables(pos_ref, invf_ref, half, IDX_DIM)
    for t in range(qw // LANES):
        sl = slice(t * LANES, (t + 1) * LANES)
        qi_ref[:, sl] = _rope(res[:, sl], c, s1, s2, half)

    r = res[:, qw:qw + LANES]
    lane = lax.broadcasted_iota(I32, r.shape, 1)
    is_k = lane < IDX_DIM
    ms = jnp.sum(jnp.where(is_k, r * r, 0.0), axis=-1, keepdims=True) * (1.0 / IDX_DIM)
    kn = r * lax.rsqrt(ms + EPS) * kgain_ref[...]
    kr = _rope(kn, jnp.where(is_k, c, 1.0), jnp.where(is_k, s1, 0.0),
               jnp.where(is_k, s2, 0.0), half)
    hi = kr.astype(BF16).astype(F32)
    lo = kr - hi
    hi_lo = jnp.where(is_k, hi, pltpu.roll(lo, IDX_DIM, 1)).astype(BF16)
    kcat_ref[:, 0:LANES] = hi_lo
    kcat_ref[:, LANES:2 * LANES] = hi_lo
    is_w = (lane >= IDX_DIM) & (lane < IDX_DIM + IDX_HEADS)
    wi_ref[...] = pltpu.roll(jnp.where(is_w, r * w_scale, 0.0), LANES - IDX_DIM, 1)


def _idx(h, w_idx, pos, invf, kgain, *, tm=512):
    n, d = h.shape
    cols = w_idx.shape[1]
    qw = IDX_HEADS * IDX_DIM
    tm = min(tm, n)
    w_scale = (IDX_HEADS ** -0.5) * (IDX_DIM ** -0.5)
    return pl.pallas_call(
        functools.partial(_idx_kernel, w_scale=w_scale),
        grid=(n // tm,),
        in_specs=[
            pl.BlockSpec((tm, d), lambda i: (i, 0)),
            pl.BlockSpec((d, cols), lambda i: (0, 0)),
            pl.BlockSpec((tm, 1), lambda i: (i, 0)),
            pl.BlockSpec((1, LANES), lambda i: (0, 0)),
            pl.BlockSpec((1, LANES), lambda i: (0, 0)),
        ],
        out_specs=[
            pl.BlockSpec((tm, qw), lambda i: (i, 0)),
            pl.BlockSpec((tm, 2 * LANES), lambda i: (i, 0)),
            pl.BlockSpec((tm, LANES), lambda i: (i, 0)),
        ],
        out_shape=[
            jax.ShapeDtypeStruct((n, qw), F32),
            jax.ShapeDtypeStruct((n, 2 * LANES), BF16),
            jax.ShapeDtypeStruct((n, LANES), F32),
        ],
        compiler_params=_params("parallel"),
        name="idx_proj",
    )(h, w_idx, pos, invf, kgain)


def _sgu_kernel(h_ref, w_ref, gv_ref, ws_ref, bs_ref, o_ref):
    tm = h_ref.shape[0]
    groups = tm // SGU_LEN
    res = jnp.dot(h_ref[...], w_ref[...], preferred_element_type=F32)
    u = jax.nn.gelu(res[:, :SGU_WIDTH])
    v = _rms(jax.nn.gelu(res[:, SGU_WIDTH:]), gv_ref[...]).astype(BF16)
    row = lax.broadcasted_iota(I32, (SGU_LEN, SGU_LEN), 0)
    col = lax.broadcasted_iota(I32, (SGU_LEN, SGU_LEN), 1)
    causal = (col // CHUNK) <= (row // CHUNK)
    for g in range(SGU_GROUPS):
        cs = slice(g * SGU_GROUP_DIM, (g + 1) * SGU_GROUP_DIM)
        wg = jnp.where(causal, ws_ref[g], 0.0).astype(BF16)
        vg = jnp.concatenate(
            [v[n * SGU_LEN:(n + 1) * SGU_LEN, cs] for n in range(groups)], axis=1)
        mixed = jnp.dot(wg, vg, preferred_element_type=F32) + bs_ref[g]
        for n in range(groups):
            rs = slice(n * SGU_LEN, (n + 1) * SGU_LEN)
            m = mixed[:, n * SGU_GROUP_DIM:(n + 1) * SGU_GROUP_DIM]
            o_ref[rs, cs] = (u[rs, cs] * m).astype(BF16)


def _sgu(h, w_sgu, gv, w_s, b_s, *, tm=512):
    n, d = h.shape
    tm = min(tm, n)
    return pl.pallas_call(
        _sgu_kernel,
        grid=(n // tm,),
        in_specs=[
            pl.BlockSpec((tm, d), lambda i: (i, 0)),
            pl.BlockSpec((d, 2 * SGU_WIDTH), lambda i: (0, 0)),
            pl.BlockSpec((1, SGU_WIDTH), lambda i: (0, 0)),
            pl.BlockSpec((SGU_GROUPS, SGU_LEN, SGU_LEN), lambda i: (0, 0, 0)),
            pl.BlockSpec((SGU_GROUPS, SGU_LEN, 1), lambda i: (0, 0, 0)),
        ],
        out_specs=pl.BlockSpec((tm, SGU_WIDTH), lambda i: (i, 0)),
        out_shape=jax.ShapeDtypeStruct((n, SGU_WIDTH), BF16),
        compiler_params=_params("parallel"),
        name="sgu_branch",
    )(h, w_sgu, gv, w_s, b_s)


def _gates_kernel(h_ref, w_ref, o_ref):
    res = jnp.dot(h_ref[...], w_ref[...], preferred_element_type=F32)
    o_ref[...] = jax.nn.sigmoid(res).astype(BF16)


def _gates(h, w_g, *, tm=1024, tn=1024):
    n, d = h.shape
    cols = w_g.shape[1]
    tm = min(tm, n)
    return pl.pallas_call(
        _gates_kernel,
        grid=(n // tm, cols // tn),
        in_specs=[
            pl.BlockSpec((tm, d), lambda i, j: (i, 0)),
            pl.BlockSpec((d, tn), lambda i, j: (0, j)),
        ],
        out_specs=pl.BlockSpec((tm, tn), lambda i, j: (i, j)),
        out_shape=jax.ShapeDtypeStruct((n, cols), BF16),
        compiler_params=_params("parallel", "arbitrary"),
        name="gates_proj",
    )(h, w_g)


def _select_kernel(qi_ref, wi_ref, kcat_ref, mask_ref, qcat_ref, s_ref, *, topk):
    t = SEL_T
    n_tiles = s_ref.shape[0]
    i = pl.program_id(1)
    n_live = i + 1

    for h in range(IDX_HEADS):
        q = qi_ref[0, h]
        hi = q.astype(BF16)
        lo = (q - hi.astype(F32)).astype(BF16)
        qcat_ref[h] = jnp.concatenate([hi, hi, lo, lo], axis=0)
    w = wi_ref[0]

    q_chunk = (i * t + lax.broadcasted_iota(I32, (t, t), 1)) // CHUNK
    k_row = lax.broadcasted_iota(I32, (t, t), 0)

    def score_body(kt, carry):
        mx, mn = carry
        kc = kcat_ref[0, pl.ds(pl.multiple_of(kt * t, t), t), :]
        acc = jnp.zeros((t, t), F32)
        for h in range(IDX_HEADS):
            d = jnp.dot(kc, qcat_ref[h], preferred_element_type=F32)
            acc = acc + jnp.maximum(d, 0.0) * w[h:h + 1, :]
        admissible = ((kt * t + k_row) // CHUNK) <= q_chunk
        s = jnp.where(admissible, acc, -jnp.inf)
        s_ref[kt] = s
        mx = jnp.maximum(mx, jnp.max(s, axis=0, keepdims=True))
        mn = jnp.minimum(mn, jnp.min(jnp.where(admissible, acc, jnp.inf),
                                     axis=0, keepdims=True))
        return mx, mn

    mx, mn = lax.fori_loop(
        0, n_live, score_body,
        (jnp.full((1, t), -jnp.inf, F32), jnp.full((1, t), jnp.inf, F32)))

    n_adm = ((i * t + lax.broadcasted_iota(I32, (1, t), 1)) // CHUNK + 1) * CHUNK
    done0 = n_adm <= topk

    def count_ge(c):
        def count_body(kt, acc):
            ind = jnp.where(s_ref[kt] >= c, 1.0, 0.0)
            for r in range(t // COUNT_ROWS):
                acc = acc + ind[r * COUNT_ROWS:(r + 1) * COUNT_ROWS]
            return acc
        acc = lax.fori_loop(0, n_live, count_body, jnp.zeros((COUNT_ROWS, t), F32))
        return jnp.sum(acc, axis=0, keepdims=True)

    def search_cond(st):
        return st[5] > 0.0

    def search_body(st):
        lo, hi, thr, live, first, _ = st
        mid = jnp.where(first > 0.0, hi, 0.5 * lo + 0.5 * hi)
        stuck = (first == 0.0) & ((mid <= lo) | (mid >= hi))
        cnt = count_ge(mid)
        finish = (live > 0.0) & (stuck | (cnt == topk))
        thr = jnp.where(finish, jnp.where(stuck, lo, mid), thr)
        live = jnp.where(finish, 0.0, live)
        lo = jnp.where(cnt > topk, mid, lo)
        hi = jnp.where(cnt < topk, mid, hi)
        return lo, hi, thr, live, jnp.float32(0.0), jnp.sum(live)

    live0 = jnp.where(done0, 0.0, 1.0)
    st0 = (mn, mx, jnp.full((1, t), -jnp.inf, F32), live0, jnp.float32(1.0),
           jnp.sum(live0))
    thr = lax.while_loop(search_cond, search_body, st0)[2]

    def mask_body(kt, carry):
        s = s_ref[kt]
        sel = jnp.where((s >= thr) & (s > -jnp.inf), 1.0, 0.0).T
        for a in range(t // MASK_ROWS):
            mask_ref[0, a, kt] = sel[a * MASK_ROWS:(a + 1) * MASK_ROWS].astype(jnp.int8)
        return carry

    lax.fori_loop(0, n_live, mask_body, 0)

    def zero_body(kt, carry):
        for a in range(t // MASK_ROWS):
            mask_ref[0, a, kt] = jnp.zeros((MASK_ROWS, t), jnp.int8)
        return carry

    lax.fori_loop(n_live, n_tiles, zero_body, 0)


def _select(qi_t, wi_t, kcat, *, topk):
    b, heads, dh, s = qi_t.shape
    t = SEL_T
    nq = s // t
    return pl.pallas_call(
        functools.partial(_select_kernel, topk=topk),
        grid=(b, nq),
        in_specs=[
            pl.BlockSpec((1, heads, dh, t), lambda bi, i: (bi, 0, 0, i)),
            pl.BlockSpec((1, heads, t), lambda bi, i: (bi, 0, i)),
            pl.BlockSpec((1, s, 4 * IDX_DIM), lambda bi, i: (bi, 0, 0)),
        ],
        out_specs=pl.BlockSpec((1, t // MASK_ROWS, nq, MASK_ROWS, t),
                               lambda bi, i: (bi, i, 0, 0, 0)),
        out_shape=jax.ShapeDtypeStruct((b, s // MASK_ROWS, nq, MASK_ROWS, t), jnp.int8),
        scratch_shapes=[
            pltpu.VMEM((heads, 4 * IDX_DIM, t), BF16),
            pltpu.VMEM((nq, t, t), F32),
        ],
        compiler_params=_params("parallel", "arbitrary"),
        name="index_select",
    )(qi_t, wi_t, kcat)


def _attn_kernel(q_ref, kt_ref, v_ref, mask_ref, o_ref,
                 m_ref, l_ref, acc_ref, v2_ref, *, scale):
    t = ATT_T
    i = pl.program_id(2)
    m_ref[...] = jnp.full(m_ref.shape, -jnp.inf, F32)
    l_ref[...] = jnp.zeros(l_ref.shape, F32)
    acc_ref[...] = jnp.zeros(acc_ref.shape, F32)
    v2_ref[...] = jnp.zeros(v2_ref.shape, BF16)
    q = q_ref[0]

    def body(kt, carry):
        k_t = kt_ref[0, kt]
        v = v_ref[0, pl.ds(pl.multiple_of(kt * t, t), t), :]
        sel = jnp.concatenate(
            [mask_ref[0, a, kt] for a in range(t // MASK_ROWS)], axis=0).astype(I32) != 0
        ps, alphas = [], []
        for hh in range(2):
            hs = slice(hh * HEAD_DIM, (hh + 1) * HEAD_DIM)
            s = jnp.dot(q[:, hs], k_t[hs, :], preferred_element_type=F32) * scale
            s = jnp.where(sel, s, MASKED_LOGIT)
            m_prev = m_ref[hh]
            m_new = jnp.maximum(m_prev, jnp.max(s, axis=1, keepdims=True))
            p = jnp.exp(s - jnp.concatenate([m_new] * (t // LANES), axis=1))
            alpha = jnp.exp(m_prev - m_new)
            l_ref[hh] = alpha * l_ref[hh] + jnp.sum(p, axis=1, keepdims=True)
            m_ref[hh] = m_new
            ps.append(p.astype(BF16))
            alphas.append(alpha)
        v2_ref[0:t, 0:HEAD_DIM] = v[:, 0:HEAD_DIM]
        v2_ref[t:2 * t, HEAD_DIM:2 * HEAD_DIM] = v[:, HEAD_DIM:2 * HEAD_DIM]
        pv = jnp.dot(jnp.concatenate(ps, axis=1), v2_ref[...],
                     preferred_element_type=F32)
        acc_ref[...] = jnp.concatenate(alphas, axis=1) * acc_ref[...] + pv
        return carry

    lax.fori_loop(0, i + 1, body, 0)
    denom = jnp.concatenate([l_ref[0], l_ref[1]], axis=1)
    o_ref[0] = (acc_ref[...] / denom).astype(BF16)


def _attn(qkv, k_t, mask):
    b, s, _ = qkv.shape
    t = ATT_T
    nq = s // t
    pairs = N_HEADS // 2
    v_col0 = 2 * ATT_WIDTH // (2 * HEAD_DIM)
    return pl.pallas_call(
        functools.partial(_attn_kernel, scale=HEAD_DIM ** -0.5),
        grid=(b, pairs, nq),
        in_specs=[
            pl.BlockSpec((1, t, 2 * HEAD_DIM), lambda bi, hp, i: (bi, i, hp)),
            pl.BlockSpec((1, nq, 2 * HEAD_DIM, t), lambda bi, hp, i: (bi, 0, hp, 0)),
            pl.BlockSpec((1, s, 2 * HEAD_DIM), lambda bi, hp, i: (bi, 0, v_col0 + hp)),
            pl.BlockSpec((1, t // MASK_ROWS, nq, MASK_ROWS, SEL_T),
                         lambda bi, hp, i: (bi, i, 0, 0, 0)),
        ],
        out_specs=pl.BlockSpec((1, t, 2 * HEAD_DIM), lambda bi, hp, i: (bi, i, hp)),
        out_shape=jax.ShapeDtypeStruct((b, s, ATT_WIDTH), BF16),
        scratch_shapes=[
            pltpu.VMEM((2, t, LANES), F32),
            pltpu.VMEM((2, t, LANES), F32),
            pltpu.VMEM((t, 2 * HEAD_DIM), F32),
            pltpu.VMEM((2 * t, 2 * HEAD_DIM), BF16),
        ],
        compiler_params=_params("parallel", "parallel", "arbitrary"),
        name="masked_attention",
    )(qkv, k_t, qkv, mask)


def _merge_kernel(ya_ref, yb_ref, g_ref, x_ref, wa_ref, wb_ref, wo_ref, o_ref):
    d = x_ref.shape[1]
    a = jnp.dot(ya_ref[...], wa_ref[...], preferred_element_type=F32)
    b = jnp.dot(yb_ref[...], wb_ref[...], preferred_element_type=F32)
    merged = g_ref[:, :d].astype(F32) * a + g_ref[:, d:].astype(F32) * b
    o_ref[...] = x_ref[...] + jnp.dot(merged.astype(BF16), wo_ref[...],
                                      preferred_element_type=F32)


def _merge(ya, yb, gates, x, wa, wb, wo, *, tm=256):
    n, d = x.shape
    tm = min(tm, n)
    return pl.pallas_call(
        _merge_kernel,
        grid=(n // tm,),
        in_specs=[
            pl.BlockSpec((tm, ya.shape[1]), lambda i: (i, 0)),
            pl.BlockSpec((tm, yb.shape[1]), lambda i: (i, 0)),
            pl.BlockSpec((tm, 2 * d), lambda i: (i, 0)),
            pl.BlockSpec((tm, d), lambda i: (i, 0)),
            pl.BlockSpec(wa.shape, lambda i: (0, 0)),
            pl.BlockSpec(wb.shape, lambda i: (0, 0)),
            pl.BlockSpec(wo.shape, lambda i: (0, 0)),
        ],
        out_specs=pl.BlockSpec((tm, d), lambda i: (i, 0)),
        out_shape=jax.ShapeDtypeStruct((n, d), F32),
        compiler_params=_params("parallel"),
        name="merge_out",
    )(ya, yb, gates, x, wa, wb, wo)


def _inv_freq_lanes(rot_dim, period, live):
    inv = ROPE_THETA ** (-jnp.arange(0, rot_dim, 2, dtype=F32) / rot_dim)
    lane = jnp.arange(LANES)
    j = lane % period
    vals = inv[j % (rot_dim // 2)]
    return jnp.where((j < rot_dim) & (lane < live), vals, 0.0).astype(F32)[None, :]


def _layer(x, pos, invf_att, invf_idx, p):
    b, s, d = x.shape
    n = b * s
    topk = min(TOPK_MAX, s // 4)
    x = x.reshape(n, d)

    x1, h = _ffn(x, p["norm_ffn1"][None], p["ffn1_w_gate"].astype(BF16),
                 p["ffn1_w_up"].astype(BF16), p["ffn1_w_down"].astype(BF16),
                 p["norm_mix"][None], emit_norm=True)

    w_in = p["w_in"]
    c_qkv = 3 * ATT_WIDTH
    c_qi = c_qkv + IDX_HEADS * IDX_DIM
    c_idx = c_qi + IDX_DIM + IDX_HEADS
    c_sgu = c_idx + 2 * SGU_WIDTH
    w_qkv = w_in[:, :c_qkv].astype(BF16)
    w_idx = jnp.pad(w_in[:, c_qkv:c_idx],
                    ((0, 0), (0, LANES - IDX_DIM - IDX_HEADS))).astype(BF16)
    w_sgu = w_in[:, c_idx:c_sgu].astype(BF16)
    w_gate = w_in[:, c_sgu:].astype(BF16)

    gains = jnp.stack([p["q_norm"], p["k_norm"]])[:, None, :]
    qkv = _qkv(h, w_qkv, pos, invf_att, gains)
    kgain = jnp.pad(p["idx_k_norm"], (0, LANES - IDX_DIM))[None]
    qi, kcat, wi = _idx(h, w_idx, pos, invf_idx, kgain)
    yb = _sgu(h, w_sgu, p["sgu_v_norm"][None], p["sgu_w_s"], p["sgu_b_s"][:, :, None])
    gates = _gates(h, w_gate)

    qi_t = qi.reshape(b, s, IDX_HEADS, IDX_DIM).transpose(0, 2, 3, 1)
    wi_t = wi.reshape(b, s, LANES)[:, :, :IDX_HEADS].transpose(0, 2, 1)
    mask = _select(qi_t, wi_t, kcat.reshape(b, s, 4 * IDX_DIM), topk=topk)
    qkv3 = qkv.reshape(b, s, c_qkv)
    k_t = qkv3[:, :, ATT_WIDTH:2 * ATT_WIDTH].reshape(
        b, s // ATT_T, ATT_T, ATT_WIDTH).transpose(0, 1, 3, 2)
    ya = _attn(qkv3, k_t, mask).reshape(n, ATT_WIDTH)

    x2 = _merge(ya, yb, gates, x1, p["w_up_attn"].astype(BF16),
                p["w_up_sgu"].astype(BF16), p["w_out"].astype(BF16))
    out = _ffn(x2, p["norm_ffn2"][None], p["ffn2_w_gate"].astype(BF16),
               p["ffn2_w_up"].astype(BF16), p["ffn2_w_down"].astype(BF16),
               p["norm_ffn2"][None], emit_norm=False)
    return out.reshape(b, s, d)


def kernel(x, positions, norm_ffn1, ffn1_w_gate, ffn1_w_up, ffn1_w_down, norm_mix, w_in,
           q_norm, k_norm, idx_k_norm, sgu_v_norm, sgu_w_s, sgu_b_s, w_up_attn, w_up_sgu,
           w_out, norm_ffn2, ffn2_w_gate, ffn2_w_up, ffn2_w_down):
    params = dict(
        norm_ffn1=norm_ffn1, ffn1_w_gate=ffn1_w_gate, ffn1_w_up=ffn1_w_up,
        ffn1_w_down=ffn1_w_down, norm_mix=norm_mix, w_in=w_in, q_norm=q_norm,
        k_norm=k_norm, idx_k_norm=idx_k_norm, sgu_v_norm=sgu_v_norm, sgu_w_s=sgu_w_s,
        sgu_b_s=sgu_b_s, w_up_attn=w_up_attn, w_up_sgu=w_up_sgu, w_out=w_out,
        norm_ffn2=norm_ffn2, ffn2_w_gate=ffn2_w_gate, ffn2_w_up=ffn2_w_up,
        ffn2_w_down=ffn2_w_down)
    pos = positions.reshape(-1, 1).astype(I32)
    invf_att = _inv_freq_lanes(ROT_DIM, LANES, LANES)
    invf_idx = _inv_freq_lanes(IDX_ROT_DIM, IDX_DIM, LANES)
    for l in range(w_in.shape[0]):
        x = _layer(x, pos, invf_att, invf_idx, {k: v[l] for k, v in params.items()})
    return x
```

```python
import functools

import jax
import jax.numpy as jnp
from jax import lax
from jax.experimental import pallas as pl
from jax.experimental.pallas import tpu as pltpu

F32 = jnp.float32
BF16 = jnp.bfloat16
I32 = jnp.int32

EPS = 1e-6
CHUNK = 64
N_HEADS = 8
HEAD_DIM = 128
ATT_WIDTH = N_HEADS * HEAD_DIM
ROPE_THETA = 500000.0
ROT_DIM = HEAD_DIM // 4
IDX_HEADS = 16
IDX_DIM = 64
IDX_ROT_DIM = IDX_DIM // 4
TOPK_MAX = 256
SGU_LEN = 128
SGU_GROUPS = 8
SGU_GROUP_DIM = 128
SGU_WIDTH = SGU_GROUPS * SGU_GROUP_DIM

LANES = 128
MXU_DIM = 256
VMEM_LIMIT_BYTES = 56 * 1024 * 1024

SEL_T = MXU_DIM
MASK_ROWS = 128
COUNT_ROWS = 64
ATT_T = 256
MASKED_LOGIT = -1e30
LOG2_E = 1.4426950408889634


def _params(*sem):
    return pltpu.CompilerParams(dimension_semantics=sem,
                                vmem_limit_bytes=VMEM_LIMIT_BYTES)


def _rms(t, gain):
    ms = jnp.mean(t * t, axis=-1, keepdims=True)
    return t * lax.rsqrt(ms + EPS) * gain


def _rope_tables(pos_ref, invf_ref, half, period):
    ang = pos_ref[...].astype(F32) * invf_ref[...]
    c = jnp.cos(ang)
    s = jnp.sin(ang)
    lane = lax.broadcasted_iota(I32, ang.shape, 1) & (period - 1)
    return c, jnp.where(lane < half, -s, 0.0), jnp.where(lane >= half, s, 0.0)


def _rope(t, c, s1, s2, half):
    return (t * c + pltpu.roll(t, LANES - half, 1) * s1
            + pltpu.roll(t, half, 1) * s2)


def _ffn_kernel(x_ref, g_ref, wg_ref, wu_ref, wd_ref, g2_ref, o_ref, *rest,
                emit_norm):
    if emit_norm:
        h_out_ref, hn_ref = rest
    else:
        (hn_ref,) = rest
    j = pl.program_id(1)

    @pl.when(j == 0)
    def _():
        hn_ref[...] = _rms(x_ref[...], g_ref[...]).astype(BF16)
        o_ref[...] = jnp.zeros_like(o_ref)

    h = hn_ref[...]
    a = jnp.dot(h, wg_ref[...], preferred_element_type=F32)
    b = jnp.dot(h, wu_ref[...], preferred_element_type=F32)
    act = (a * jax.nn.sigmoid(a) * b).astype(BF16)
    o_ref[...] += jnp.dot(act, wd_ref[...], preferred_element_type=F32)

    @pl.when(j == pl.num_programs(1) - 1)
    def _():
        y = x_ref[...] + 0.5 * o_ref[...]
        o_ref[...] = y
        if emit_norm:
            h_out_ref[...] = _rms(y, g2_ref[...]).astype(BF16)


def _ffn(x, gain, w_gate, w_up, w_down, gain2, *, emit_norm, tm=512, tf=512):
    n, d = x.shape
    f = w_gate.shape[1]
    tm = min(tm, n)
    out_shape = [jax.ShapeDtypeStruct((n, d), F32)]
    out_specs = [pl.BlockSpec((tm, d), lambda i, j: (i, 0))]
    if emit_norm:
        out_shape.append(jax.ShapeDtypeStruct((n, d), BF16))
        out_specs.append(pl.BlockSpec((tm, d), lambda i, j: (i, 0)))
    res = pl.pallas_call(
        functools.partial(_ffn_kernel, emit_norm=emit_norm),
        grid=(n // tm, f // tf),
        in_specs=[
            pl.BlockSpec((tm, d), lambda i, j: (i, 0)),
            pl.BlockSpec((1, d), lambda i, j: (0, 0)),
            pl.BlockSpec((d, tf), lambda i, j: (0, j)),
            pl.BlockSpec((d, tf), lambda i, j: (0, j)),
            pl.BlockSpec((tf, d), lambda i, j: (j, 0)),
            pl.BlockSpec((1, d), lambda i, j: (0, 0)),
        ],
        out_specs=out_specs,
        out_shape=out_shape,
        scratch_shapes=[pltpu.VMEM((tm, d), BF16)],
        compiler_params=_params("parallel", "arbitrary"),
        name="ffn_norm" if emit_norm else "ffn",
    )(x, gain, w_gate, w_up, w_down, gain2)
    return res if emit_norm else res[0]


def _qkv_kernel(h_ref, w_ref, pos_ref, invf_ref, gains_ref, o_ref,
                c_ref, s1_ref, s2_ref, *, rope_tiles):
    j = pl.program_id(1)
    half = ROT_DIM // 2

    @pl.when(j == 0)
    def _():
        c, s1, s2 = _rope_tables(pos_ref, invf_ref, half, LANES)
        c_ref[...] = c
        s1_ref[...] = s1
        s2_ref[...] = s2

    res = jnp.dot(h_ref[...], w_ref[...], preferred_element_type=F32)

    @pl.when(j < rope_tiles)
    def _():
        gain = gains_ref[0]
        for hh in range(res.shape[1] // HEAD_DIM):
            sl = slice(hh * HEAD_DIM, (hh + 1) * HEAD_DIM)
            t = _rms(res[:, sl], gain)
            t = _rope(t, c_ref[...], s1_ref[...], s2_ref[...], half)
            o_ref[:, sl] = t.astype(BF16)

    @pl.when(j >= rope_tiles)
    def _():
        o_ref[...] = res.astype(BF16)


def _qkv(h, w_qkv, pos, invf, gains, *, tm=1024, tn=512):
    n, d = h.shape
    cols = w_qkv.shape[1]
    tm = min(tm, n)
    per = ATT_WIDTH // tn
    return pl.pallas_call(
        functools.partial(_qkv_kernel, rope_tiles=2 * per),
        grid=(n // tm, cols // tn),
        in_specs=[
            pl.BlockSpec((tm, d), lambda i, j: (i, 0)),
            pl.BlockSpec((d, tn), lambda i, j: (0, j)),
            pl.BlockSpec((tm, 1), lambda i, j: (i, 0)),
            pl.BlockSpec((1, LANES), lambda i, j: (0, 0)),
            pl.BlockSpec((1, 1, HEAD_DIM),
                         lambda i, j: (jnp.minimum(j // per, 1), 0, 0)),
        ],
        out_specs=pl.BlockSpec((tm, tn), lambda i, j: (i, j)),
        out_shape=jax.ShapeDtypeStruct((n, cols), BF16),
        scratch_shapes=[pltpu.VMEM((tm, LANES), F32)] * 3,
        compiler_params=_params("parallel", "arbitrary"),
        name="qkv_proj",
    )(h, w_qkv, pos, invf, gains)


def _idx_kernel(h_ref, w_ref, pos_ref, invf_ref, kgain_ref,
                qi_ref, kcat_ref, wi_ref, *, w_scale):
    half = IDX_ROT_DIM // 2
    qw = IDX_HEADS * IDX_DIM
    res = jnp.dot(h_ref[...], w_ref[...], preferred_element_type=F32)
    c, s1, s2 = _rope_tables(pos_ref, invf_ref, half, IDX_DIM)
    for t in range(qw // LANES):
        sl = slice(t * LANES, (t + 1) * LANES)
        qi_ref[:, sl] = _rope(res[:, sl], c, s1, s2, half)

    r = res[:, qw:qw + LANES]
    lane = lax.broadcasted_iota(I32, r.shape, 1)
    is_k = lane < IDX_DIM
    ms = jnp.sum(jnp.where(is_k, r * r, 0.0), axis=-1, keepdims=True) * (1.0 / IDX_DIM)
    kn = r * lax.rsqrt(ms + EPS) * kgain_ref[...]
    kr = _rope(kn, jnp.where(is_k, c, 1.0), jnp.where(is_k, s1, 0.0),
               jnp.where(is_k, s2, 0.0), half)
    hi = kr.astype(BF16).astype(F32)
    lo = kr - hi
    hi_lo = jnp.where(is_k, hi, pltpu.roll(lo, IDX_DIM, 1)).astype(BF16)
    kcat_ref[:, 0:LANES] = hi_lo
    kcat_ref[:, LANES:2 * LANES] = hi_lo
    is_w = (lane >= IDX_DIM) & (lane < IDX_DIM + IDX_HEADS)
    wi_ref[...] = pltpu.roll(jnp.where(is_w, r * w_scale, 0.0), LANES - IDX_DIM, 1)


def _idx(h, w_idx, pos, invf, kgain, *, tm=512):
    n, d = h.shape
    cols = w_idx.shape[1]
    qw = IDX_HEADS * IDX_DIM
    tm = min(tm, n)
    w_scale = (IDX_HEADS ** -0.5) * (IDX_DIM ** -0.5)
    return pl.pallas_call(
        functools.partial(_idx_kernel, w_scale=w_scale),
        grid=(n // tm,),
        in_specs=[
            pl.BlockSpec((tm, d), lambda i: (i, 0)),
            pl.BlockSpec((d, cols), lambda i: (0, 0)),
            pl.BlockSpec((tm, 1), lambda i: (i, 0)),
            pl.BlockSpec((1, LANES), lambda i: (0, 0)),
            pl.BlockSpec((1, LANES), lambda i: (0, 0)),
        ],
        out_specs=[
            pl.BlockSpec((tm, qw), lambda i: (i, 0)),
            pl.BlockSpec((tm, 2 * LANES), lambda i: (i, 0)),
            pl.BlockSpec((tm, LANES), lambda i: (i, 0)),
        ],
        out_shape=[
            jax.ShapeDtypeStruct((n, qw), F32),
            jax.ShapeDtypeStruct((n, 2 * LANES), BF16),
            jax.ShapeDtypeStruct((n, LANES), F32),
        ],
        compiler_params=_params("parallel"),
        name="idx_proj",
    )(h, w_idx, pos, invf, kgain)


def _sgu_kernel(h_ref, w_ref, gv_ref, ws_ref, bs_ref, o_ref):
    tm = h_ref.shape[0]
    groups = tm // SGU_LEN
    res = jnp.dot(h_ref[...], w_ref[...], preferred_element_type=F32)
    u = jax.nn.gelu(res[:, :SGU_WIDTH])
    v = _rms(jax.nn.gelu(res[:, SGU_WIDTH:]), gv_ref[...]).astype(BF16)
    row = lax.broadcasted_iota(I32, (SGU_LEN, SGU_LEN), 0)
    col = lax.broadcasted_iota(I32, (SGU_LEN, SGU_LEN), 1)
    causal = (col // CHUNK) <= (row // CHUNK)
    for g in range(SGU_GROUPS):
        cs = slice(g * SGU_GROUP_DIM, (g + 1) * SGU_GROUP_DIM)
        wg = jnp.where(causal, ws_ref[g], 0.0).astype(BF16)
        vg = jnp.concatenate(
            [v[n * SGU_LEN:(n + 1) * SGU_LEN, cs] for n in range(groups)], axis=1)
        mixed = jnp.dot(wg, vg, preferred_element_type=F32) + bs_ref[g]
        for n in range(groups):
            rs = slice(n * SGU_LEN, (n + 1) * SGU_LEN)
            m = mixed[:, n * SGU_GROUP_DIM:(n + 1) * SGU_GROUP_DIM]
            o_ref[rs, cs] = (u[rs, cs] * m).astype(BF16)


def _sgu(h, w_sgu, gv, w_s, b_s, *, tm=512):
    n, d = h.shape
    tm = min(tm, n)
    return pl.pallas_call(
        _sgu_kernel,
        grid=(n // tm,),
        in_specs=[
            pl.BlockSpec((tm, d), lambda i: (i, 0)),
            pl.BlockSpec((d, 2 * SGU_WIDTH), lambda i: (0, 0)),
            pl.BlockSpec((1, SGU_WIDTH), lambda i: (0, 0)),
            pl.BlockSpec((SGU_GROUPS, SGU_LEN, SGU_LEN), lambda i: (0, 0, 0)),
            pl.BlockSpec((SGU_GROUPS, SGU_LEN, 1), lambda i: (0, 0, 0)),
        ],
        out_specs=pl.BlockSpec((tm, SGU_WIDTH), lambda i: (i, 0)),
        out_shape=jax.ShapeDtypeStruct((n, SGU_WIDTH), BF16),
        compiler_params=_params("parallel"),
        name="sgu_branch",
    )(h, w_sgu, gv, w_s, b_s)


def _gates_kernel(h_ref, w_ref, o_ref):
    res = jnp.dot(h_ref[...], w_ref[...], preferred_element_type=F32)
    o_ref[...] = jax.nn.sigmoid(res).astype(BF16)


def _gates(h, w_g, *, tm=1024, tn=1024):
    n, d = h.shape
    cols = w_g.shape[1]
    tm = min(tm, n)
    return pl.pallas_call(
        _gates_kernel,
        grid=(n // tm, cols // tn),
        in_specs=[
            pl.BlockSpec((tm, d), lambda i, j: (i, 0)),
            pl.BlockSpec((d, tn), lambda i, j: (0, j)),
        ],
        out_specs=pl.BlockSpec((tm, tn), lambda i, j: (i, j)),
        out_shape=jax.ShapeDtypeStruct((n, cols), BF16),
        compiler_params=_params("parallel", "arbitrary"),
        name="gates_proj",
    )(h, w_g)


def _select_kernel(qi_ref, wi_ref, kcat_ref, mask_ref, qcat_ref, s_ref, *, topk):
    t = SEL_T
    n_tiles = s_ref.shape[0]
    i = pl.program_id(1)
    n_live = i + 1

    for h in range(IDX_HEADS):
        q = qi_ref[0, h]
        hi = q.astype(BF16)
        lo = (q - hi.astype(F32)).astype(BF16)
        qcat_ref[h] = jnp.concatenate([hi, hi, lo, lo], axis=0)
    w = wi_ref[0]

    q_chunk = (i * t + lax.broadcasted_iota(I32, (t, t), 1)) // CHUNK
    k_row = lax.broadcasted_iota(I32, (t, t), 0)

    def score_body(kt, carry):
        mx, mn = carry
        kc = kcat_ref[0, pl.ds(pl.multiple_of(kt * t, t), t), :]
        acc = jnp.zeros((t, t), F32)
        for h in range(IDX_HEADS):
            d = jnp.dot(kc, qcat_ref[h], preferred_element_type=F32)
            acc = acc + jnp.maximum(d, 0.0) * w[h:h + 1, :]
        admissible = ((kt * t + k_row) // CHUNK) <= q_chunk
        s = jnp.where(admissible, acc, -jnp.inf)
        s_ref[kt] = s
        mx = jnp.maximum(mx, jnp.max(s, axis=0, keepdims=True))
        mn = jnp.minimum(mn, jnp.min(jnp.where(admissible, acc, jnp.inf),
                                     axis=0, keepdims=True))
        return mx, mn

    mx, mn = lax.fori_loop(
        0, n_live, score_body,
        (jnp.full((1, t), -jnp.inf, F32), jnp.full((1, t), jnp.inf, F32)))

    n_adm = ((i * t + lax.broadcasted_iota(I32, (1, t), 1)) // CHUNK + 1) * CHUNK
    done0 = n_adm <= topk

    def count_ge(c):
        def count_body(kt, acc):
            ind = jnp.where(s_ref[kt] >= c, 1.0, 0.0)
            for r in range(t // COUNT_ROWS):
                acc = acc + ind[r * COUNT_ROWS:(r + 1) * COUNT_ROWS]
            return acc
        acc = lax.fori_loop(0, n_live, count_body, jnp.zeros((COUNT_ROWS, t), F32))
        return jnp.sum(acc, axis=0, keepdims=True)

    def search_cond(st):
        return st[5] > 0.0

    def search_body(st):
        lo, hi, thr, live, first, _ = st
        mid = jnp.where(first > 0.0, hi, 0.5 * lo + 0.5 * hi)
        stuck = (first == 0.0) & ((mid <= lo) | (mid >= hi))
        cnt = count_ge(mid)
        finish = (live > 0.0) & (stuck | (cnt == topk))
        thr = jnp.where(finish, jnp.where(stuck, lo, mid), thr)
        live = jnp.where(finish, 0.0, live)
        lo = jnp.where(cnt > topk, mid, lo)
        hi = jnp.where(cnt < topk, mid, hi)
        return lo, hi, thr, live, jnp.float32(0.0), jnp.sum(live)

    live0 = jnp.where(done0, 0.0, 1.0)
    st0 = (mn, mx, jnp.full((1, t), -jnp.inf, F32), live0, jnp.float32(1.0),
           jnp.sum(live0))
    thr = lax.while_loop(search_cond, search_body, st0)[2]

    def mask_body(kt, carry):
        s = s_ref[kt]
        sel = jnp.where((s >= thr) & (s > -jnp.inf), 1.0, 0.0).T
        for a in range(t // MASK_ROWS):
            mask_ref[0, a, kt] = sel[a * MASK_ROWS:(a + 1) * MASK_ROWS].astype(jnp.int8)
        return carry

    lax.fori_loop(0, n_live, mask_body, 0)

    def zero_body(kt, carry):
        for a in range(t // MASK_ROWS):
            mask_ref[0, a, kt] = jnp.zeros((MASK_ROWS, t), jnp.int8)
        return carry

    lax.fori_loop(n_live, n_tiles, zero_body, 0)


def _select(qi_t, wi_t, kcat, *, topk):
    b, heads, dh, s = qi_t.shape
    t = SEL_T
    nq = s // t
    return pl.pallas_call(
        functools.partial(_select_kernel, topk=topk),
        grid=(b, nq),
        in_specs=[
            pl.BlockSpec((1, heads, dh, t), lambda bi, i: (bi, 0, 0, i)),
            pl.BlockSpec((1, heads, t), lambda bi, i: (bi, 0, i)),
            pl.BlockSpec((1, s, 4 * IDX_DIM), lambda bi, i: (bi, 0, 0)),
        ],
        out_specs=pl.BlockSpec((1, t // MASK_ROWS, nq, MASK_ROWS, t),
                               lambda bi, i: (bi, i, 0, 0, 0)),
        out_shape=jax.ShapeDtypeStruct((b, s // MASK_ROWS, nq, MASK_ROWS, t), jnp.int8),
        scratch_shapes=[
            pltpu.VMEM((heads, 4 * IDX_DIM, t), BF16),
            pltpu.VMEM((nq, t, t), F32),
        ],
        compiler_params=_params("parallel", "arbitrary"),
        name="index_select",
    )(qi_t, wi_t, kcat)


def _attn_kernel(q_ref, kt_ref, v_ref, mask_ref, o_ref,
                 m_ref, l_ref, acc_ref, v2_ref, s_ref, *, log2_scale):
    t = ATT_T
    i = pl.program_id(2)
    n = i + 1
    m_ref[...] = jnp.full(m_ref.shape, -jnp.inf, F32)
    l_ref[...] = jnp.zeros(l_ref.shape, F32)
    acc_ref[...] = jnp.zeros(acc_ref.shape, F32)
    v2_ref[...] = jnp.zeros(v2_ref.shape, BF16)
    q = q_ref[0]

    def logits(kt, slot):
        k_t = kt_ref[0, kt]
        for hh in range(2):
            hs = slice(hh * HEAD_DIM, (hh + 1) * HEAD_DIM)
            s_ref[slot, hh] = jnp.dot(q[:, hs], k_t[hs, :],
                                      preferred_element_type=F32) * log2_scale

    def update(kt, slot, off):
        v = v_ref[0, pl.ds(pl.multiple_of(kt * t, t), t), :]
        sel = jnp.concatenate(
            [mask_ref[0, a, kt] for a in range(t // MASK_ROWS)], axis=0).astype(I32) > off
        ps, alphas = [], []
        for hh in range(2):
            s = jnp.where(sel, s_ref[slot, hh], MASKED_LOGIT)
            m_prev = m_ref[hh]
            m_new = jnp.maximum(m_prev, jnp.max(s, axis=1, keepdims=True))
            p = jnp.exp2(s - jnp.concatenate([m_new] * (t // LANES), axis=1))
            alpha = jnp.exp2(m_prev - m_new)
            l_ref[hh] = alpha * l_ref[hh] + jnp.sum(p, axis=1, keepdims=True)
            m_ref[hh] = m_new
            ps.append(p.astype(BF16))
            alphas.append(alpha)
        v2_ref[slot, 0:t, 0:HEAD_DIM] = v[:, 0:HEAD_DIM]
        v2_ref[slot, t:2 * t, HEAD_DIM:2 * HEAD_DIM] = v[:, HEAD_DIM:2 * HEAD_DIM]
        pv = jnp.dot(jnp.concatenate(ps, axis=1), v2_ref[slot],
                     preferred_element_type=F32)
        acc_ref[...] = jnp.concatenate(alphas, axis=1) * acc_ref[...] + pv

    logits(0, 0)

    def body(j, carry):
        first = 2 * j
        second = jnp.minimum(first + 1, n - 1)
        logits(second, 1)
        update(first, 0, 0)
        logits(jnp.minimum(first + 2, n - 1), 0)
        update(second, 1, jnp.where(first + 1 < n, 0, 1))
        return carry

    lax.fori_loop(0, (n + 1) // 2, body, 0)
    denom = jnp.concatenate([l_ref[0], l_ref[1]], axis=1)
    o_ref[0] = (acc_ref[...] / denom).astype(BF16)


def _attn(qkv, k_t, mask):
    b, s, _ = qkv.shape
    t = ATT_T
    nq = s // t
    pairs = N_HEADS // 2
    v_col0 = 2 * ATT_WIDTH // (2 * HEAD_DIM)
    return pl.pallas_call(
        functools.partial(_attn_kernel, log2_scale=HEAD_DIM ** -0.5 * LOG2_E),
        grid=(b, pairs, nq),
        in_specs=[
            pl.BlockSpec((1, t, 2 * HEAD_DIM), lambda bi, hp, i: (bi, i, hp)),
            pl.BlockSpec((1, nq, 2 * HEAD_DIM, t), lambda bi, hp, i: (bi, 0, hp, 0)),
            pl.BlockSpec((1, s, 2 * HEAD_DIM), lambda bi, hp, i: (bi, 0, v_col0 + hp)),
            pl.BlockSpec((1, t // MASK_ROWS, nq, MASK_ROWS, SEL_T),
                         lambda bi, hp, i: (bi, i, 0, 0, 0)),
        ],
        out_specs=pl.BlockSpec((1, t, 2 * HEAD_DIM), lambda bi, hp, i: (bi, i, hp)),
        out_shape=jax.ShapeDtypeStruct((b, s, ATT_WIDTH), BF16),
        scratch_shapes=[
            pltpu.VMEM((2, t, LANES), F32),
            pltpu.VMEM((2, t, LANES), F32),
            pltpu.VMEM((t, 2 * HEAD_DIM), F32),
            pltpu.VMEM((2, 2 * t, 2 * HEAD_DIM), BF16),
            pltpu.VMEM((2, 2, t, t), F32),
        ],
        compiler_params=_params("parallel", "parallel", "arbitrary"),
        name="masked_attention",
    )(qkv, k_t, qkv, mask)


def _merge_kernel(ya_ref, yb_ref, g_ref, x_ref, wa_ref, wb_ref, wo_ref, o_ref):
    d = x_ref.shape[1]
    a = jnp.dot(ya_ref[...], wa_ref[...], preferred_element_type=F32)
    b = jnp.dot(yb_ref[...], wb_ref[...], preferred_element_type=F32)
    merged = g_ref[:, :d].astype(F32) * a + g_ref[:, d:].astype(F32) * b
    o_ref[...] = x_ref[...] + jnp.dot(merged.astype(BF16), wo_ref[...],
                                      preferred_element_type=F32)


def _merge(ya, yb, gates, x, wa, wb, wo, *, tm=256):
    n, d = x.shape
    tm = min(tm, n)
    return pl.pallas_call(
        _merge_kernel,
        grid=(n // tm,),
        in_specs=[
            pl.BlockSpec((tm, ya.shape[1]), lambda i: (i, 0)),
            pl.BlockSpec((tm, yb.shape[1]), lambda i: (i, 0)),
            pl.BlockSpec((tm, 2 * d), lambda i: (i, 0)),
            pl.BlockSpec((tm, d), lambda i: (i, 0)),
            pl.BlockSpec(wa.shape, lambda i: (0, 0)),
            pl.BlockSpec(wb.shape, lambda i: (0, 0)),
            pl.BlockSpec(wo.shape, lambda i: (0, 0)),
        ],
        out_specs=pl.BlockSpec((tm, d), lambda i: (i, 0)),
        out_shape=jax.ShapeDtypeStruct((n, d), F32),
        compiler_params=_params("parallel"),
        name="merge_out",
    )(ya, yb, gates, x, wa, wb, wo)


def _inv_freq_lanes(rot_dim, period, live):
    inv = ROPE_THETA ** (-jnp.arange(0, rot_dim, 2, dtype=F32) / rot_dim)
    lane = jnp.arange(LANES)
    j = lane % period
    vals = inv[j % (rot_dim // 2)]
    return jnp.where((j < rot_dim) & (lane < live), vals, 0.0).astype(F32)[None, :]


def _layer(x, pos, invf_att, invf_idx, p):
    b, s, d = x.shape
    n = b * s
    topk = min(TOPK_MAX, s // 4)
    x = x.reshape(n, d)

    x1, h = _ffn(x, p["norm_ffn1"][None], p["ffn1_w_gate"].astype(BF16),
                 p["ffn1_w_up"].astype(BF16), p["ffn1_w_down"].astype(BF16),
                 p["norm_mix"][None], emit_norm=True)

    w_in = p["w_in"]
    c_qkv = 3 * ATT_WIDTH
    c_qi = c_qkv + IDX_HEADS * IDX_DIM
    c_idx = c_qi + IDX_DIM + IDX_HEADS
    c_sgu = c_idx + 2 * SGU_WIDTH
    w_qkv = w_in[:, :c_qkv].astype(BF16)
    w_idx = jnp.pad(w_in[:, c_qkv:c_idx],
                    ((0, 0), (0, LANES - IDX_DIM - IDX_HEADS))).astype(BF16)
    w_sgu = w_in[:, c_idx:c_sgu].astype(BF16)
    w_gate = w_in[:, c_sgu:].astype(BF16)

    gains = jnp.stack([p["q_norm"], p["k_norm"]])[:, None, :]
    qkv = _qkv(h, w_qkv, pos, invf_att, gains)
    kgain = jnp.pad(p["idx_k_norm"], (0, LANES - IDX_DIM))[None]
    qi, kcat, wi = _idx(h, w_idx, pos, invf_idx, kgain)
    yb = _sgu(h, w_sgu, p["sgu_v_norm"][None], p["sgu_w_s"], p["sgu_b_s"][:, :, None])
    gates = _gates(h, w_gate)

    qi_t = qi.reshape(b, s, IDX_HEADS, IDX_DIM).transpose(0, 2, 3, 1)
    wi_t = wi.reshape(b, s, LANES)[:, :, :IDX_HEADS].transpose(0, 2, 1)
    mask = _select(qi_t, wi_t, kcat.reshape(b, s, 4 * IDX_DIM), topk=topk)
    qkv3 = qkv.reshape(b, s, c_qkv)
    k_t = qkv3[:, :, ATT_WIDTH:2 * ATT_WIDTH].reshape(
        b, s // ATT_T, ATT_T, ATT_WIDTH).transpose(0, 1, 3, 2)
    ya = _attn(qkv3, k_t, mask).reshape(n, ATT_WIDTH)

    x2 = _merge(ya, yb, gates, x1, p["w_up_attn"].astype(BF16),
                p["w_up_sgu"].astype(BF16), p["w_out"].astype(BF16))
    out = _ffn(x2, p["norm_ffn2"][None], p["ffn2_w_gate"].astype(BF16),
               p["ffn2_w_up"].astype(BF16), p["ffn2_w_down"].astype(BF16),
               p["norm_ffn2"][None], emit_norm=False)
    return out.reshape(b, s, d)


def kernel(x, positions, norm_ffn1, ffn1_w_gate, ffn1_w_up, ffn1_w_down, norm_mix, w_in,
           q_norm, k_norm, idx_k_norm, sgu_v_norm, sgu_w_s, sgu_b_s, w_up_attn, w_up_sgu,
           w_out, norm_ffn2, ffn2_w_gate, ffn2_w_up, ffn2_w_down):
    params = dict(
        norm_ffn1=norm_ffn1, ffn1_w_gate=ffn1_w_gate, ffn1_w_up=ffn1_w_up,
        ffn1_w_down=ffn1_w_down, norm_mix=norm_mix, w_in=w_in, q_norm=q_norm,
        k_norm=k_norm, idx_k_norm=idx_k_norm, sgu_v_norm=sgu_v_norm, sgu_w_s=sgu_w_s,
        sgu_b_s=sgu_b_s, w_up_attn=w_up_attn, w_up_sgu=w_up_sgu, w_out=w_out,
        norm_ffn2=norm_ffn2, ffn2_w_gate=ffn2_w_gate, ffn2_w_up=ffn2_w_up,
        ffn2_w_down=ffn2_w_down)
    pos = positions.reshape(-1, 1).astype(I32)
    invf_att = _inv_freq_lanes(ROT_DIM, LANES, LANES)
    invf_idx = _inv_freq_lanes(IDX_ROT_DIM, IDX_DIM, LANES)
    for l in range(w_in.shape[0]):
        x = _layer(x, pos, invf_att, invf_idx, {k: v[l] for k, v in params.items()})
    return x
```

```python
import functools

import jax
import jax.numpy as jnp
from jax import lax
from jax.experimental import pallas as pl
from jax.experimental.pallas import tpu as pltpu

F32 = jnp.float32
BF16 = jnp.bfloat16
I32 = jnp.int32

EPS = 1e-6
CHUNK = 64
N_HEADS = 8
HEAD_DIM = 128
ATT_WIDTH = N_HEADS * HEAD_DIM
ROPE_THETA = 500000.0
ROT_DIM = HEAD_DIM // 4
IDX_HEADS = 16
IDX_DIM = 64
IDX_ROT_DIM = IDX_DIM // 4
TOPK_MAX = 256
SGU_LEN = 128
SGU_GROUPS = 8
SGU_GROUP_DIM = 128
SGU_WIDTH = SGU_GROUPS * SGU_GROUP_DIM

LANES = 128
MXU_DIM = 256
VMEM_LIMIT_BYTES = 56 * 1024 * 1024

SEL_T = MXU_DIM
MASK_ROWS = 128
COUNT_ROWS = 64
ATT_T = 256
MASKED_LOGIT = -1e30
LOG2_E = 1.4426950408889634


def _params(*sem):
    return pltpu.CompilerParams(dimension_semantics=sem,
                                vmem_limit_bytes=VMEM_LIMIT_BYTES)


def _rms(t, gain):
    ms = jnp.mean(t * t, axis=-1, keepdims=True)
    return t * lax.rsqrt(ms + EPS) * gain


def _rope_tables(pos_ref, invf_ref, half, period):
    ang = pos_ref[...].astype(F32) * invf_ref[...]
    c = jnp.cos(ang)
    s = jnp.sin(ang)
    lane = lax.broadcasted_iota(I32, ang.shape, 1) & (period - 1)
    return c, jnp.where(lane < half, -s, 0.0), jnp.where(lane >= half, s, 0.0)


def _rope(t, c, s1, s2, half):
    return (t * c + pltpu.roll(t, LANES - half, 1) * s1
            + pltpu.roll(t, half, 1) * s2)


def _ffn_kernel(x_ref, g_ref, wg_ref, wu_ref, wd_ref, g2_ref, o_ref, *rest,
                emit_norm):
    if emit_norm:
        h_out_ref, hn_ref = rest
    else:
        (hn_ref,) = rest
    j = pl.program_id(1)

    @pl.when(j == 0)
    def _():
        hn_ref[...] = _rms(x_ref[...], g_ref[...]).astype(BF16)
        o_ref[...] = jnp.zeros_like(o_ref)

    h = hn_ref[...]
    a = jnp.dot(h, wg_ref[...], preferred_element_type=F32)
    b = jnp.dot(h, wu_ref[...], preferred_element_type=F32)
    act = (a * jax.nn.sigmoid(a) * b).astype(BF16)
    o_ref[...] += jnp.dot(act, wd_ref[...], preferred_element_type=F32)

    @pl.when(j == pl.num_programs(1) - 1)
    def _():
        y = x_ref[...] + 0.5 * o_ref[...]
        o_ref[...] = y
        if emit_norm:
            h_out_ref[...] = _rms(y, g2_ref[...]).astype(BF16)


def _ffn(x, gain, w_gate, w_up, w_down, gain2, *, emit_norm, tm=512, tf=512):
    n, d = x.shape
    f = w_gate.shape[1]
    tm = min(tm, n)
    out_shape = [jax.ShapeDtypeStruct((n, d), F32)]
    out_specs = [pl.BlockSpec((tm, d), lambda i, j: (i, 0))]
    if emit_norm:
        out_shape.append(jax.ShapeDtypeStruct((n, d), BF16))
        out_specs.append(pl.BlockSpec((tm, d), lambda i, j: (i, 0)))
    res = pl.pallas_call(
        functools.partial(_ffn_kernel, emit_norm=emit_norm),
        grid=(n // tm, f // tf),
        in_specs=[
            pl.BlockSpec((tm, d), lambda i, j: (i, 0)),
            pl.BlockSpec((1, d), lambda i, j: (0, 0)),
            pl.BlockSpec((d, tf), lambda i, j: (0, j)),
            pl.BlockSpec((d, tf), lambda i, j: (0, j)),
            pl.BlockSpec((tf, d), lambda i, j: (j, 0)),
            pl.BlockSpec((1, d), lambda i, j: (0, 0)),
        ],
        out_specs=out_specs,
        out_shape=out_shape,
        scratch_shapes=[pltpu.VMEM((tm, d), BF16)],
        compiler_params=_params("parallel", "arbitrary"),
        name="ffn_norm" if emit_norm else "ffn",
    )(x, gain, w_gate, w_up, w_down, gain2)
    return res if emit_norm else res[0]


def _qkv_kernel(h_ref, w_ref, pos_ref, invf_ref, gains_ref, o_ref,
                c_ref, s1_ref, s2_ref, *, rope_tiles):
    j = pl.program_id(1)
    half = ROT_DIM // 2

    @pl.when(j == 0)
    def _():
        c, s1, s2 = _rope_tables(pos_ref, invf_ref, half, LANES)
        c_ref[...] = c
        s1_ref[...] = s1
        s2_ref[...] = s2

    res = jnp.dot(h_ref[...], w_ref[...], preferred_element_type=F32)

    @pl.when(j < rope_tiles)
    def _():
        gain = gains_ref[0]
        for hh in range(res.shape[1] // HEAD_DIM):
            sl = slice(hh * HEAD_DIM, (hh + 1) * HEAD_DIM)
            t = _rms(res[:, sl], gain)
            t = _rope(t, c_ref[...], s1_ref[...], s2_ref[...], half)
            o_ref[:, sl] = t.astype(BF16)

    @pl.when(j >= rope_tiles)
    def _():
        o_ref[...] = res.astype(BF16)


def _qkv(h, w_qkv, pos, invf, gains, *, tm=1024, tn=512):
    n, d = h.shape
    cols = w_qkv.shape[1]
    tm = min(tm, n)
    per = ATT_WIDTH // tn
    return pl.pallas_call(
        functools.partial(_qkv_kernel, rope_tiles=2 * per),
        grid=(n // tm, cols // tn),
        in_specs=[
            pl.BlockSpec((tm, d), lambda i, j: (i, 0)),
            pl.BlockSpec((d, tn), lambda i, j: (0, j)),
            pl.BlockSpec((tm, 1), lambda i, j: (i, 0)),
            pl.BlockSpec((1, LANES), lambda i, j: (0, 0)),
            pl.BlockSpec((1, 1, HEAD_DIM),
                         lambda i, j: (jnp.minimum(j // per, 1), 0, 0)),
        ],
        out_specs=pl.BlockSpec((tm, tn), lambda i, j: (i, j)),
        out_shape=jax.ShapeDtypeStruct((n, cols), BF16),
        scratch_shapes=[pltpu.VMEM((tm, LANES), F32)] * 3,
        compiler_params=_params("parallel", "arbitrary"),
        name="qkv_proj",
    )(h, w_qkv, pos, invf, gains)


def _idx_kernel(h_ref, w_ref, pos_ref, invf_ref, kgain_ref,
                qi_ref, kcat_ref, wi_ref, *, w_scale):
    half = IDX_ROT_DIM // 2
    qw = IDX_HEADS * IDX_DIM
    res = jnp.dot(h_ref[...], w_ref[...], preferred_element_type=F32)
    c, s1, s2 = _rope_tables(pos_ref, invf_ref, half, IDX_DIM)
    for t in range(qw // LANES):
        sl = slice(t * LANES, (t + 1) * LANES)
        qi_ref[:, sl] = _rope(res[:, sl], c, s1, s2, half)

    r = res[:, qw:qw + LANES]
    lane = lax.broadcasted_iota(I32, r.shape, 1)
    is_k = lane < IDX_DIM
    ms = jnp.sum(jnp.where(is_k, r * r, 0.0), axis=-1, keepdims=True) * (1.0 / IDX_DIM)
    kn = r * lax.rsqrt(ms + EPS) * kgain_ref[...]
    kr = _rope(kn, jnp.where(is_k, c, 1.0), jnp.where(is_k, s1, 0.0),
               jnp.where(is_k, s2, 0.0), half)
    hi = kr.astype(BF16).astype(F32)
    lo = kr - hi
    hi_lo = jnp.where(is_k, hi, pltpu.roll(lo, IDX_DIM, 1)).astype(BF16)
    kcat_ref[:, 0:LANES] = hi_lo
    kcat_ref[:, LANES:2 * LANES] = hi_lo
    is_w = (lane >= IDX_DIM) & (lane < IDX_DIM + IDX_HEADS)
    wi_ref[...] = pltpu.roll(jnp.where(is_w, r * w_scale, 0.0), LANES - IDX_DIM, 1)


def _idx(h, w_idx, pos, invf, kgain, *, tm=512):
    n, d = h.shape
    cols = w_idx.shape[1]
    qw = IDX_HEADS * IDX_DIM
    tm = min(tm, n)
    w_scale = (IDX_HEADS ** -0.5) * (IDX_DIM ** -0.5)
    return pl.pallas_call(
        functools.partial(_idx_kernel, w_scale=w_scale),
        grid=(n // tm,),
        in_specs=[
            pl.BlockSpec((tm, d), lambda i: (i, 0)),
            pl.BlockSpec((d, cols), lambda i: (0, 0)),
            pl.BlockSpec((tm, 1), lambda i: (i, 0)),
            pl.BlockSpec((1, LANES), lambda i: (0, 0)),
            pl.BlockSpec((1, LANES), lambda i: (0, 0)),
        ],
        out_specs=[
            pl.BlockSpec((tm, qw), lambda i: (i, 0)),
            pl.BlockSpec((tm, 2 * LANES), lambda i: (i, 0)),
            pl.BlockSpec((tm, LANES), lambda i: (i, 0)),
        ],
        out_shape=[
            jax.ShapeDtypeStruct((n, qw), F32),
            jax.ShapeDtypeStruct((n, 2 * LANES), BF16),
            jax.ShapeDtypeStruct((n, LANES), F32),
        ],
        compiler_params=_params("parallel"),
        name="idx_proj",
    )(h, w_idx, pos, invf, kgain)


def _sgu_kernel(h_ref, w_ref, gv_ref, ws_ref, bs_ref, o_ref):
    tm = h_ref.shape[0]
    groups = tm // SGU_LEN
    res = jnp.dot(h_ref[...], w_ref[...], preferred_element_type=F32)
    u = jax.nn.gelu(res[:, :SGU_WIDTH])
    v = _rms(jax.nn.gelu(res[:, SGU_WIDTH:]), gv_ref[...]).astype(BF16)
    row = lax.broadcasted_iota(I32, (SGU_LEN, SGU_LEN), 0)
    col = lax.broadcasted_iota(I32, (SGU_LEN, SGU_LEN), 1)
    causal = (col // CHUNK) <= (row // CHUNK)
    for g in range(SGU_GROUPS):
        cs = slice(g * SGU_GROUP_DIM, (g + 1) * SGU_GROUP_DIM)
        wg = jnp.where(causal, ws_ref[g], 0.0).astype(BF16)
        vg = jnp.concatenate(
            [v[n * SGU_LEN:(n + 1) * SGU_LEN, cs] for n in range(groups)], axis=1)
        mixed = jnp.dot(wg, vg, preferred_element_type=F32) + bs_ref[g]
        for n in range(groups):
            rs = slice(n * SGU_LEN, (n + 1) * SGU_LEN)
            m = mixed[:, n * SGU_GROUP_DIM:(n + 1) * SGU_GROUP_DIM]
            o_ref[rs, cs] = (u[rs, cs] * m).astype(BF16)


def _sgu(h, w_sgu, gv, w_s, b_s, *, tm=512):
    n, d = h.shape
    tm = min(tm, n)
    return pl.pallas_call(
        _sgu_kernel,
        grid=(n // tm,),
        in_specs=[
            pl.BlockSpec((tm, d), lambda i: (i, 0)),
            pl.BlockSpec((d, 2 * SGU_WIDTH), lambda i: (0, 0)),
            pl.BlockSpec((1, SGU_WIDTH), lambda i: (0, 0)),
            pl.BlockSpec((SGU_GROUPS, SGU_LEN, SGU_LEN), lambda i: (0, 0, 0)),
            pl.BlockSpec((SGU_GROUPS, SGU_LEN, 1), lambda i: (0, 0, 0)),
        ],
        out_specs=pl.BlockSpec((tm, SGU_WIDTH), lambda i: (i, 0)),
        out_shape=jax.ShapeDtypeStruct((n, SGU_WIDTH), BF16),
        compiler_params=_params("parallel"),
        name="sgu_branch",
    )(h, w_sgu, gv, w_s, b_s)


def _gates_kernel(h_ref, w_ref, o_ref):
    res = jnp.dot(h_ref[...], w_ref[...], preferred_element_type=F32)
    o_ref[...] = jax.nn.sigmoid(res).astype(BF16)


def _gates(h, w_g, *, tm=1024, tn=1024):
    n, d = h.shape
    cols = w_g.shape[1]
    tm = min(tm, n)
    return pl.pallas_call(
        _gates_kernel,
        grid=(n // tm, cols // tn),
        in_specs=[
            pl.BlockSpec((tm, d), lambda i, j: (i, 0)),
            pl.BlockSpec((d, tn), lambda i, j: (0, j)),
        ],
        out_specs=pl.BlockSpec((tm, tn), lambda i, j: (i, j)),
        out_shape=jax.ShapeDtypeStruct((n, cols), BF16),
        compiler_params=_params("parallel", "arbitrary"),
        name="gates_proj",
    )(h, w_g)


def _select_kernel(qi_ref, wi_ref, kcat_ref, mask_ref, qcat_ref, s_ref, *, topk):
    t = SEL_T
    n_tiles = s_ref.shape[0]
    i = pl.program_id(1)
    n_live = i + 1

    for h in range(IDX_HEADS):
        q = qi_ref[0, h]
        hi = q.astype(BF16)
        lo = (q - hi.astype(F32)).astype(BF16)
        qcat_ref[h] = jnp.concatenate([hi, hi, lo, lo], axis=0)
    w = wi_ref[0]

    q_chunk = (i * t + lax.broadcasted_iota(I32, (t, t), 1)) // CHUNK
    k_row = lax.broadcasted_iota(I32, (t, t), 0)

    def score_body(kt, carry):
        mx, mn = carry
        kc = kcat_ref[0, pl.ds(pl.multiple_of(kt * t, t), t), :]
        acc = jnp.zeros((t, t), F32)
        for h in range(IDX_HEADS):
            d = jnp.dot(kc, qcat_ref[h], preferred_element_type=F32)
            acc = acc + jnp.maximum(d, 0.0) * w[h:h + 1, :]
        admissible = ((kt * t + k_row) // CHUNK) <= q_chunk
        s = jnp.where(admissible, acc, -jnp.inf)
        s_ref[kt] = s
        mx = jnp.maximum(mx, jnp.max(s, axis=0, keepdims=True))
        mn = jnp.minimum(mn, jnp.min(jnp.where(admissible, acc, jnp.inf),
                                     axis=0, keepdims=True))
        return mx, mn

    mx, mn = lax.fori_loop(
        0, n_live, score_body,
        (jnp.full((1, t), -jnp.inf, F32), jnp.full((1, t), jnp.inf, F32)))

    n_adm = ((i * t + lax.broadcasted_iota(I32, (1, t), 1)) // CHUNK + 1) * CHUNK
    done0 = n_adm <= topk

    def count_ge(c):
        def count_body(kt, acc):
            ind = jnp.where(s_ref[kt] >= c, 1.0, 0.0)
            for r in range(t // COUNT_ROWS):
                acc = acc + ind[r * COUNT_ROWS:(r + 1) * COUNT_ROWS]
            return acc
        acc = lax.fori_loop(0, n_live, count_body, jnp.zeros((COUNT_ROWS, t), F32))
        return jnp.sum(acc, axis=0, keepdims=True)

    def search_cond(st):
        return st[5] > 0.0

    def search_body(st):
        lo, hi, thr, live, first, _ = st
        mid = jnp.where(first > 0.0, hi, 0.5 * lo + 0.5 * hi)
        stuck = (first == 0.0) & ((mid <= lo) | (mid >= hi))
        cnt = count_ge(mid)
        finish = (live > 0.0) & (stuck | (cnt == topk))
        thr = jnp.where(finish, jnp.where(stuck, lo, mid), thr)
        live = jnp.where(finish, 0.0, live)
        lo = jnp.where(cnt > topk, mid, lo)
        hi = jnp.where(cnt < topk, mid, hi)
        return lo, hi, thr, live, jnp.float32(0.0), jnp.sum(live)

    live0 = jnp.where(done0, 0.0, 1.0)
    st0 = (mn, mx, jnp.full((1, t), -jnp.inf, F32), live0, jnp.float32(1.0),
           jnp.sum(live0))
    thr = lax.while_loop(search_cond, search_body, st0)[2]

    def mask_body(kt, carry):
        s = s_ref[kt]
        sel = (s >= thr) & (s > -jnp.inf)
        mask_ref[0, 0, kt] = jnp.where(sel, 1, 0).astype(jnp.int8)
        return carry

    lax.fori_loop(0, n_live, mask_body, 0)

    def zero_body(kt, carry):
        mask_ref[0, 0, kt] = jnp.zeros((t, t), jnp.int8)
        return carry

    lax.fori_loop(n_live, n_tiles, zero_body, 0)


def _select(qi_t, wi_t, kcat, *, topk):
    b, heads, dh, s = qi_t.shape
    t = SEL_T
    nq = s // t
    return pl.pallas_call(
        functools.partial(_select_kernel, topk=topk),
        grid=(b, nq),
        in_specs=[
            pl.BlockSpec((1, heads, dh, t), lambda bi, i: (bi, 0, 0, i)),
            pl.BlockSpec((1, heads, t), lambda bi, i: (bi, 0, i)),
            pl.BlockSpec((1, s, 4 * IDX_DIM), lambda bi, i: (bi, 0, 0)),
        ],
        out_specs=pl.BlockSpec((1, 1, nq, t, t), lambda bi, i: (bi, i, 0, 0, 0)),
        out_shape=jax.ShapeDtypeStruct((b, nq, nq, t, t), jnp.int8),
        scratch_shapes=[
            pltpu.VMEM((heads, 4 * IDX_DIM, t), BF16),
            pltpu.VMEM((nq, t, t), F32),
        ],
        compiler_params=_params("parallel", "arbitrary"),
        name="index_select",
    )(qi_t, wi_t, kcat)


def _attn_kernel(qt_ref, k_ref, vt_ref, mask_ref, o_ref,
                 m_ref, l_ref, acc_ref, s_ref):
    t = ATT_T
    i = pl.program_id(2)
    n = i + 1
    m_ref[...] = jnp.full(m_ref.shape, -jnp.inf, F32)
    l_ref[...] = jnp.zeros(l_ref.shape, F32)
    acc_ref[...] = jnp.zeros(acc_ref.shape, F32)
    q_t = qt_ref[0]

    def logits(kt, slot):
        k = k_ref[0, pl.ds(pl.multiple_of(kt * t, t), t), :]
        for hh in range(2):
            hs = slice(hh * HEAD_DIM, (hh + 1) * HEAD_DIM)
            s_ref[slot, hh] = jnp.dot(k[:, hs], q_t[hs, :], preferred_element_type=F32)

    def update(kt, slot, off):
        v_t = vt_ref[0, kt]
        sel = mask_ref[0, 0, kt].astype(I32) > off
        for hh in range(2):
            hs = slice(hh * HEAD_DIM, (hh + 1) * HEAD_DIM)
            s = jnp.where(sel, s_ref[slot, hh], MASKED_LOGIT)
            m_prev = m_ref[hh]
            m_new = jnp.maximum(m_prev, jnp.max(s, axis=0, keepdims=True))
            p = jnp.exp2(s - m_new)
            alpha = jnp.exp2(m_prev - m_new)
            l_ref[hh] = alpha * l_ref[hh] + jnp.sum(p, axis=0, keepdims=True)
            m_ref[hh] = m_new
            pv = jnp.dot(v_t[hs, :], p.astype(BF16), preferred_element_type=F32)
            acc_ref[hh] = alpha * acc_ref[hh] + pv

    logits(0, 0)

    def body(j, carry):
        first = 2 * j
        second = jnp.minimum(first + 1, n - 1)
        logits(second, 1)
        update(first, 0, 0)
        logits(jnp.minimum(first + 2, n - 1), 0)
        update(second, 1, jnp.where(first + 1 < n, 0, 1))
        return carry

    lax.fori_loop(0, (n + 1) // 2, body, 0)
    for hh in range(2):
        o_ref[0, hh * HEAD_DIM:(hh + 1) * HEAD_DIM, :] = (
            acc_ref[hh] / l_ref[hh]).astype(BF16)


def _attn(q_t, qkv, v_t, mask):
    b, s, _ = qkv.shape
    t = ATT_T
    nq = s // t
    pairs = N_HEADS // 2
    k_col0 = ATT_WIDTH // (2 * HEAD_DIM)
    return pl.pallas_call(
        _attn_kernel,
        grid=(b, pairs, nq),
        in_specs=[
            pl.BlockSpec((1, 2 * HEAD_DIM, t), lambda bi, hp, i: (bi, hp, i)),
            pl.BlockSpec((1, s, 2 * HEAD_DIM), lambda bi, hp, i: (bi, 0, k_col0 + hp)),
            pl.BlockSpec((1, nq, 2 * HEAD_DIM, t), lambda bi, hp, i: (bi, 0, hp, 0)),
            pl.BlockSpec((1, 1, nq, t, t), lambda bi, hp, i: (bi, i, 0, 0, 0)),
        ],
        out_specs=pl.BlockSpec((1, 2 * HEAD_DIM, t), lambda bi, hp, i: (bi, hp, i)),
        out_shape=jax.ShapeDtypeStruct((b, ATT_WIDTH, s), BF16),
        scratch_shapes=[
            pltpu.VMEM((2, 1, t), F32),
            pltpu.VMEM((2, 1, t), F32),
            pltpu.VMEM((2, HEAD_DIM, t), F32),
            pltpu.VMEM((2, 2, t, t), F32),
        ],
        compiler_params=_params("parallel", "parallel", "arbitrary"),
        name="masked_attention",
    )(q_t, qkv, v_t, mask)


def _merge_kernel(ya_ref, yb_ref, g_ref, x_ref, wa_ref, wb_ref, wo_ref, o_ref):
    d = x_ref.shape[1]
    a = jnp.dot(ya_ref[...], wa_ref[...], preferred_element_type=F32)
    b = jnp.dot(yb_ref[...], wb_ref[...], preferred_element_type=F32)
    merged = g_ref[:, :d].astype(F32) * a + g_ref[:, d:].astype(F32) * b
    o_ref[...] = x_ref[...] + jnp.dot(merged.astype(BF16), wo_ref[...],
                                      preferred_element_type=F32)


def _merge(ya, yb, gates, x, wa, wb, wo, *, tm=256):
    n, d = x.shape
    tm = min(tm, n)
    return pl.pallas_call(
        _merge_kernel,
        grid=(n // tm,),
        in_specs=[
            pl.BlockSpec((tm, ya.shape[1]), lambda i: (i, 0)),
            pl.BlockSpec((tm, yb.shape[1]), lambda i: (i, 0)),
            pl.BlockSpec((tm, 2 * d), lambda i: (i, 0)),
            pl.BlockSpec((tm, d), lambda i: (i, 0)),
            pl.BlockSpec(wa.shape, lambda i: (0, 0)),
            pl.BlockSpec(wb.shape, lambda i: (0, 0)),
            pl.BlockSpec(wo.shape, lambda i: (0, 0)),
        ],
        out_specs=pl.BlockSpec((tm, d), lambda i: (i, 0)),
        out_shape=jax.ShapeDtypeStruct((n, d), F32),
        compiler_params=_params("parallel"),
        name="merge_out",
    )(ya, yb, gates, x, wa, wb, wo)


def _inv_freq_lanes(rot_dim, period, live):
    inv = ROPE_THETA ** (-jnp.arange(0, rot_dim, 2, dtype=F32) / rot_dim)
    lane = jnp.arange(LANES)
    j = lane % period
    vals = inv[j % (rot_dim // 2)]
    return jnp.where((j < rot_dim) & (lane < live), vals, 0.0).astype(F32)[None, :]


def _layer(x, pos, invf_att, invf_idx, p):
    b, s, d = x.shape
    n = b * s
    topk = min(TOPK_MAX, s // 4)
    x = x.reshape(n, d)

    x1, h = _ffn(x, p["norm_ffn1"][None], p["ffn1_w_gate"].astype(BF16),
                 p["ffn1_w_up"].astype(BF16), p["ffn1_w_down"].astype(BF16),
                 p["norm_mix"][None], emit_norm=True)

    w_in = p["w_in"]
    c_qkv = 3 * ATT_WIDTH
    c_qi = c_qkv + IDX_HEADS * IDX_DIM
    c_idx = c_qi + IDX_DIM + IDX_HEADS
    c_sgu = c_idx + 2 * SGU_WIDTH
    w_qkv = w_in[:, :c_qkv].astype(BF16)
    w_idx = jnp.pad(w_in[:, c_qkv:c_idx],
                    ((0, 0), (0, LANES - IDX_DIM - IDX_HEADS))).astype(BF16)
    w_sgu = w_in[:, c_idx:c_sgu].astype(BF16)
    w_gate = w_in[:, c_sgu:].astype(BF16)

    gains = jnp.stack([p["q_norm"] * (HEAD_DIM ** -0.5 * LOG2_E), p["k_norm"]])[:, None, :]
    qkv = _qkv(h, w_qkv, pos, invf_att, gains)
    kgain = jnp.pad(p["idx_k_norm"], (0, LANES - IDX_DIM))[None]
    qi, kcat, wi = _idx(h, w_idx, pos, invf_idx, kgain)
    yb = _sgu(h, w_sgu, p["sgu_v_norm"][None], p["sgu_w_s"], p["sgu_b_s"][:, :, None])
    gates = _gates(h, w_gate)

    qi_t = qi.reshape(b, s, IDX_HEADS, IDX_DIM).transpose(0, 2, 3, 1)
    wi_t = wi.reshape(b, s, LANES)[:, :, :IDX_HEADS].transpose(0, 2, 1)
    mask = _select(qi_t, wi_t, kcat.reshape(b, s, 4 * IDX_DIM), topk=topk)
    qkv3 = qkv.reshape(b, s, c_qkv)
    q_t = qkv3[:, :, :ATT_WIDTH].transpose(0, 2, 1)
    v_t = qkv3[:, :, 2 * ATT_WIDTH:].reshape(
        b, s // ATT_T, ATT_T, ATT_WIDTH).transpose(0, 1, 3, 2)
    ya = _attn(q_t, qkv3, v_t, mask).transpose(0, 2, 1).reshape(n, ATT_WIDTH)

    x2 = _merge(ya, yb, gates, x1, p["w_up_attn"].astype(BF16),
                p["w_up_sgu"].astype(BF16), p["w_out"].astype(BF16))
    out = _ffn(x2, p["norm_ffn2"][None], p["ffn2_w_gate"].astype(BF16),
               p["ffn2_w_up"].astype(BF16), p["ffn2_w_down"].astype(BF16),
               p["norm_ffn2"][None], emit_norm=False)
    return out.reshape(b, s, d)


def kernel(x, positions, norm_ffn1, ffn1_w_gate, ffn1_w_up, ffn1_w_down, norm_mix, w_in,
           q_norm, k_norm, idx_k_norm, sgu_v_norm, sgu_w_s, sgu_b_s, w_up_attn, w_up_sgu,
           w_out, norm_ffn2, ffn2_w_gate, ffn2_w_up, ffn2_w_down):
    params = dict(
        norm_ffn1=norm_ffn1, ffn1_w_gate=ffn1_w_gate, ffn1_w_up=ffn1_w_up,
        ffn1_w_down=ffn1_w_down, norm_mix=norm_mix, w_in=w_in, q_norm=q_norm,
        k_norm=k_norm, idx_k_norm=idx_k_norm, sgu_v_norm=sgu_v_norm, sgu_w_s=sgu_w_s,
        sgu_b_s=sgu_b_s, w_up_attn=w_up_attn, w_up_sgu=w_up_sgu, w_out=w_out,
        norm_ffn2=norm_ffn2, ffn2_w_gate=ffn2_w_gate, ffn2_w_up=ffn2_w_up,
        ffn2_w_down=ffn2_w_down)
    pos = positions.reshape(-1, 1).astype(I32)
    invf_att = _inv_freq_lanes(ROT_DIM, LANES, LANES)
    invf_idx = _inv_freq_lanes(IDX_ROT_DIM, IDX_DIM, LANES)
    for l in range(w_in.shape[0]):
        x = _layer(x, pos, invf_att, invf_idx, {k: v[l] for k, v in params.items()})
    return x
```

```python
import functools

import jax
import jax.numpy as jnp
from jax import lax
from jax.experimental import pallas as pl
from jax.experimental.pallas import tpu as pltpu

F32 = jnp.float32
BF16 = jnp.bfloat16
I32 = jnp.int32

EPS = 1e-6
CHUNK = 64
N_HEADS = 8
HEAD_DIM = 128
ATT_WIDTH = N_HEADS * HEAD_DIM
ROPE_THETA = 500000.0
ROT_DIM = HEAD_DIM // 4
IDX_HEADS = 16
IDX_DIM = 64
IDX_ROT_DIM = IDX_DIM // 4
TOPK_MAX = 256
SGU_LEN = 128
SGU_GROUPS = 8
SGU_GROUP_DIM = 128
SGU_WIDTH = SGU_GROUPS * SGU_GROUP_DIM

LANES = 128
MXU_DIM = 256
VMEM_LIMIT_BYTES = 56 * 1024 * 1024

SEL_T = MXU_DIM
MASK_ROWS = 128
COUNT_ROWS = 64
ATT_T = 256
DENOM_ROWS = 16
ABOVE_MAX_REL = 1e-6
ABOVE_MAX_FLOOR = 1e-30
MASKED_LOGIT = -1e30
LOG2_E = 1.4426950408889634


def _params(*sem):
    return pltpu.CompilerParams(dimension_semantics=sem,
                                vmem_limit_bytes=VMEM_LIMIT_BYTES)


def _rms(t, gain):
    ms = jnp.mean(t * t, axis=-1, keepdims=True)
    return t * lax.rsqrt(ms + EPS) * gain


def _rope_tables(pos_ref, invf_ref, half, period):
    ang = pos_ref[...].astype(F32) * invf_ref[...]
    c = jnp.cos(ang)
    s = jnp.sin(ang)
    lane = lax.broadcasted_iota(I32, ang.shape, 1) & (period - 1)
    return c, jnp.where(lane < half, -s, 0.0), jnp.where(lane >= half, s, 0.0)


def _rope(t, c, s1, s2, half):
    return (t * c + pltpu.roll(t, LANES - half, 1) * s1
            + pltpu.roll(t, half, 1) * s2)


def _ffn_kernel(x_ref, g_ref, wg_ref, wu_ref, wd_ref, g2_ref, o_ref, *rest,
                emit_norm):
    if emit_norm:
        h_out_ref, hn_ref = rest
    else:
        (hn_ref,) = rest
    j = pl.program_id(1)

    @pl.when(j == 0)
    def _():
        hn_ref[...] = _rms(x_ref[...], g_ref[...]).astype(BF16)
        o_ref[...] = jnp.zeros_like(o_ref)

    h = hn_ref[...]
    a = jnp.dot(h, wg_ref[...], preferred_element_type=F32)
    b = jnp.dot(h, wu_ref[...], preferred_element_type=F32)
    act = (a * jax.nn.sigmoid(a) * b).astype(BF16)
    o_ref[...] += jnp.dot(act, wd_ref[...], preferred_element_type=F32)

    @pl.when(j == pl.num_programs(1) - 1)
    def _():
        y = x_ref[...] + 0.5 * o_ref[...]
        o_ref[...] = y
        if emit_norm:
            h_out_ref[...] = _rms(y, g2_ref[...]).astype(BF16)


def _ffn(x, gain, w_gate, w_up, w_down, gain2, *, emit_norm, tm=512, tf=512):
    n, d = x.shape
    f = w_gate.shape[1]
    tm = min(tm, n)
    out_shape = [jax.ShapeDtypeStruct((n, d), F32)]
    out_specs = [pl.BlockSpec((tm, d), lambda i, j: (i, 0))]
    if emit_norm:
        out_shape.append(jax.ShapeDtypeStruct((n, d), BF16))
        out_specs.append(pl.BlockSpec((tm, d), lambda i, j: (i, 0)))
    res = pl.pallas_call(
        functools.partial(_ffn_kernel, emit_norm=emit_norm),
        grid=(n // tm, f // tf),
        in_specs=[
            pl.BlockSpec((tm, d), lambda i, j: (i, 0)),
            pl.BlockSpec((1, d), lambda i, j: (0, 0)),
            pl.BlockSpec((d, tf), lambda i, j: (0, j)),
            pl.BlockSpec((d, tf), lambda i, j: (0, j)),
            pl.BlockSpec((tf, d), lambda i, j: (j, 0)),
            pl.BlockSpec((1, d), lambda i, j: (0, 0)),
        ],
        out_specs=out_specs,
        out_shape=out_shape,
        scratch_shapes=[pltpu.VMEM((tm, d), BF16)],
        compiler_params=_params("parallel", "arbitrary"),
        name="ffn_norm" if emit_norm else "ffn",
    )(x, gain, w_gate, w_up, w_down, gain2)
    return res if emit_norm else res[0]


def _qkv_kernel(h_ref, w_ref, pos_ref, invf_ref, gains_ref, o_ref,
                c_ref, s1_ref, s2_ref, *, rope_tiles):
    j = pl.program_id(1)
    half = ROT_DIM // 2

    @pl.when(j == 0)
    def _():
        c, s1, s2 = _rope_tables(pos_ref, invf_ref, half, LANES)
        c_ref[...] = c
        s1_ref[...] = s1
        s2_ref[...] = s2

    res = jnp.dot(h_ref[...], w_ref[...], preferred_element_type=F32)

    @pl.when(j < rope_tiles)
    def _():
        gain = gains_ref[0]
        for hh in range(res.shape[1] // HEAD_DIM):
            sl = slice(hh * HEAD_DIM, (hh + 1) * HEAD_DIM)
            t = _rms(res[:, sl], gain)
            t = _rope(t, c_ref[...], s1_ref[...], s2_ref[...], half)
            o_ref[:, sl] = t.astype(BF16)

    @pl.when(j >= rope_tiles)
    def _():
        o_ref[...] = res.astype(BF16)


def _qkv(h, w_qkv, pos, invf, gains, *, tm=1024, tn=512):
    n, d = h.shape
    cols = w_qkv.shape[1]
    tm = min(tm, n)
    per = ATT_WIDTH // tn
    return pl.pallas_call(
        functools.partial(_qkv_kernel, rope_tiles=2 * per),
        grid=(n // tm, cols // tn),
        in_specs=[
            pl.BlockSpec((tm, d), lambda i, j: (i, 0)),
            pl.BlockSpec((d, tn), lambda i, j: (0, j)),
            pl.BlockSpec((tm, 1), lambda i, j: (i, 0)),
            pl.BlockSpec((1, LANES), lambda i, j: (0, 0)),
            pl.BlockSpec((1, 1, HEAD_DIM),
                         lambda i, j: (jnp.minimum(j // per, 1), 0, 0)),
        ],
        out_specs=pl.BlockSpec((tm, tn), lambda i, j: (i, j)),
        out_shape=jax.ShapeDtypeStruct((n, cols), BF16),
        scratch_shapes=[pltpu.VMEM((tm, LANES), F32)] * 3,
        compiler_params=_params("parallel", "arbitrary"),
        name="qkv_proj",
    )(h, w_qkv, pos, invf, gains)


def _idx_kernel(h_ref, w_ref, pos_ref, invf_ref, kgain_ref,
                qi_ref, kcat_ref, wi_ref, *, w_scale):
    half = IDX_ROT_DIM // 2
    qw = IDX_HEADS * IDX_DIM
    res = jnp.dot(h_ref[...], w_ref[...], preferred_element_type=F32)
    c, s1, s2 = _rope_tables(pos_ref, invf_ref, half, IDX_DIM)
    for t in range(qw // LANES):
        sl = slice(t * LANES, (t + 1) * LANES)
        q_t = _rope(res[:, sl], c, s1, s2, half).T
        qi_ref[0, 2 * t] = q_t[:IDX_DIM]
        qi_ref[0, 2 * t + 1] = q_t[IDX_DIM:]

    r = res[:, qw:qw + LANES]
    lane = lax.broadcasted_iota(I32, r.shape, 1)
    is_k = lane < IDX_DIM
    ms = jnp.sum(jnp.where(is_k, r * r, 0.0), axis=-1, keepdims=True) * (1.0 / IDX_DIM)
    kn = r * lax.rsqrt(ms + EPS) * kgain_ref[...]
    kr = _rope(kn, jnp.where(is_k, c, 1.0), jnp.where(is_k, s1, 0.0),
               jnp.where(is_k, s2, 0.0), half)
    hi = kr.astype(BF16).astype(F32)
    lo = kr - hi
    hi_lo = jnp.where(is_k, hi, pltpu.roll(lo, IDX_DIM, 1)).astype(BF16)
    kcat_ref[:, 0:LANES] = hi_lo
    kcat_ref[:, LANES:2 * LANES] = hi_lo
    is_w = (lane >= IDX_DIM) & (lane < IDX_DIM + IDX_HEADS)
    w_t = pltpu.roll(jnp.where(is_w, r * w_scale, 0.0), LANES - IDX_DIM, 1).T
    wi_ref[0] = w_t[:IDX_HEADS]


def _idx(h, w_idx, pos, invf, kgain, *, batch, tm=512):
    n, d = h.shape
    cols = w_idx.shape[1]
    s = n // batch
    tm = min(tm, s)
    per_b = s // tm
    w_scale = (IDX_HEADS ** -0.5) * (IDX_DIM ** -0.5)
    return pl.pallas_call(
        functools.partial(_idx_kernel, w_scale=w_scale),
        grid=(n // tm,),
        in_specs=[
            pl.BlockSpec((tm, d), lambda i: (i, 0)),
            pl.BlockSpec((d, cols), lambda i: (0, 0)),
            pl.BlockSpec((tm, 1), lambda i: (i, 0)),
            pl.BlockSpec((1, LANES), lambda i: (0, 0)),
            pl.BlockSpec((1, LANES), lambda i: (0, 0)),
        ],
        out_specs=[
            pl.BlockSpec((1, IDX_HEADS, IDX_DIM, tm),
                         lambda i: (i // per_b, 0, 0, i % per_b)),
            pl.BlockSpec((tm, 2 * LANES), lambda i: (i, 0)),
            pl.BlockSpec((1, IDX_HEADS, tm), lambda i: (i // per_b, 0, i % per_b)),
        ],
        out_shape=[
            jax.ShapeDtypeStruct((batch, IDX_HEADS, IDX_DIM, s), F32),
            jax.ShapeDtypeStruct((n, 2 * LANES), BF16),
            jax.ShapeDtypeStruct((batch, IDX_HEADS, s), F32),
        ],
        compiler_params=_params("parallel"),
        name="idx_proj",
    )(h, w_idx, pos, invf, kgain)


def _sgu_kernel(h_ref, w_ref, gv_ref, ws_ref, bs_ref, o_ref):
    tm = h_ref.shape[0]
    groups = tm // SGU_LEN
    res = jnp.dot(h_ref[...], w_ref[...], preferred_element_type=F32)
    u = jax.nn.gelu(res[:, :SGU_WIDTH])
    v = _rms(jax.nn.gelu(res[:, SGU_WIDTH:]), gv_ref[...]).astype(BF16)
    row = lax.broadcasted_iota(I32, (SGU_LEN, SGU_LEN), 0)
    col = lax.broadcasted_iota(I32, (SGU_LEN, SGU_LEN), 1)
    causal = (col // CHUNK) <= (row // CHUNK)
    for g in range(SGU_GROUPS):
        cs = slice(g * SGU_GROUP_DIM, (g + 1) * SGU_GROUP_DIM)
        wg = jnp.where(causal, ws_ref[g], 0.0).astype(BF16)
        vg = jnp.concatenate(
            [v[n * SGU_LEN:(n + 1) * SGU_LEN, cs] for n in range(groups)], axis=1)
        mixed = jnp.dot(wg, vg, preferred_element_type=F32) + bs_ref[g]
        for n in range(groups):
            rs = slice(n * SGU_LEN, (n + 1) * SGU_LEN)
            m = mixed[:, n * SGU_GROUP_DIM:(n + 1) * SGU_GROUP_DIM]
            o_ref[rs, cs] = (u[rs, cs] * m).astype(BF16)


def _sgu(h, w_sgu, gv, w_s, b_s, *, tm=512):
    n, d = h.shape
    tm = min(tm, n)
    return pl.pallas_call(
        _sgu_kernel,
        grid=(n // tm,),
        in_specs=[
            pl.BlockSpec((tm, d), lambda i: (i, 0)),
            pl.BlockSpec((d, 2 * SGU_WIDTH), lambda i: (0, 0)),
            pl.BlockSpec((1, SGU_WIDTH), lambda i: (0, 0)),
            pl.BlockSpec((SGU_GROUPS, SGU_LEN, SGU_LEN), lambda i: (0, 0, 0)),
            pl.BlockSpec((SGU_GROUPS, SGU_LEN, 1), lambda i: (0, 0, 0)),
        ],
        out_specs=pl.BlockSpec((tm, SGU_WIDTH), lambda i: (i, 0)),
        out_shape=jax.ShapeDtypeStruct((n, SGU_WIDTH), BF16),
        compiler_params=_params("parallel"),
        name="sgu_branch",
    )(h, w_sgu, gv, w_s, b_s)


def _gates_kernel(h_ref, w_ref, o_ref):
    res = jnp.dot(h_ref[...], w_ref[...], preferred_element_type=F32)
    o_ref[...] = jax.nn.sigmoid(res).astype(BF16)


def _gates(h, w_g, *, tm=1024, tn=1024):
    n, d = h.shape
    cols = w_g.shape[1]
    tm = min(tm, n)
    return pl.pallas_call(
        _gates_kernel,
        grid=(n // tm, cols // tn),
        in_specs=[
            pl.BlockSpec((tm, d), lambda i, j: (i, 0)),
            pl.BlockSpec((d, tn), lambda i, j: (0, j)),
        ],
        out_specs=pl.BlockSpec((tm, tn), lambda i, j: (i, j)),
        out_shape=jax.ShapeDtypeStruct((n, cols), BF16),
        compiler_params=_params("parallel", "arbitrary"),
        name="gates_proj",
    )(h, w_g)


def _select_kernel(qi_ref, wi_ref, kcat_ref, mask_ref, qcat_ref, s_ref, *, topk):
    t = SEL_T
    n_tiles = s_ref.shape[0]
    i = pl.program_id(1)
    n_live = i + 1

    for h in range(IDX_HEADS):
        q = qi_ref[0, h]
        hi = q.astype(BF16)
        lo = (q - hi.astype(F32)).astype(BF16)
        qcat_ref[h] = jnp.concatenate([hi, hi, lo, lo], axis=0)
    w = wi_ref[0]

    q_chunk = (i * t + lax.broadcasted_iota(I32, (t, t), 1)) // CHUNK
    k_row = lax.broadcasted_iota(I32, (t, t), 0)

    def score_tile(kt, mx, mn):
        kc = kcat_ref[0, pl.ds(pl.multiple_of(kt * t, t), t), :]
        acc = jnp.zeros((t, t), F32)
        for h in range(IDX_HEADS):
            d = jnp.dot(kc, qcat_ref[h], preferred_element_type=F32)
            acc = acc + jnp.maximum(d, 0.0) * w[h:h + 1, :]
        admissible = ((kt * t + k_row) // CHUNK) <= q_chunk
        s = jnp.where(admissible, acc, -jnp.inf)
        s_ref[kt] = s
        mx = jnp.maximum(mx, jnp.max(s, axis=0, keepdims=True))
        mn = jnp.minimum(mn, jnp.min(jnp.where(admissible, acc, jnp.inf),
                                     axis=0, keepdims=True))
        return mx, mn

    def score_body(j, carry):
        mx, mn = score_tile(2 * j, *carry)
        return score_tile(jnp.minimum(2 * j + 1, n_live - 1), mx, mn)

    mx, mn = lax.fori_loop(
        0, (n_live + 1) // 2, score_body,
        (jnp.full((1, t), -jnp.inf, F32), jnp.full((1, t), jnp.inf, F32)))

    n_adm = ((i * t + lax.broadcasted_iota(I32, (1, t), 1)) // CHUNK + 1) * CHUNK
    done0 = n_adm <= topk

    def count_ge(c):
        def count_body(kt, acc):
            ind = jnp.where(s_ref[kt] >= c, 1.0, 0.0)
            for r in range(t // COUNT_ROWS):
                acc = acc + ind[r * COUNT_ROWS:(r + 1) * COUNT_ROWS]
            return acc
        acc = lax.fori_loop(0, n_live, count_body, jnp.zeros((COUNT_ROWS, t), F32))
        return jnp.sum(acc, axis=0, keepdims=True)

    def search_cond(st):
        return st[4] > 0.0

    def search_body(st):
        lo, hi, thr, live, _ = st
        mid = 0.5 * lo + 0.5 * hi
        stuck = (mid <= lo) | (mid >= hi)
        cnt = count_ge(mid)
        finish = (live > 0.0) & (stuck | (cnt == topk))
        thr = jnp.where(finish, jnp.where(stuck, lo, mid), thr)
        live = jnp.where(finish, 0.0, live)
        lo = jnp.where(cnt > topk, mid, lo)
        hi = jnp.where(cnt < topk, mid, hi)
        return lo, hi, thr, live, jnp.sum(live)

    hi0 = jnp.minimum(mx + jnp.maximum(jnp.abs(mx), ABOVE_MAX_FLOOR) * ABOVE_MAX_REL,
                      jnp.finfo(F32).max)
    live0 = jnp.where(done0, 0.0, 1.0)
    st0 = (mn, hi0, jnp.full((1, t), -jnp.inf, F32), live0, jnp.sum(live0))
    thr = lax.while_loop(search_cond, search_body, st0)[2]

    def mask_body(kt, carry):
        s = s_ref[kt]
        sel = (s >= thr) & (s > -jnp.inf)
        mask_ref[0, 0, kt] = jnp.where(sel, 1, 0).astype(jnp.int8)
        return carry

    lax.fori_loop(0, n_live, mask_body, 0)

    def zero_body(kt, carry):
        mask_ref[0, 0, kt] = jnp.zeros((t, t), jnp.int8)
        return carry

    lax.fori_loop(n_live, n_tiles, zero_body, 0)


def _select(qi_t, wi_t, kcat, *, topk):
    b, heads, dh, s = qi_t.shape
    t = SEL_T
    nq = s // t
    return pl.pallas_call(
        functools.partial(_select_kernel, topk=topk),
        grid=(b, nq),
        in_specs=[
            pl.BlockSpec((1, heads, dh, t), lambda bi, i: (bi, 0, 0, i)),
            pl.BlockSpec((1, heads, t), lambda bi, i: (bi, 0, i)),
            pl.BlockSpec((1, s, 4 * IDX_DIM), lambda bi, i: (bi, 0, 0)),
        ],
        out_specs=pl.BlockSpec((1, 1, nq, t, t), lambda bi, i: (bi, i, 0, 0, 0)),
        out_shape=jax.ShapeDtypeStruct((b, nq, nq, t, t), jnp.int8),
        scratch_shapes=[
            pltpu.VMEM((heads, 4 * IDX_DIM, t), BF16),
            pltpu.VMEM((nq, t, t), F32),
        ],
        compiler_params=_params("parallel", "arbitrary"),
        name="index_select",
    )(qi_t, wi_t, kcat)


def _attn_kernel(qt_ref, k_ref, vt_ref, mask_ref, o_ref,
                 m_ref, acc_ref, s_ref):
    t = ATT_T
    i = pl.program_id(2)
    n = i + 1
    m_ref[...] = jnp.full(m_ref.shape, -jnp.inf, F32)
    acc_ref[...] = jnp.zeros(acc_ref.shape, F32)
    q_t = qt_ref[0]

    def logits(kt, slot):
        k = k_ref[0, pl.ds(pl.multiple_of(kt * t, t), t), :]
        for hh in range(2):
            hs = slice(hh * HEAD_DIM, (hh + 1) * HEAD_DIM)
            s_ref[slot, hh] = jnp.dot(k[:, hs], q_t[hs, :], preferred_element_type=F32)

    def update(kt, slot, off):
        v_t = vt_ref[0, kt]
        sel = mask_ref[0, 0, kt].astype(I32) > off
        for hh in range(2):
            hs = slice(hh * HEAD_DIM, (hh + 1) * HEAD_DIM)
            s = jnp.where(sel, s_ref[slot, hh], MASKED_LOGIT)
            m_prev = m_ref[hh]
            m_new = jnp.maximum(m_prev, jnp.max(s, axis=0, keepdims=True))
            p = jnp.exp2(s - m_new)
            alpha = jnp.exp2(m_prev - m_new)
            m_ref[hh] = m_new
            lhs = jnp.concatenate([v_t[hs, :], jnp.ones((DENOM_ROWS, t), BF16)], axis=0)
            pv = jnp.dot(lhs, p.astype(BF16), preferred_element_type=F32)
            acc_ref[hh] = alpha * acc_ref[hh] + pv

    logits(0, 0)

    def body(j, carry):
        first = 2 * j
        second = jnp.minimum(first + 1, n - 1)
        logits(second, 1)
        update(first, 0, 0)
        logits(jnp.minimum(first + 2, n - 1), 0)
        update(second, 1, jnp.where(first + 1 < n, 0, 1))
        return carry

    lax.fori_loop(0, (n + 1) // 2, body, 0)
    for hh in range(2):
        acc = acc_ref[hh]
        o_ref[0, hh * HEAD_DIM:(hh + 1) * HEAD_DIM, :] = (
            acc[:HEAD_DIM] / acc[HEAD_DIM:HEAD_DIM + 1]).astype(BF16)


def _attn(q_t, qkv, v_t, mask):
    b, s, _ = qkv.shape
    t = ATT_T
    nq = s // t
    pairs = N_HEADS // 2
    k_col0 = ATT_WIDTH // (2 * HEAD_DIM)
    return pl.pallas_call(
        _attn_kernel,
        grid=(b, pairs, nq),
        in_specs=[
            pl.BlockSpec((1, 2 * HEAD_DIM, t), lambda bi, hp, i: (bi, hp, i)),
            pl.BlockSpec((1, s, 2 * HEAD_DIM), lambda bi, hp, i: (bi, 0, k_col0 + hp)),
            pl.BlockSpec((1, nq, 2 * HEAD_DIM, t), lambda bi, hp, i: (bi, 0, hp, 0)),
            pl.BlockSpec((1, 1, nq, t, t), lambda bi, hp, i: (bi, i, 0, 0, 0)),
        ],
        out_specs=pl.BlockSpec((1, 2 * HEAD_DIM, t), lambda bi, hp, i: (bi, hp, i)),
        out_shape=jax.ShapeDtypeStruct((b, ATT_WIDTH, s), BF16),
        scratch_shapes=[
            pltpu.VMEM((2, 1, t), F32),
            pltpu.VMEM((2, HEAD_DIM + DENOM_ROWS, t), F32),
            pltpu.VMEM((2, 2, t, t), F32),
        ],
        compiler_params=_params("parallel", "parallel", "arbitrary"),
        name="masked_attention",
    )(q_t, qkv, v_t, mask)


def _merge_kernel(ya_ref, yb_ref, g_ref, x_ref, wa_ref, wb_ref, wo_ref, o_ref):
    d = x_ref.shape[1]
    a = jnp.dot(ya_ref[...], wa_ref[...], preferred_element_type=F32)
    b = jnp.dot(yb_ref[...], wb_ref[...], preferred_element_type=F32)
    merged = g_ref[:, :d].astype(F32) * a + g_ref[:, d:].astype(F32) * b
    o_ref[...] = x_ref[...] + jnp.dot(merged.astype(BF16), wo_ref[...],
                                      preferred_element_type=F32)


def _merge(ya, yb, gates, x, wa, wb, wo, *, tm=256):
    n, d = x.shape
    tm = min(tm, n)
    return pl.pallas_call(
        _merge_kernel,
        grid=(n // tm,),
        in_specs=[
            pl.BlockSpec((tm, ya.shape[1]), lambda i: (i, 0)),
            pl.BlockSpec((tm, yb.shape[1]), lambda i: (i, 0)),
            pl.BlockSpec((tm, 2 * d), lambda i: (i, 0)),
            pl.BlockSpec((tm, d), lambda i: (i, 0)),
            pl.BlockSpec(wa.shape, lambda i: (0, 0)),
            pl.BlockSpec(wb.shape, lambda i: (0, 0)),
            pl.BlockSpec(wo.shape, lambda i: (0, 0)),
        ],
        out_specs=pl.BlockSpec((tm, d), lambda i: (i, 0)),
        out_shape=jax.ShapeDtypeStruct((n, d), F32),
        compiler_params=_params("parallel"),
        name="merge_out",
    )(ya, yb, gates, x, wa, wb, wo)


def _inv_freq_lanes(rot_dim, period, live):
    inv = ROPE_THETA ** (-jnp.arange(0, rot_dim, 2, dtype=F32) / rot_dim)
    lane = jnp.arange(LANES)
    j = lane % period
    vals = inv[j % (rot_dim // 2)]
    return jnp.where((j < rot_dim) & (lane < live), vals, 0.0).astype(F32)[None, :]


def _layer(x, pos, invf_att, invf_idx, p):
    b, s, d = x.shape
    n = b * s
    topk = min(TOPK_MAX, s // 4)
    x = x.reshape(n, d)

    x1, h = _ffn(x, p["norm_ffn1"][None], p["ffn1_w_gate"].astype(BF16),
                 p["ffn1_w_up"].astype(BF16), p["ffn1_w_down"].astype(BF16),
                 p["norm_mix"][None], emit_norm=True)

    w_in = p["w_in"]
    c_qkv = 3 * ATT_WIDTH
    c_qi = c_qkv + IDX_HEADS * IDX_DIM
    c_idx = c_qi + IDX_DIM + IDX_HEADS
    c_sgu = c_idx + 2 * SGU_WIDTH
    w_qkv = w_in[:, :c_qkv].astype(BF16)
    w_idx = jnp.pad(w_in[:, c_qkv:c_idx],
                    ((0, 0), (0, LANES - IDX_DIM - IDX_HEADS))).astype(BF16)
    w_sgu = w_in[:, c_idx:c_sgu].astype(BF16)
    w_gate = w_in[:, c_sgu:].astype(BF16)

    gains = jnp.stack([p["q_norm"] * (HEAD_DIM ** -0.5 * LOG2_E), p["k_norm"]])[:, None, :]
    qkv = _qkv(h, w_qkv, pos, invf_att, gains)
    kgain = jnp.pad(p["idx_k_norm"], (0, LANES - IDX_DIM))[None]
    qi_t, kcat, wi_t = _idx(h, w_idx, pos, invf_idx, kgain, batch=b)
    yb = _sgu(h, w_sgu, p["sgu_v_norm"][None], p["sgu_w_s"], p["sgu_b_s"][:, :, None])
    gates = _gates(h, w_gate)

    mask = _select(qi_t, wi_t, kcat.reshape(b, s, 4 * IDX_DIM), topk=topk)
    qkv3 = qkv.reshape(b, s, c_qkv)
    q_t = qkv3[:, :, :ATT_WIDTH].transpose(0, 2, 1)
    v_t = qkv3[:, :, 2 * ATT_WIDTH:].reshape(
        b, s // ATT_T, ATT_T, ATT_WIDTH).transpose(0, 1, 3, 2)
    ya = _attn(q_t, qkv3, v_t, mask).transpose(0, 2, 1).reshape(n, ATT_WIDTH)

    x2 = _merge(ya, yb, gates, x1, p["w_up_attn"].astype(BF16),
                p["w_up_sgu"].astype(BF16), p["w_out"].astype(BF16))
    out = _ffn(x2, p["norm_ffn2"][None], p["ffn2_w_gate"].astype(BF16),
               p["ffn2_w_up"].astype(BF16), p["ffn2_w_down"].astype(BF16),
               p["norm_ffn2"][None], emit_norm=False)
    return out.reshape(b, s, d)


def kernel(x, positions, norm_ffn1, ffn1_w_gate, ffn1_w_up, ffn1_w_down, norm_mix, w_in,
           q_norm, k_norm, idx_k_norm, sgu_v_norm, sgu_w_s, sgu_b_s, w_up_attn, w_up_sgu,
           w_out, norm_ffn2, ffn2_w_gate, ffn2_w_up, ffn2_w_down):
    params = dict(
        norm_ffn1=norm_ffn1, ffn1_w_gate=ffn1_w_gate, ffn1_w_up=ffn1_w_up,
        ffn1_w_down=ffn1_w_down, norm_mix=norm_mix, w_in=w_in, q_norm=q_norm,
        k_norm=k_norm, idx_k_norm=idx_k_norm, sgu_v_norm=sgu_v_norm, sgu_w_s=sgu_w_s,
        sgu_b_s=sgu_b_s, w_up_attn=w_up_attn, w_up_sgu=w_up_sgu, w_out=w_out,
        norm_ffn2=norm_ffn2, ffn2_w_gate=ffn2_w_gate, ffn2_w_up=ffn2_w_up,
        ffn2_w_down=ffn2_w_down)
    pos = positions.reshape(-1, 1).astype(I32)
    invf_att = _inv_freq_lanes(ROT_DIM, LANES, LANES)
    invf_idx = _inv_freq_lanes(IDX_ROT_DIM, IDX_DIM, LANES)
    for l in range(w_in.shape[0]):
        x = _layer(x, pos, invf_att, invf_idx, {k: v[l] for k, v in params.items()})
    return x
```

```python
import functools

import jax
import jax.numpy as jnp
from jax import lax
from jax.experimental import pallas as pl
from jax.experimental.pallas import tpu as pltpu

F32 = jnp.float32
BF16 = jnp.bfloat16
I32 = jnp.int32

EPS = 1e-6
CHUNK = 64
N_HEADS = 8
HEAD_DIM = 128
ATT_WIDTH = N_HEADS * HEAD_DIM
ROPE_THETA = 500000.0
ROT_DIM = HEAD_DIM // 4
IDX_HEADS = 16
IDX_DIM = 64
IDX_ROT_DIM = IDX_DIM // 4
TOPK_MAX = 256
SGU_LEN = 128
SGU_GROUPS = 8
SGU_GROUP_DIM = 128
SGU_WIDTH = SGU_GROUPS * SGU_GROUP_DIM

LANES = 128
MXU_DIM = 256
VMEM_LIMIT_BYTES = 56 * 1024 * 1024

SEL_T = MXU_DIM
MASK_ROWS = 128
COUNT_ROWS = 64
ATT_T = 256
ATT_HEADS = 4
DENOM_ROWS = 16
ABOVE_MAX_REL = 1e-6
ABOVE_MAX_FLOOR = 1e-30
MASKED_LOGIT = -1e30
LOG2_E = 1.4426950408889634


def _params(*sem):
    return pltpu.CompilerParams(dimension_semantics=sem,
                                vmem_limit_bytes=VMEM_LIMIT_BYTES)


def _rms(t, gain):
    ms = jnp.mean(t * t, axis=-1, keepdims=True)
    return t * lax.rsqrt(ms + EPS) * gain


def _rope_tables(pos_ref, invf_ref, half, period):
    ang = pos_ref[...].astype(F32) * invf_ref[...]
    c = jnp.cos(ang)
    s = jnp.sin(ang)
    lane = lax.broadcasted_iota(I32, ang.shape, 1) & (period - 1)
    return c, jnp.where(lane < half, -s, 0.0), jnp.where(lane >= half, s, 0.0)


def _rope(t, c, s1, s2, half):
    return (t * c + pltpu.roll(t, LANES - half, 1) * s1
            + pltpu.roll(t, half, 1) * s2)


def _ffn_kernel(x_ref, g_ref, wg_ref, wu_ref, wd_ref, g2_ref, o_ref, *rest,
                emit_norm):
    if emit_norm:
        h_out_ref, hn_ref = rest
    else:
        (hn_ref,) = rest
    j = pl.program_id(1)

    @pl.when(j == 0)
    def _():
        hn_ref[...] = _rms(x_ref[...], g_ref[...]).astype(BF16)
        o_ref[...] = jnp.zeros_like(o_ref)

    h = hn_ref[...]
    a = jnp.dot(h, wg_ref[...], preferred_element_type=F32)
    b = jnp.dot(h, wu_ref[...], preferred_element_type=F32)
    act = (a * jax.nn.sigmoid(a) * b).astype(BF16)
    o_ref[...] += jnp.dot(act, wd_ref[...], preferred_element_type=F32)

    @pl.when(j == pl.num_programs(1) - 1)
    def _():
        y = x_ref[...] + 0.5 * o_ref[...]
        o_ref[...] = y
        if emit_norm:
            h_out_ref[...] = _rms(y, g2_ref[...]).astype(BF16)


def _ffn(x, gain, w_gate, w_up, w_down, gain2, *, emit_norm, tm=512, tf=512):
    n, d = x.shape
    f = w_gate.shape[1]
    tm = min(tm, n)
    out_shape = [jax.ShapeDtypeStruct((n, d), F32)]
    out_specs = [pl.BlockSpec((tm, d), lambda i, j: (i, 0))]
    if emit_norm:
        out_shape.append(jax.ShapeDtypeStruct((n, d), BF16))
        out_specs.append(pl.BlockSpec((tm, d), lambda i, j: (i, 0)))
    res = pl.pallas_call(
        functools.partial(_ffn_kernel, emit_norm=emit_norm),
        grid=(n // tm, f // tf),
        in_specs=[
            pl.BlockSpec((tm, d), lambda i, j: (i, 0)),
            pl.BlockSpec((1, d), lambda i, j: (0, 0)),
            pl.BlockSpec((d, tf), lambda i, j: (0, j)),
            pl.BlockSpec((d, tf), lambda i, j: (0, j)),
            pl.BlockSpec((tf, d), lambda i, j: (j, 0)),
            pl.BlockSpec((1, d), lambda i, j: (0, 0)),
        ],
        out_specs=out_specs,
        out_shape=out_shape,
        scratch_shapes=[pltpu.VMEM((tm, d), BF16)],
        compiler_params=_params("parallel", "arbitrary"),
        name="ffn_norm" if emit_norm else "ffn",
    )(x, gain, w_gate, w_up, w_down, gain2)
    return res if emit_norm else res[0]


def _qkv_kernel(h_ref, w_ref, pos_ref, invf_ref, gains_ref, o_ref,
                c_ref, s1_ref, s2_ref, *, rope_tiles):
    j = pl.program_id(1)
    half = ROT_DIM // 2

    @pl.when(j == 0)
    def _():
        c, s1, s2 = _rope_tables(pos_ref, invf_ref, half, LANES)
        c_ref[...] = c
        s1_ref[...] = s1
        s2_ref[...] = s2

    res = jnp.dot(h_ref[...], w_ref[...], preferred_element_type=F32)

    @pl.when(j < rope_tiles)
    def _():
        gain = gains_ref[0]
        for hh in range(res.shape[1] // HEAD_DIM):
            sl = slice(hh * HEAD_DIM, (hh + 1) * HEAD_DIM)
            t = _rms(res[:, sl], gain)
            t = _rope(t, c_ref[...], s1_ref[...], s2_ref[...], half)
            o_ref[:, sl] = t.astype(BF16)

    @pl.when(j >= rope_tiles)
    def _():
        o_ref[...] = res.astype(BF16)


def _qkv(h, w_qkv, pos, invf, gains, *, tm=1024, tn=512):
    n, d = h.shape
    cols = w_qkv.shape[1]
    tm = min(tm, n)
    per = ATT_WIDTH // tn
    return pl.pallas_call(
        functools.partial(_qkv_kernel, rope_tiles=2 * per),
        grid=(n // tm, cols // tn),
        in_specs=[
            pl.BlockSpec((tm, d), lambda i, j: (i, 0)),
            pl.BlockSpec((d, tn), lambda i, j: (0, j)),
            pl.BlockSpec((tm, 1), lambda i, j: (i, 0)),
            pl.BlockSpec((1, LANES), lambda i, j: (0, 0)),
            pl.BlockSpec((1, 1, HEAD_DIM),
                         lambda i, j: (jnp.minimum(j // per, 1), 0, 0)),
        ],
        out_specs=pl.BlockSpec((tm, tn), lambda i, j: (i, j)),
        out_shape=jax.ShapeDtypeStruct((n, cols), BF16),
        scratch_shapes=[pltpu.VMEM((tm, LANES), F32)] * 3,
        compiler_params=_params("parallel", "arbitrary"),
        name="qkv_proj",
    )(h, w_qkv, pos, invf, gains)


def _idx_kernel(h_ref, w_ref, pos_ref, invf_ref, kgain_ref,
                qi_ref, kcat_ref, wi_ref, *, w_scale):
    half = IDX_ROT_DIM // 2
    qw = IDX_HEADS * IDX_DIM
    res = jnp.dot(h_ref[...], w_ref[...], preferred_element_type=F32)
    c, s1, s2 = _rope_tables(pos_ref, invf_ref, half, IDX_DIM)
    for t in range(qw // LANES):
        sl = slice(t * LANES, (t + 1) * LANES)
        q_t = _rope(res[:, sl], c, s1, s2, half).T
        qi_ref[0, 2 * t] = q_t[:IDX_DIM]
        qi_ref[0, 2 * t + 1] = q_t[IDX_DIM:]

    r = res[:, qw:qw + LANES]
    lane = lax.broadcasted_iota(I32, r.shape, 1)
    is_k = lane < IDX_DIM
    ms = jnp.sum(jnp.where(is_k, r * r, 0.0), axis=-1, keepdims=True) * (1.0 / IDX_DIM)
    kn = r * lax.rsqrt(ms + EPS) * kgain_ref[...]
    kr = _rope(kn, jnp.where(is_k, c, 1.0), jnp.where(is_k, s1, 0.0),
               jnp.where(is_k, s2, 0.0), half)
    hi = kr.astype(BF16).astype(F32)
    lo = kr - hi
    hi_lo = jnp.where(is_k, hi, pltpu.roll(lo, IDX_DIM, 1)).astype(BF16)
    kcat_ref[:, 0:LANES] = hi_lo
    kcat_ref[:, LANES:2 * LANES] = hi_lo
    is_w = (lane >= IDX_DIM) & (lane < IDX_DIM + IDX_HEADS)
    w_t = pltpu.roll(jnp.where(is_w, r * w_scale, 0.0), LANES - IDX_DIM, 1).T
    wi_ref[0] = w_t[:IDX_HEADS]


def _idx(h, w_idx, pos, invf, kgain, *, batch, tm=512):
    n, d = h.shape
    cols = w_idx.shape[1]
    s = n // batch
    tm = min(tm, s)
    per_b = s // tm
    w_scale = (IDX_HEADS ** -0.5) * (IDX_DIM ** -0.5)
    return pl.pallas_call(
        functools.partial(_idx_kernel, w_scale=w_scale),
        grid=(n // tm,),
        in_specs=[
            pl.BlockSpec((tm, d), lambda i: (i, 0)),
            pl.BlockSpec((d, cols), lambda i: (0, 0)),
            pl.BlockSpec((tm, 1), lambda i: (i, 0)),
            pl.BlockSpec((1, LANES), lambda i: (0, 0)),
            pl.BlockSpec((1, LANES), lambda i: (0, 0)),
        ],
        out_specs=[
            pl.BlockSpec((1, IDX_HEADS, IDX_DIM, tm),
                         lambda i: (i // per_b, 0, 0, i % per_b)),
            pl.BlockSpec((tm, 2 * LANES), lambda i: (i, 0)),
            pl.BlockSpec((1, IDX_HEADS, tm), lambda i: (i // per_b, 0, i % per_b)),
        ],
        out_shape=[
            jax.ShapeDtypeStruct((batch, IDX_HEADS, IDX_DIM, s), F32),
            jax.ShapeDtypeStruct((n, 2 * LANES), BF16),
            jax.ShapeDtypeStruct((batch, IDX_HEADS, s), F32),
        ],
        compiler_params=_params("parallel"),
        name="idx_proj",
    )(h, w_idx, pos, invf, kgain)


def _sgu_kernel(h_ref, w_ref, gv_ref, ws_ref, bs_ref, o_ref):
    tm = h_ref.shape[0]
    groups = tm // SGU_LEN
    res = jnp.dot(h_ref[...], w_ref[...], preferred_element_type=F32)
    u = jax.nn.gelu(res[:, :SGU_WIDTH])
    v = _rms(jax.nn.gelu(res[:, SGU_WIDTH:]), gv_ref[...]).astype(BF16)
    row = lax.broadcasted_iota(I32, (SGU_LEN, SGU_LEN), 0)
    col = lax.broadcasted_iota(I32, (SGU_LEN, SGU_LEN), 1)
    causal = (col // CHUNK) <= (row // CHUNK)
    for g in range(SGU_GROUPS):
        cs = slice(g * SGU_GROUP_DIM, (g + 1) * SGU_GROUP_DIM)
        wg = jnp.where(causal, ws_ref[g], 0.0).astype(BF16)
        vg = jnp.concatenate(
            [v[n * SGU_LEN:(n + 1) * SGU_LEN, cs] for n in range(groups)], axis=1)
        mixed = jnp.dot(wg, vg, preferred_element_type=F32) + bs_ref[g]
        for n in range(groups):
            rs = slice(n * SGU_LEN, (n + 1) * SGU_LEN)
            m = mixed[:, n * SGU_GROUP_DIM:(n + 1) * SGU_GROUP_DIM]
            o_ref[rs, cs] = (u[rs, cs] * m).astype(BF16)


def _sgu(h, w_sgu, gv, w_s, b_s, *, tm=512):
    n, d = h.shape
    tm = min(tm, n)
    return pl.pallas_call(
        _sgu_kernel,
        grid=(n // tm,),
        in_specs=[
            pl.BlockSpec((tm, d), lambda i: (i, 0)),
            pl.BlockSpec((d, 2 * SGU_WIDTH), lambda i: (0, 0)),
            pl.BlockSpec((1, SGU_WIDTH), lambda i: (0, 0)),
            pl.BlockSpec((SGU_GROUPS, SGU_LEN, SGU_LEN), lambda i: (0, 0, 0)),
            pl.BlockSpec((SGU_GROUPS, SGU_LEN, 1), lambda i: (0, 0, 0)),
        ],
        out_specs=pl.BlockSpec((tm, SGU_WIDTH), lambda i: (i, 0)),
        out_shape=jax.ShapeDtypeStruct((n, SGU_WIDTH), BF16),
        compiler_params=_params("parallel"),
        name="sgu_branch",
    )(h, w_sgu, gv, w_s, b_s)


def _gates_kernel(h_ref, w_ref, o_ref):
    res = jnp.dot(h_ref[...], w_ref[...], preferred_element_type=F32)
    o_ref[...] = jax.nn.sigmoid(res).astype(BF16)


def _gates(h, w_g, *, tm=1024, tn=1024):
    n, d = h.shape
    cols = w_g.shape[1]
    tm = min(tm, n)
    return pl.pallas_call(
        _gates_kernel,
        grid=(n // tm, cols // tn),
        in_specs=[
            pl.BlockSpec((tm, d), lambda i, j: (i, 0)),
            pl.BlockSpec((d, tn), lambda i, j: (0, j)),
        ],
        out_specs=pl.BlockSpec((tm, tn), lambda i, j: (i, j)),
        out_shape=jax.ShapeDtypeStruct((n, cols), BF16),
        compiler_params=_params("parallel", "arbitrary"),
        name="gates_proj",
    )(h, w_g)


def _select_kernel(qi_ref, wi_ref, kcat_ref, mask_ref, qcat_ref, s_ref, *, topk):
    t = SEL_T
    n_tiles = s_ref.shape[0]
    i = pl.program_id(1)
    n_live = i + 1

    for h in range(IDX_HEADS):
        q = qi_ref[0, h]
        hi = q.astype(BF16)
        lo = (q - hi.astype(F32)).astype(BF16)
        qcat_ref[h] = jnp.concatenate([hi, hi, lo, lo], axis=0)
    w = wi_ref[0]

    q_chunk = (i * t + lax.broadcasted_iota(I32, (t, t), 1)) // CHUNK
    k_row = lax.broadcasted_iota(I32, (t, t), 0)

    def score_tile(kt, mx, mn):
        kc = kcat_ref[0, pl.ds(pl.multiple_of(kt * t, t), t), :]
        acc = jnp.zeros((t, t), F32)
        for h in range(IDX_HEADS):
            d = jnp.dot(kc, qcat_ref[h], preferred_element_type=F32)
            acc = acc + jnp.maximum(d, 0.0) * w[h:h + 1, :]
        admissible = ((kt * t + k_row) // CHUNK) <= q_chunk
        s = jnp.where(admissible, acc, -jnp.inf)
        s_ref[kt] = s
        mx = jnp.maximum(mx, jnp.max(s, axis=0, keepdims=True))
        mn = jnp.minimum(mn, jnp.min(jnp.where(admissible, acc, jnp.inf),
                                     axis=0, keepdims=True))
        return mx, mn

    def score_body(j, carry):
        mx, mn = score_tile(2 * j, *carry)
        return score_tile(jnp.minimum(2 * j + 1, n_live - 1), mx, mn)

    mx, mn = lax.fori_loop(
        0, (n_live + 1) // 2, score_body,
        (jnp.full((1, t), -jnp.inf, F32), jnp.full((1, t), jnp.inf, F32)))

    n_adm = ((i * t + lax.broadcasted_iota(I32, (1, t), 1)) // CHUNK + 1) * CHUNK
    done0 = n_adm <= topk

    def count_ge(c):
        def add_tile(kt, weight, acc):
            ind = jnp.where(s_ref[kt] >= c, weight, 0.0)
            for r in range(t // COUNT_ROWS):
                acc = acc + ind[r * COUNT_ROWS:(r + 1) * COUNT_ROWS]
            return acc

        def count_body(j, acc):
            acc = add_tile(2 * j, 1.0, acc)
            second = 2 * j + 1
            return add_tile(jnp.minimum(second, n_live - 1),
                            jnp.where(second < n_live, 1.0, 0.0), acc)

        acc = lax.fori_loop(0, (n_live + 1) // 2, count_body,
                            jnp.zeros((COUNT_ROWS, t), F32))
        return jnp.sum(acc, axis=0, keepdims=True)

    def search_cond(st):
        return st[4] > 0.0

    def search_body(st):
        lo, hi, thr, live, _ = st
        mid = 0.5 * lo + 0.5 * hi
        stuck = (mid <= lo) | (mid >= hi)
        cnt = count_ge(mid)
        finish = (live > 0.0) & (stuck | (cnt == topk))
        thr = jnp.where(finish, jnp.where(stuck, lo, mid), thr)
        live = jnp.where(finish, 0.0, live)
        lo = jnp.where(cnt > topk, mid, lo)
        hi = jnp.where(cnt < topk, mid, hi)
        return lo, hi, thr, live, jnp.sum(live)

    hi0 = jnp.minimum(mx + jnp.maximum(jnp.abs(mx), ABOVE_MAX_FLOOR) * ABOVE_MAX_REL,
                      jnp.finfo(F32).max)
    live0 = jnp.where(done0, 0.0, 1.0)
    st0 = (mn, hi0, jnp.full((1, t), -jnp.inf, F32), live0, jnp.sum(live0))
    thr = lax.while_loop(search_cond, search_body, st0)[2]

    def mask_body(kt, carry):
        s = s_ref[kt]
        sel = (s >= thr) & (s > -jnp.inf)
        mask_ref[0, 0, kt] = jnp.where(sel, 1, 0).astype(jnp.int8)
        return carry

    lax.fori_loop(0, n_live, mask_body, 0)

    def zero_body(kt, carry):
        mask_ref[0, 0, kt] = jnp.zeros((t, t), jnp.int8)
        return carry

    lax.fori_loop(n_live, n_tiles, zero_body, 0)


def _select(qi_t, wi_t, kcat, *, topk):
    b, heads, dh, s = qi_t.shape
    t = SEL_T
    nq = s // t
    return pl.pallas_call(
        functools.partial(_select_kernel, topk=topk),
        grid=(b, nq),
        in_specs=[
            pl.BlockSpec((1, heads, dh, t), lambda bi, i: (bi, 0, 0, i)),
            pl.BlockSpec((1, heads, t), lambda bi, i: (bi, 0, i)),
            pl.BlockSpec((1, s, 4 * IDX_DIM), lambda bi, i: (bi, 0, 0)),
        ],
        out_specs=pl.BlockSpec((1, 1, nq, t, t), lambda bi, i: (bi, i, 0, 0, 0)),
        out_shape=jax.ShapeDtypeStruct((b, nq, nq, t, t), jnp.int8),
        scratch_shapes=[
            pltpu.VMEM((heads, 4 * IDX_DIM, t), BF16),
            pltpu.VMEM((nq, t, t), F32),
        ],
        compiler_params=_params("parallel", "arbitrary"),
        name="index_select",
    )(qi_t, wi_t, kcat)


def _attn_kernel(qt_ref, k_ref, vt_ref, mask_ref, o_ref,
                 m_ref, acc_ref, s_ref):
    t = ATT_T
    i = pl.program_id(2)
    n = i + 1
    m_ref[...] = jnp.full(m_ref.shape, -jnp.inf, F32)
    acc_ref[...] = jnp.zeros(acc_ref.shape, F32)
    q_t = qt_ref[0]

    def logits(kt, slot):
        k = k_ref[0, pl.ds(pl.multiple_of(kt * t, t), t), :]
        for hh in range(ATT_HEADS):
            hs = slice(hh * HEAD_DIM, (hh + 1) * HEAD_DIM)
            s_ref[slot, hh] = jnp.dot(k[:, hs], q_t[hs, :], preferred_element_type=F32)

    def update(kt, slot, off):
        v_t = vt_ref[0, kt]
        sel = mask_ref[0, 0, kt].astype(I32) > off
        for hh in range(ATT_HEADS):
            hs = slice(hh * HEAD_DIM, (hh + 1) * HEAD_DIM)
            s = jnp.where(sel, s_ref[slot, hh], MASKED_LOGIT)
            m_prev = m_ref[hh]
            m_new = jnp.maximum(m_prev, jnp.max(s, axis=0, keepdims=True))
            p = jnp.exp2(s - m_new)
            alpha = jnp.exp2(m_prev - m_new)
            m_ref[hh] = m_new
            lhs = jnp.concatenate([v_t[hs, :], jnp.ones((DENOM_ROWS, t), BF16)], axis=0)
            pv = jnp.dot(lhs, p.astype(BF16), preferred_element_type=F32)
            acc_ref[hh] = alpha * acc_ref[hh] + pv

    logits(0, 0)

    def body(j, carry):
        first = 2 * j
        second = jnp.minimum(first + 1, n - 1)
        logits(second, 1)
        update(first, 0, 0)
        logits(jnp.minimum(first + 2, n - 1), 0)
        update(second, 1, jnp.where(first + 1 < n, 0, 1))
        return carry

    lax.fori_loop(0, (n + 1) // 2, body, 0)
    for hh in range(ATT_HEADS):
        acc = acc_ref[hh]
        o_ref[0, hh * HEAD_DIM:(hh + 1) * HEAD_DIM, :] = (
            acc[:HEAD_DIM] / acc[HEAD_DIM:HEAD_DIM + 1]).astype(BF16)


def _attn(q_t, qkv, v_t, mask):
    b, s, _ = qkv.shape
    t = ATT_T
    nq = s // t
    width = ATT_HEADS * HEAD_DIM
    k_col0 = ATT_WIDTH // width
    return pl.pallas_call(
        _attn_kernel,
        grid=(b, N_HEADS // ATT_HEADS, nq),
        in_specs=[
            pl.BlockSpec((1, width, t), lambda bi, hp, i: (bi, hp, i)),
            pl.BlockSpec((1, s, width), lambda bi, hp, i: (bi, 0, k_col0 + hp)),
            pl.BlockSpec((1, nq, width, t), lambda bi, hp, i: (bi, 0, hp, 0)),
            pl.BlockSpec((1, 1, nq, t, t), lambda bi, hp, i: (bi, i, 0, 0, 0)),
        ],
        out_specs=pl.BlockSpec((1, width, t), lambda bi, hp, i: (bi, hp, i)),
        out_shape=jax.ShapeDtypeStruct((b, ATT_WIDTH, s), BF16),
        scratch_shapes=[
            pltpu.VMEM((ATT_HEADS, 1, t), F32),
            pltpu.VMEM((ATT_HEADS, HEAD_DIM + DENOM_ROWS, t), F32),
            pltpu.VMEM((2, ATT_HEADS, t, t), F32),
        ],
        compiler_params=_params("parallel", "parallel", "arbitrary"),
        name="masked_attention",
    )(q_t, qkv, v_t, mask)


def _merge_kernel(ya_ref, yb_ref, g_ref, x_ref, wa_ref, wb_ref, wo_ref, o_ref):
    d = x_ref.shape[1]
    a = jnp.dot(ya_ref[...], wa_ref[...], preferred_element_type=F32)
    b = jnp.dot(yb_ref[...], wb_ref[...], preferred_element_type=F32)
    merged = g_ref[:, :d].astype(F32) * a + g_ref[:, d:].astype(F32) * b
    o_ref[...] = x_ref[...] + jnp.dot(merged.astype(BF16), wo_ref[...],
                                      preferred_element_type=F32)


def _merge(ya, yb, gates, x, wa, wb, wo, *, tm=256):
    n, d = x.shape
    tm = min(tm, n)
    return pl.pallas_call(
        _merge_kernel,
        grid=(n // tm,),
        in_specs=[
            pl.BlockSpec((tm, ya.shape[1]), lambda i: (i, 0)),
            pl.BlockSpec((tm, yb.shape[1]), lambda i: (i, 0)),
            pl.BlockSpec((tm, 2 * d), lambda i: (i, 0)),
            pl.BlockSpec((tm, d), lambda i: (i, 0)),
            pl.BlockSpec(wa.shape, lambda i: (0, 0)),
            pl.BlockSpec(wb.shape, lambda i: (0, 0)),
            pl.BlockSpec(wo.shape, lambda i: (0, 0)),
        ],
        out_specs=pl.BlockSpec((tm, d), lambda i: (i, 0)),
        out_shape=jax.ShapeDtypeStruct((n, d), F32),
        compiler_params=_params("parallel"),
        name="merge_out",
    )(ya, yb, gates, x, wa, wb, wo)


def _inv_freq_lanes(rot_dim, period, live):
    inv = ROPE_THETA ** (-jnp.arange(0, rot_dim, 2, dtype=F32) / rot_dim)
    lane = jnp.arange(LANES)
    j = lane % period
    vals = inv[j % (rot_dim // 2)]
    return jnp.where((j < rot_dim) & (lane < live), vals, 0.0).astype(F32)[None, :]


def _layer(x, pos, invf_att, invf_idx, p):
    b, s, d = x.shape
    n = b * s
    topk = min(TOPK_MAX, s // 4)
    x = x.reshape(n, d)

    x1, h = _ffn(x, p["norm_ffn1"][None], p["ffn1_w_gate"].astype(BF16),
                 p["ffn1_w_up"].astype(BF16), p["ffn1_w_down"].astype(BF16),
                 p["norm_mix"][None], emit_norm=True)

    w_in = p["w_in"]
    c_qkv = 3 * ATT_WIDTH
    c_qi = c_qkv + IDX_HEADS * IDX_DIM
    c_idx = c_qi + IDX_DIM + IDX_HEADS
    c_sgu = c_idx + 2 * SGU_WIDTH
    w_qkv = w_in[:, :c_qkv].astype(BF16)
    w_idx = jnp.pad(w_in[:, c_qkv:c_idx],
                    ((0, 0), (0, LANES - IDX_DIM - IDX_HEADS))).astype(BF16)
    w_sgu = w_in[:, c_idx:c_sgu].astype(BF16)
    w_gate = w_in[:, c_sgu:].astype(BF16)

    gains = jnp.stack([p["q_norm"] * (HEAD_DIM ** -0.5 * LOG2_E), p["k_norm"]])[:, None, :]
    qkv = _qkv(h, w_qkv, pos, invf_att, gains)
    kgain = jnp.pad(p["idx_k_norm"], (0, LANES - IDX_DIM))[None]
    qi_t, kcat, wi_t = _idx(h, w_idx, pos, invf_idx, kgain, batch=b)
    yb = _sgu(h, w_sgu, p["sgu_v_norm"][None], p["sgu_w_s"], p["sgu_b_s"][:, :, None])
    gates = _gates(h, w_gate)

    mask = _select(qi_t, wi_t, kcat.reshape(b, s, 4 * IDX_DIM), topk=topk)
    qkv3 = qkv.reshape(b, s, c_qkv)
    q_t = qkv3[:, :, :ATT_WIDTH].transpose(0, 2, 1)
    v_t = qkv3[:, :, 2 * ATT_WIDTH:].reshape(
        b, s // ATT_T, ATT_T, ATT_WIDTH).transpose(0, 1, 3, 2)
    ya = _attn(q_t, qkv3, v_t, mask).transpose(0, 2, 1).reshape(n, ATT_WIDTH)

    x2 = _merge(ya, yb, gates, x1, p["w_up_attn"].astype(BF16),
                p["w_up_sgu"].astype(BF16), p["w_out"].astype(BF16))
    out = _ffn(x2, p["norm_ffn2"][None], p["ffn2_w_gate"].astype(BF16),
               p["ffn2_w_up"].astype(BF16), p["ffn2_w_down"].astype(BF16),
               p["norm_ffn2"][None], emit_norm=False)
    return out.reshape(b, s, d)


def kernel(x, positions, norm_ffn1, ffn1_w_gate, ffn1_w_up, ffn1_w_down, norm_mix, w_in,
           q_norm, k_norm, idx_k_norm, sgu_v_norm, sgu_w_s, sgu_b_s, w_up_attn, w_up_sgu,
           w_out, norm_ffn2, ffn2_w_gate, ffn2_w_up, ffn2_w_down):
    params = dict(
        norm_ffn1=norm_ffn1, ffn1_w_gate=ffn1_w_gate, ffn1_w_up=ffn1_w_up,
        ffn1_w_down=ffn1_w_down, norm_mix=norm_mix, w_in=w_in, q_norm=q_norm,
        k_norm=k_norm, idx_k_norm=idx_k_norm, sgu_v_norm=sgu_v_norm, sgu_w_s=sgu_w_s,
        sgu_b_s=sgu_b_s, w_up_attn=w_up_attn, w_up_sgu=w_up_sgu, w_out=w_out,
        norm_ffn2=norm_ffn2, ffn2_w_gate=ffn2_w_gate, ffn2_w_up=ffn2_w_up,
        ffn2_w_down=ffn2_w_down)
    pos = positions.reshape(-1, 1).astype(I32)
    invf_att = _inv_freq_lanes(ROT_DIM, LANES, LANES)
    invf_idx = _inv_freq_lanes(IDX_ROT_DIM, IDX_DIM, LANES)
    for l in range(w_in.shape[0]):
        x = _layer(x, pos, invf_att, invf_idx, {k: v[l] for k, v in params.items()})
    return x
```

```python
import functools

import jax
import jax.numpy as jnp
from jax import lax
from jax.experimental import pallas as pl
from jax.experimental.pallas import tpu as pltpu

F32 = jnp.float32
BF16 = jnp.bfloat16
I32 = jnp.int32

EPS = 1e-6
CHUNK = 64
N_HEADS = 8
HEAD_DIM = 128
ATT_WIDTH = N_HEADS * HEAD_DIM
ROPE_THETA = 500000.0
ROT_DIM = HEAD_DIM // 4
IDX_HEADS = 16
IDX_DIM = 64
IDX_ROT_DIM = IDX_DIM // 4
TOPK_MAX = 256
SGU_LEN = 128
SGU_GROUPS = 8
SGU_GROUP_DIM = 128
SGU_WIDTH = SGU_GROUPS * SGU_GROUP_DIM

LANES = 128
MXU_DIM = 256
VMEM_LIMIT_BYTES = 56 * 1024 * 1024

SEL_T = MXU_DIM
MASK_ROWS = 128
COUNT_ROWS = 64
ATT_T = 256
ATT_HEADS = 4
DENOM_ROWS = 16
ABOVE_MAX_REL = 1e-6
ABOVE_MAX_FLOOR = 1e-30
MASKED_LOGIT = -1e30
LOG2_E = 1.4426950408889634


def _params(*sem):
    return pltpu.CompilerParams(dimension_semantics=sem,
                                vmem_limit_bytes=VMEM_LIMIT_BYTES)


def _rms(t, gain):
    ms = jnp.mean(t * t, axis=-1, keepdims=True)
    return t * lax.rsqrt(ms + EPS) * gain


def _rope_tables(pos_ref, invf_ref, half, period):
    ang = pos_ref[...].astype(F32) * invf_ref[...]
    c = jnp.cos(ang)
    s = jnp.sin(ang)
    lane = lax.broadcasted_iota(I32, ang.shape, 1) & (period - 1)
    return c, jnp.where(lane < half, -s, 0.0), jnp.where(lane >= half, s, 0.0)


def _rope(t, c, s1, s2, half):
    return (t * c + pltpu.roll(t, LANES - half, 1) * s1
            + pltpu.roll(t, half, 1) * s2)


def _ffn_kernel(x_ref, g_ref, wg_ref, wu_ref, wd_ref, g2_ref, o_ref, *rest,
                emit_norm):
    if emit_norm:
        h_out_ref, hn_ref = rest
    else:
        (hn_ref,) = rest
    j = pl.program_id(1)

    @pl.when(j == 0)
    def _():
        hn_ref[...] = _rms(x_ref[...], g_ref[...]).astype(BF16)
        o_ref[...] = jnp.zeros_like(o_ref)

    h = hn_ref[...]
    a = jnp.dot(h, wg_ref[...], preferred_element_type=F32)
    b = jnp.dot(h, wu_ref[...], preferred_element_type=F32)
    act = (a * jax.nn.sigmoid(a) * b).astype(BF16)
    o_ref[...] += jnp.dot(act, wd_ref[...], preferred_element_type=F32)

    @pl.when(j == pl.num_programs(1) - 1)
    def _():
        y = x_ref[...] + 0.5 * o_ref[...]
        o_ref[...] = y
        if emit_norm:
            h_out_ref[...] = _rms(y, g2_ref[...]).astype(BF16)


def _ffn(x, gain, w_gate, w_up, w_down, gain2, *, emit_norm, tm=512, tf=512):
    n, d = x.shape
    f = w_gate.shape[1]
    tm = min(tm, n)
    out_shape = [jax.ShapeDtypeStruct((n, d), F32)]
    out_specs = [pl.BlockSpec((tm, d), lambda i, j: (i, 0))]
    if emit_norm:
        out_shape.append(jax.ShapeDtypeStruct((n, d), BF16))
        out_specs.append(pl.BlockSpec((tm, d), lambda i, j: (i, 0)))
    res = pl.pallas_call(
        functools.partial(_ffn_kernel, emit_norm=emit_norm),
        grid=(n // tm, f // tf),
        in_specs=[
            pl.BlockSpec((tm, d), lambda i, j: (i, 0)),
            pl.BlockSpec((1, d), lambda i, j: (0, 0)),
            pl.BlockSpec((d, tf), lambda i, j: (0, j)),
            pl.BlockSpec((d, tf), lambda i, j: (0, j)),
            pl.BlockSpec((tf, d), lambda i, j: (j, 0)),
            pl.BlockSpec((1, d), lambda i, j: (0, 0)),
        ],
        out_specs=out_specs,
        out_shape=out_shape,
        scratch_shapes=[pltpu.VMEM((tm, d), BF16)],
        compiler_params=_params("parallel", "arbitrary"),
        name="ffn_norm" if emit_norm else "ffn",
    )(x, gain, w_gate, w_up, w_down, gain2)
    return res if emit_norm else res[0]


def _rope_table_kernel(pos_ref, invf_ref, c_ref, s1_ref, s2_ref, *, half, period):
    c, s1, s2 = _rope_tables(pos_ref, invf_ref, half, period)
    c_ref[...] = c
    s1_ref[...] = s1
    s2_ref[...] = s2


def _rope_table(pos, invf, *, half, period, name, tm=2048):
    n = pos.shape[0]
    tm = min(tm, n)
    spec = pl.BlockSpec((tm, LANES), lambda i: (i, 0))
    return pl.pallas_call(
        functools.partial(_rope_table_kernel, half=half, period=period),
        grid=(n // tm,),
        in_specs=[pl.BlockSpec((tm, 1), lambda i: (i, 0)),
                  pl.BlockSpec((1, LANES), lambda i: (0, 0))],
        out_specs=[spec] * 3,
        out_shape=[jax.ShapeDtypeStruct((n, LANES), F32)] * 3,
        compiler_params=_params("parallel"),
        name=name,
    )(pos, invf)


def _head_proj_kernel(h_ref, w_ref, gain_ref, c_ref, s1_ref, s2_ref, o_ref, r_ref,
                      *, transpose):
    half = ROT_DIM // 2
    res = jnp.dot(h_ref[...], w_ref[...], preferred_element_type=F32)
    heads = res.shape[1] // HEAD_DIM
    for hh in range(heads):
        r_ref[hh] = res[:, hh * HEAD_DIM:(hh + 1) * HEAD_DIM]

    def head(hh, carry):
        t = _rope(_rms(r_ref[hh], gain_ref[...]), c_ref[...], s1_ref[...], s2_ref[...], half)
        if transpose:
            o_ref[0, pl.ds(pl.multiple_of(hh * HEAD_DIM, HEAD_DIM), HEAD_DIM), :] = (
                t.T.astype(BF16))
        else:
            o_ref[0, hh] = t.astype(BF16)
        return carry

    lax.fori_loop(0, heads, head, 0)


def _head_proj(h, w, gain, tables, *, batch, transpose, name, tm=1024, tn=512):
    n, d = h.shape
    cols = w.shape[1]
    s = n // batch
    tm = min(tm, s)
    per_b = s // tm
    heads = tn // HEAD_DIM
    table_spec = pl.BlockSpec((tm, LANES), lambda i, j: (i, 0))
    if transpose:
        out_spec = pl.BlockSpec((1, tn, tm), lambda i, j: (i // per_b, j, i % per_b))
        out_shape = jax.ShapeDtypeStruct((batch, cols, s), BF16)
    else:
        out_spec = pl.BlockSpec((1, heads, tm, HEAD_DIM),
                                lambda i, j: (i // per_b, j, i % per_b, 0))
        out_shape = jax.ShapeDtypeStruct((batch, cols // HEAD_DIM, s, HEAD_DIM), BF16)
    return pl.pallas_call(
        functools.partial(_head_proj_kernel, transpose=transpose),
        grid=(n // tm, cols // tn),
        in_specs=[
            pl.BlockSpec((tm, d), lambda i, j: (i, 0)),
            pl.BlockSpec((d, tn), lambda i, j: (0, j)),
            pl.BlockSpec((1, HEAD_DIM), lambda i, j: (0, 0)),
            table_spec, table_spec, table_spec,
        ],
        out_specs=out_spec,
        out_shape=out_shape,
        scratch_shapes=[pltpu.VMEM((heads, tm, HEAD_DIM), F32)],
        compiler_params=_params("parallel", "arbitrary"),
        name=name,
    )(h, w, gain, *tables)


def _v_proj_kernel(h_ref, w_ref, o_ref):
    res = jnp.dot(h_ref[...], w_ref[...], preferred_element_type=F32)
    for a in range(res.shape[0] // ATT_T):
        o_ref[0, a] = res[a * ATT_T:(a + 1) * ATT_T, :].T.astype(BF16)


def _v_proj(h, w, *, batch, tm=1024, tn=512):
    n, d = h.shape
    cols = w.shape[1]
    s = n // batch
    tm = min(tm, s)
    per_b = s // tm
    return pl.pallas_call(
        _v_proj_kernel,
        grid=(n // tm, cols // tn),
        in_specs=[
            pl.BlockSpec((tm, d), lambda i, j: (i, 0)),
            pl.BlockSpec((d, tn), lambda i, j: (0, j)),
        ],
        out_specs=pl.BlockSpec((1, tm // ATT_T, tn, ATT_T),
                               lambda i, j: (i // per_b, i % per_b, j, 0)),
        out_shape=jax.ShapeDtypeStruct((batch, s // ATT_T, cols, ATT_T), BF16),
        compiler_params=_params("parallel", "arbitrary"),
        name="v_proj",
    )(h, w)


def _idx_kernel(h_ref, w_ref, kgain_ref, c_ref, s1_ref, s2_ref,
                qi_ref, kcat_ref, wi_ref, *, w_scale):
    half = IDX_ROT_DIM // 2
    qw = IDX_HEADS * IDX_DIM
    res = jnp.dot(h_ref[...], w_ref[...], preferred_element_type=F32)
    c, s1, s2 = c_ref[...], s1_ref[...], s2_ref[...]
    for t in range(qw // LANES):
        sl = slice(t * LANES, (t + 1) * LANES)
        q_t = _rope(res[:, sl], c, s1, s2, half).T
        qi_ref[0, 2 * t] = q_t[:IDX_DIM]
        qi_ref[0, 2 * t + 1] = q_t[IDX_DIM:]

    r = res[:, qw:qw + LANES]
    lane = lax.broadcasted_iota(I32, r.shape, 1)
    is_k = lane < IDX_DIM
    ms = jnp.sum(jnp.where(is_k, r * r, 0.0), axis=-1, keepdims=True) * (1.0 / IDX_DIM)
    kn = r * lax.rsqrt(ms + EPS) * kgain_ref[...]
    kr = _rope(kn, jnp.where(is_k, c, 1.0), jnp.where(is_k, s1, 0.0),
               jnp.where(is_k, s2, 0.0), half)
    hi = kr.astype(BF16).astype(F32)
    lo = kr - hi
    hi_lo = jnp.where(is_k, hi, pltpu.roll(lo, IDX_DIM, 1)).astype(BF16)
    kcat_ref[:, 0:LANES] = hi_lo
    kcat_ref[:, LANES:2 * LANES] = hi_lo
    is_w = (lane >= IDX_DIM) & (lane < IDX_DIM + IDX_HEADS)
    w_t = pltpu.roll(jnp.where(is_w, r * w_scale, 0.0), LANES - IDX_DIM, 1).T
    wi_ref[0] = w_t[:IDX_HEADS]


def _idx(h, w_idx, kgain, tables, *, batch, tm=512):
    n, d = h.shape
    cols = w_idx.shape[1]
    s = n // batch
    tm = min(tm, s)
    per_b = s // tm
    w_scale = (IDX_HEADS ** -0.5) * (IDX_DIM ** -0.5)
    return pl.pallas_call(
        functools.partial(_idx_kernel, w_scale=w_scale),
        grid=(n // tm,),
        in_specs=[
            pl.BlockSpec((tm, d), lambda i: (i, 0)),
            pl.BlockSpec((d, cols), lambda i: (0, 0)),
            pl.BlockSpec((1, LANES), lambda i: (0, 0)),
            pl.BlockSpec((tm, LANES), lambda i: (i, 0)),
            pl.BlockSpec((tm, LANES), lambda i: (i, 0)),
            pl.BlockSpec((tm, LANES), lambda i: (i, 0)),
        ],
        out_specs=[
            pl.BlockSpec((1, IDX_HEADS, IDX_DIM, tm),
                         lambda i: (i // per_b, 0, 0, i % per_b)),
            pl.BlockSpec((tm, 2 * LANES), lambda i: (i, 0)),
            pl.BlockSpec((1, IDX_HEADS, tm), lambda i: (i // per_b, 0, i % per_b)),
        ],
        out_shape=[
            jax.ShapeDtypeStruct((batch, IDX_HEADS, IDX_DIM, s), F32),
            jax.ShapeDtypeStruct((n, 2 * LANES), BF16),
            jax.ShapeDtypeStruct((batch, IDX_HEADS, s), F32),
        ],
        compiler_params=_params("parallel"),
        name="idx_proj",
    )(h, w_idx, kgain, *tables)


def _sgu_kernel(h_ref, w_ref, gv_ref, ws_ref, bs_ref, o_ref):
    tm = h_ref.shape[0]
    groups = tm // SGU_LEN
    res = jnp.dot(h_ref[...], w_ref[...], preferred_element_type=F32)
    u = jax.nn.gelu(res[:, :SGU_WIDTH])
    v = _rms(jax.nn.gelu(res[:, SGU_WIDTH:]), gv_ref[...]).astype(BF16)
    row = lax.broadcasted_iota(I32, (SGU_LEN, SGU_LEN), 0)
    col = lax.broadcasted_iota(I32, (SGU_LEN, SGU_LEN), 1)
    causal = (col // CHUNK) <= (row // CHUNK)
    for g in range(SGU_GROUPS):
        cs = slice(g * SGU_GROUP_DIM, (g + 1) * SGU_GROUP_DIM)
        wg = jnp.where(causal, ws_ref[g], 0.0).astype(BF16)
        vg = jnp.concatenate(
            [v[n * SGU_LEN:(n + 1) * SGU_LEN, cs] for n in range(groups)], axis=1)
        mixed = jnp.dot(wg, vg, preferred_element_type=F32) + bs_ref[g]
        for n in range(groups):
            rs = slice(n * SGU_LEN, (n + 1) * SGU_LEN)
            m = mixed[:, n * SGU_GROUP_DIM:(n + 1) * SGU_GROUP_DIM]
            o_ref[rs, cs] = (u[rs, cs] * m).astype(BF16)


def _sgu(h, w_sgu, gv, w_s, b_s, *, tm=512):
    n, d = h.shape
    tm = min(tm, n)
    return pl.pallas_call(
        _sgu_kernel,
        grid=(n // tm,),
        in_specs=[
            pl.BlockSpec((tm, d), lambda i: (i, 0)),
            pl.BlockSpec((d, 2 * SGU_WIDTH), lambda i: (0, 0)),
            pl.BlockSpec((1, SGU_WIDTH), lambda i: (0, 0)),
            pl.BlockSpec((SGU_GROUPS, SGU_LEN, SGU_LEN), lambda i: (0, 0, 0)),
            pl.BlockSpec((SGU_GROUPS, SGU_LEN, 1), lambda i: (0, 0, 0)),
        ],
        out_specs=pl.BlockSpec((tm, SGU_WIDTH), lambda i: (i, 0)),
        out_shape=jax.ShapeDtypeStruct((n, SGU_WIDTH), BF16),
        compiler_params=_params("parallel"),
        name="sgu_branch",
    )(h, w_sgu, gv, w_s, b_s)


def _gates_kernel(h_ref, w_ref, o_ref):
    res = jnp.dot(h_ref[...], w_ref[...], preferred_element_type=F32)
    o_ref[...] = jax.nn.sigmoid(res).astype(BF16)


def _gates(h, w_g, *, tm=1024, tn=1024):
    n, d = h.shape
    cols = w_g.shape[1]
    tm = min(tm, n)
    return pl.pallas_call(
        _gates_kernel,
        grid=(n // tm, cols // tn),
        in_specs=[
            pl.BlockSpec((tm, d), lambda i, j: (i, 0)),
            pl.BlockSpec((d, tn), lambda i, j: (0, j)),
        ],
        out_specs=pl.BlockSpec((tm, tn), lambda i, j: (i, j)),
        out_shape=jax.ShapeDtypeStruct((n, cols), BF16),
        compiler_params=_params("parallel", "arbitrary"),
        name="gates_proj",
    )(h, w_g)


def _select_kernel(qi_ref, wi_ref, kcat_ref, mask_ref, qcat_ref, s_ref, *, topk):
    t = SEL_T
    n_tiles = s_ref.shape[0]
    i = pl.program_id(1)
    n_live = i + 1

    for h in range(IDX_HEADS):
        q = qi_ref[0, h]
        hi = q.astype(BF16)
        lo = (q - hi.astype(F32)).astype(BF16)
        qcat_ref[h] = jnp.concatenate([hi, hi, lo, lo], axis=0)
    w = wi_ref[0]

    q_chunk = (i * t + lax.broadcasted_iota(I32, (t, t), 1)) // CHUNK
    k_row = lax.broadcasted_iota(I32, (t, t), 0)

    def score_tile(kt, mx, mn):
        kc = kcat_ref[0, pl.ds(pl.multiple_of(kt * t, t), t), :]
        acc = jnp.zeros((t, t), F32)
        for h in range(IDX_HEADS):
            d = jnp.dot(kc, qcat_ref[h], preferred_element_type=F32)
            acc = acc + jnp.maximum(d, 0.0) * w[h:h + 1, :]
        admissible = ((kt * t + k_row) // CHUNK) <= q_chunk
        s = jnp.where(admissible, acc, -jnp.inf)
        s_ref[kt] = s
        mx = jnp.maximum(mx, jnp.max(s, axis=0, keepdims=True))
        mn = jnp.minimum(mn, jnp.min(jnp.where(admissible, acc, jnp.inf),
                                     axis=0, keepdims=True))
        return mx, mn

    def score_body(j, carry):
        mx, mn = score_tile(2 * j, *carry)
        return score_tile(jnp.minimum(2 * j + 1, n_live - 1), mx, mn)

    mx, mn = lax.fori_loop(
        0, (n_live + 1) // 2, score_body,
        (jnp.full((1, t), -jnp.inf, F32), jnp.full((1, t), jnp.inf, F32)))

    n_adm = ((i * t + lax.broadcasted_iota(I32, (1, t), 1)) // CHUNK + 1) * CHUNK
    done0 = n_adm <= topk

    def count_ge(c):
        def add_tile(kt, weight, acc):
            ind = jnp.where(s_ref[kt] >= c, weight, 0.0)
            for r in range(t // COUNT_ROWS):
                acc = acc + ind[r * COUNT_ROWS:(r + 1) * COUNT_ROWS]
            return acc

        def count_body(j, acc):
            acc = add_tile(2 * j, 1.0, acc)
            second = 2 * j + 1
            return add_tile(jnp.minimum(second, n_live - 1),
                            jnp.where(second < n_live, 1.0, 0.0), acc)

        acc = lax.fori_loop(0, (n_live + 1) // 2, count_body,
                            jnp.zeros((COUNT_ROWS, t), F32))
        return jnp.sum(acc, axis=0, keepdims=True)

    def search_cond(st):
        return st[4] > 0.0

    def search_body(st):
        lo, hi, thr, live, _ = st
        mid = 0.5 * lo + 0.5 * hi
        stuck = (mid <= lo) | (mid >= hi)
        cnt = count_ge(mid)
        finish = (live > 0.0) & (stuck | (cnt == topk))
        thr = jnp.where(finish, jnp.where(stuck, lo, mid), thr)
        live = jnp.where(finish, 0.0, live)
        lo = jnp.where(cnt > topk, mid, lo)
        hi = jnp.where(cnt < topk, mid, hi)
        return lo, hi, thr, live, jnp.sum(live)

    hi0 = jnp.minimum(mx + jnp.maximum(jnp.abs(mx), ABOVE_MAX_FLOOR) * ABOVE_MAX_REL,
                      jnp.finfo(F32).max)
    live0 = jnp.where(done0, 0.0, 1.0)
    st0 = (mn, hi0, jnp.full((1, t), -jnp.inf, F32), live0, jnp.sum(live0))
    thr = lax.while_loop(search_cond, search_body, st0)[2]

    def mask_body(kt, carry):
        s = s_ref[kt]
        sel = (s >= thr) & (s > -jnp.inf)
        mask_ref[0, 0, kt] = jnp.where(sel, 1, 0).astype(jnp.int8)
        return carry

    lax.fori_loop(0, n_live, mask_body, 0)

    def zero_body(kt, carry):
        mask_ref[0, 0, kt] = jnp.zeros((t, t), jnp.int8)
        return carry

    lax.fori_loop(n_live, n_tiles, zero_body, 0)


def _select(qi_t, wi_t, kcat, *, topk):
    b, heads, dh, s = qi_t.shape
    t = SEL_T
    nq = s // t
    return pl.pallas_call(
        functools.partial(_select_kernel, topk=topk),
        grid=(b, nq),
        in_specs=[
            pl.BlockSpec((1, heads, dh, t), lambda bi, i: (bi, 0, 0, i)),
            pl.BlockSpec((1, heads, t), lambda bi, i: (bi, 0, i)),
            pl.BlockSpec((1, s, 4 * IDX_DIM), lambda bi, i: (bi, 0, 0)),
        ],
        out_specs=pl.BlockSpec((1, 1, nq, t, t), lambda bi, i: (bi, i, 0, 0, 0)),
        out_shape=jax.ShapeDtypeStruct((b, nq, nq, t, t), jnp.int8),
        scratch_shapes=[
            pltpu.VMEM((heads, 4 * IDX_DIM, t), BF16),
            pltpu.VMEM((nq, t, t), F32),
        ],
        compiler_params=_params("parallel", "arbitrary"),
        name="index_select",
    )(qi_t, wi_t, kcat)


def _attn_kernel(qt_ref, k_ref, vt_ref, mask_ref, o_ref,
                 m_ref, acc_ref, s_ref):
    t = ATT_T
    i = pl.program_id(2)
    n = i + 1
    m_ref[...] = jnp.full(m_ref.shape, -jnp.inf, F32)
    acc_ref[...] = jnp.zeros(acc_ref.shape, F32)
    q_t = qt_ref[0]

    def logits(kt, slot):
        rows = pl.ds(pl.multiple_of(kt * t, t), t)
        for hh in range(ATT_HEADS):
            hs = slice(hh * HEAD_DIM, (hh + 1) * HEAD_DIM)
            s_ref[slot, hh] = jnp.dot(k_ref[0, hh, rows, :], q_t[hs, :],
                                      preferred_element_type=F32)

    def update(kt, slot, off):
        v_t = vt_ref[0, kt]
        sel = mask_ref[0, 0, kt].astype(I32) > off
        for hh in range(ATT_HEADS):
            hs = slice(hh * HEAD_DIM, (hh + 1) * HEAD_DIM)
            s = jnp.where(sel, s_ref[slot, hh], MASKED_LOGIT)
            m_prev = m_ref[hh]
            m_new = jnp.maximum(m_prev, jnp.max(s, axis=0, keepdims=True))
            p = jnp.exp2(s - m_new)
            alpha = jnp.exp2(m_prev - m_new)
            m_ref[hh] = m_new
            lhs = jnp.concatenate([v_t[hs, :], jnp.ones((DENOM_ROWS, t), BF16)], axis=0)
            pv = jnp.dot(lhs, p.astype(BF16), preferred_element_type=F32)
            acc_ref[hh] = alpha * acc_ref[hh] + pv

    logits(0, 0)

    def body(j, carry):
        first = 2 * j
        second = jnp.minimum(first + 1, n - 1)
        logits(second, 1)
        update(first, 0, 0)
        logits(jnp.minimum(first + 2, n - 1), 0)
        update(second, 1, jnp.where(first + 1 < n, 0, 1))
        return carry

    lax.fori_loop(0, (n + 1) // 2, body, 0)
    for hh in range(ATT_HEADS):
        acc = acc_ref[hh]
        o_ref[0, hh * HEAD_DIM:(hh + 1) * HEAD_DIM, :] = (
            acc[:HEAD_DIM] / acc[HEAD_DIM:HEAD_DIM + 1]).astype(BF16)


def _attn(q_t, k, v_t, mask):
    b, _, s, _ = k.shape
    t = ATT_T
    nq = s // t
    width = ATT_HEADS * HEAD_DIM
    return pl.pallas_call(
        _attn_kernel,
        grid=(b, N_HEADS // ATT_HEADS, nq),
        in_specs=[
            pl.BlockSpec((1, width, t), lambda bi, hp, i: (bi, hp, i)),
            pl.BlockSpec((1, ATT_HEADS, s, HEAD_DIM), lambda bi, hp, i: (bi, hp, 0, 0)),
            pl.BlockSpec((1, nq, width, t), lambda bi, hp, i: (bi, 0, hp, 0)),
            pl.BlockSpec((1, 1, nq, t, t), lambda bi, hp, i: (bi, i, 0, 0, 0)),
        ],
        out_specs=pl.BlockSpec((1, width, t), lambda bi, hp, i: (bi, hp, i)),
        out_shape=jax.ShapeDtypeStruct((b, ATT_WIDTH, s), BF16),
        scratch_shapes=[
            pltpu.VMEM((ATT_HEADS, 1, t), F32),
            pltpu.VMEM((ATT_HEADS, HEAD_DIM + DENOM_ROWS, t), F32),
            pltpu.VMEM((2, ATT_HEADS, t, t), F32),
        ],
        compiler_params=_params("parallel", "parallel", "arbitrary"),
        name="masked_attention",
    )(q_t, k, v_t, mask)


def _merge_kernel(ya_ref, yb_ref, g_ref, x_ref, wa_ref, wb_ref, wo_ref, o_ref):
    d = x_ref.shape[1]
    a = lax.dot_general(ya_ref[0], wa_ref[...], (((0,), (0,)), ((), ())),
                        preferred_element_type=F32)
    b = jnp.dot(yb_ref[...], wb_ref[...], preferred_element_type=F32)
    merged = g_ref[:, :d].astype(F32) * a + g_ref[:, d:].astype(F32) * b
    o_ref[...] = x_ref[...] + jnp.dot(merged.astype(BF16), wo_ref[...],
                                      preferred_element_type=F32)


def _merge(ya_t, yb, gates, x, wa, wb, wo, *, tm=256):
    n, d = x.shape
    batch, width, s = ya_t.shape
    tm = min(tm, s)
    per_b = s // tm
    return pl.pallas_call(
        _merge_kernel,
        grid=(n // tm,),
        in_specs=[
            pl.BlockSpec((1, width, tm), lambda i: (i // per_b, 0, i % per_b)),
            pl.BlockSpec((tm, yb.shape[1]), lambda i: (i, 0)),
            pl.BlockSpec((tm, 2 * d), lambda i: (i, 0)),
            pl.BlockSpec((tm, d), lambda i: (i, 0)),
            pl.BlockSpec(wa.shape, lambda i: (0, 0)),
            pl.BlockSpec(wb.shape, lambda i: (0, 0)),
            pl.BlockSpec(wo.shape, lambda i: (0, 0)),
        ],
        out_specs=pl.BlockSpec((tm, d), lambda i: (i, 0)),
        out_shape=jax.ShapeDtypeStruct((n, d), F32),
        compiler_params=_params("parallel"),
        name="merge_out",
    )(ya_t, yb, gates, x, wa, wb, wo)


def _inv_freq_lanes(rot_dim, period, live):
    inv = ROPE_THETA ** (-jnp.arange(0, rot_dim, 2, dtype=F32) / rot_dim)
    lane = jnp.arange(LANES)
    j = lane % period
    vals = inv[j % (rot_dim // 2)]
    return jnp.where((j < rot_dim) & (lane < live), vals, 0.0).astype(F32)[None, :]


def _layer(x, tables_att, tables_idx, p):
    b, s, d = x.shape
    n = b * s
    topk = min(TOPK_MAX, s // 4)
    x = x.reshape(n, d)

    x1, h = _ffn(x, p["norm_ffn1"][None], p["ffn1_w_gate"].astype(BF16),
                 p["ffn1_w_up"].astype(BF16), p["ffn1_w_down"].astype(BF16),
                 p["norm_mix"][None], emit_norm=True)

    w_in = p["w_in"]
    c_qkv = 3 * ATT_WIDTH
    c_qi = c_qkv + IDX_HEADS * IDX_DIM
    c_idx = c_qi + IDX_DIM + IDX_HEADS
    c_sgu = c_idx + 2 * SGU_WIDTH
    w_q = w_in[:, :ATT_WIDTH].astype(BF16)
    w_k = w_in[:, ATT_WIDTH:2 * ATT_WIDTH].astype(BF16)
    w_v = w_in[:, 2 * ATT_WIDTH:c_qkv].astype(BF16)
    w_idx = jnp.pad(w_in[:, c_qkv:c_idx],
                    ((0, 0), (0, LANES - IDX_DIM - IDX_HEADS))).astype(BF16)
    w_sgu = w_in[:, c_idx:c_sgu].astype(BF16)
    w_gate = w_in[:, c_sgu:].astype(BF16)

    q_gain = (p["q_norm"] * (HEAD_DIM ** -0.5 * LOG2_E))[None]
    q_t = _head_proj(h, w_q, q_gain, tables_att, batch=b, transpose=True, name="q_proj")
    k = _head_proj(h, w_k, p["k_norm"][None], tables_att, batch=b, transpose=False,
                   name="k_proj")
    v_t = _v_proj(h, w_v, batch=b)
    kgain = jnp.pad(p["idx_k_norm"], (0, LANES - IDX_DIM))[None]
    qi_t, kcat, wi_t = _idx(h, w_idx, kgain, tables_idx, batch=b)
    yb = _sgu(h, w_sgu, p["sgu_v_norm"][None], p["sgu_w_s"], p["sgu_b_s"][:, :, None])
    gates = _gates(h, w_gate)

    mask = _select(qi_t, wi_t, kcat.reshape(b, s, 4 * IDX_DIM), topk=topk)
    ya_t = _attn(q_t, k, v_t, mask)

    x2 = _merge(ya_t, yb, gates, x1, p["w_up_attn"].astype(BF16),
                p["w_up_sgu"].astype(BF16), p["w_out"].astype(BF16))
    out = _ffn(x2, p["norm_ffn2"][None], p["ffn2_w_gate"].astype(BF16),
               p["ffn2_w_up"].astype(BF16), p["ffn2_w_down"].astype(BF16),
               p["norm_ffn2"][None], emit_norm=False)
    return out.reshape(b, s, d)


def kernel(x, positions, norm_ffn1, ffn1_w_gate, ffn1_w_up, ffn1_w_down, norm_mix, w_in,
           q_norm, k_norm, idx_k_norm, sgu_v_norm, sgu_w_s, sgu_b_s, w_up_attn, w_up_sgu,
           w_out, norm_ffn2, ffn2_w_gate, ffn2_w_up, ffn2_w_down):
    params = dict(
        norm_ffn1=norm_ffn1, ffn1_w_gate=ffn1_w_gate, ffn1_w_up=ffn1_w_up,
        ffn1_w_down=ffn1_w_down, norm_mix=norm_mix, w_in=w_in, q_norm=q_norm,
        k_norm=k_norm, idx_k_norm=idx_k_norm, sgu_v_norm=sgu_v_norm, sgu_w_s=sgu_w_s,
        sgu_b_s=sgu_b_s, w_up_attn=w_up_attn, w_up_sgu=w_up_sgu, w_out=w_out,
        norm_ffn2=norm_ffn2, ffn2_w_gate=ffn2_w_gate, ffn2_w_up=ffn2_w_up,
        ffn2_w_down=ffn2_w_down)
    pos = positions.reshape(-1, 1).astype(I32)
    invf_att = _inv_freq_lanes(ROT_DIM, LANES, LANES)
    invf_idx = _inv_freq_lanes(IDX_ROT_DIM, IDX_DIM, LANES)
    tables_att = _rope_table(pos, invf_att, half=ROT_DIM // 2, period=LANES,
                             name="rope_table_att")
    tables_idx = _rope_table(pos, invf_idx, half=IDX_ROT_DIM // 2, period=IDX_DIM,
                             name="rope_table_idx")
    for l in range(w_in.shape[0]):
        x = _layer(x, tables_att, tables_idx, {k: v[l] for k, v in params.items()})
    return x
```

```python
import functools

import jax
import jax.numpy as jnp
from jax import lax
from jax.experimental import pallas as pl
from jax.experimental.pallas import tpu as pltpu

F32 = jnp.float32
BF16 = jnp.bfloat16
I32 = jnp.int32

EPS = 1e-6
CHUNK = 64
N_HEADS = 8
HEAD_DIM = 128
ATT_WIDTH = N_HEADS * HEAD_DIM
ROPE_THETA = 500000.0
ROT_DIM = HEAD_DIM // 4
IDX_HEADS = 16
IDX_DIM = 64
IDX_ROT_DIM = IDX_DIM // 4
TOPK_MAX = 256
SGU_LEN = 128
SGU_GROUPS = 8
SGU_GROUP_DIM = 128
SGU_WIDTH = SGU_GROUPS * SGU_GROUP_DIM

LANES = 128
MXU_DIM = 256
VMEM_LIMIT_BYTES = 56 * 1024 * 1024

SEL_T = MXU_DIM
MASK_ROWS = 128
COUNT_ROWS = 64
ATT_T = 256
ATT_HEADS = 4
DENOM_ROWS = 16
ABOVE_MAX_REL = 1e-6
ABOVE_MAX_FLOOR = 1e-30
MASKED_LOGIT = -1e30
LOG2_E = 1.4426950408889634


def _params(*sem):
    return pltpu.CompilerParams(dimension_semantics=sem,
                                vmem_limit_bytes=VMEM_LIMIT_BYTES)


def _rms(t, gain):
    ms = jnp.mean(t * t, axis=-1, keepdims=True)
    return t * lax.rsqrt(ms + EPS) * gain


def _rope_tables(pos_ref, invf_ref, half, period):
    ang = pos_ref[...].astype(F32) * invf_ref[...]
    c = jnp.cos(ang)
    s = jnp.sin(ang)
    lane = lax.broadcasted_iota(I32, ang.shape, 1) & (period - 1)
    return c, jnp.where(lane < half, -s, 0.0), jnp.where(lane >= half, s, 0.0)


def _rope(t, c, s1, s2, half):
    return (t * c + pltpu.roll(t, LANES - half, 1) * s1
            + pltpu.roll(t, half, 1) * s2)


def _ffn_kernel(x_ref, g_ref, wg_ref, wu_ref, wd_ref, g2_ref, o_ref, *rest,
                emit_norm):
    if emit_norm:
        h_out_ref, hn_ref = rest
    else:
        (hn_ref,) = rest
    j = pl.program_id(1)

    @pl.when(j == 0)
    def _():
        hn_ref[...] = _rms(x_ref[...], g_ref[...]).astype(BF16)
        o_ref[...] = jnp.zeros_like(o_ref)

    h = hn_ref[...]
    a = jnp.dot(h, wg_ref[...], preferred_element_type=F32)
    b = jnp.dot(h, wu_ref[...], preferred_element_type=F32)
    act = (a * jax.nn.sigmoid(a) * b).astype(BF16)
    o_ref[...] += jnp.dot(act, wd_ref[...], preferred_element_type=F32)

    @pl.when(j == pl.num_programs(1) - 1)
    def _():
        y = x_ref[...] + 0.5 * o_ref[...]
        o_ref[...] = y
        if emit_norm:
            h_out_ref[...] = _rms(y, g2_ref[...]).astype(BF16)


def _ffn(x, gain, w_gate, w_up, w_down, gain2, *, emit_norm, tm=512, tf=512):
    n, d = x.shape
    f = w_gate.shape[1]
    tm = min(tm, n)
    out_shape = [jax.ShapeDtypeStruct((n, d), F32)]
    out_specs = [pl.BlockSpec((tm, d), lambda i, j: (i, 0))]
    if emit_norm:
        out_shape.append(jax.ShapeDtypeStruct((n, d), BF16))
        out_specs.append(pl.BlockSpec((tm, d), lambda i, j: (i, 0)))
    res = pl.pallas_call(
        functools.partial(_ffn_kernel, emit_norm=emit_norm),
        grid=(n // tm, f // tf),
        in_specs=[
            pl.BlockSpec((tm, d), lambda i, j: (i, 0)),
            pl.BlockSpec((1, d), lambda i, j: (0, 0)),
            pl.BlockSpec((d, tf), lambda i, j: (0, j)),
            pl.BlockSpec((d, tf), lambda i, j: (0, j)),
            pl.BlockSpec((tf, d), lambda i, j: (j, 0)),
            pl.BlockSpec((1, d), lambda i, j: (0, 0)),
        ],
        out_specs=out_specs,
        out_shape=out_shape,
        scratch_shapes=[pltpu.VMEM((tm, d), BF16)],
        compiler_params=_params("parallel", "arbitrary"),
        name="ffn_norm" if emit_norm else "ffn",
    )(x, gain, w_gate, w_up, w_down, gain2)
    return res if emit_norm else res[0]


def _rope_table_kernel(pos_ref, invf_ref, c_ref, s1_ref, s2_ref, *, half, period):
    c, s1, s2 = _rope_tables(pos_ref, invf_ref, half, period)
    c_ref[...] = c
    s1_ref[...] = s1
    s2_ref[...] = s2


def _rope_table(pos, invf, *, half, period, name, tm=2048):
    n = pos.shape[0]
    tm = min(tm, n)
    spec = pl.BlockSpec((tm, LANES), lambda i: (i, 0))
    return pl.pallas_call(
        functools.partial(_rope_table_kernel, half=half, period=period),
        grid=(n // tm,),
        in_specs=[pl.BlockSpec((tm, 1), lambda i: (i, 0)),
                  pl.BlockSpec((1, LANES), lambda i: (0, 0))],
        out_specs=[spec] * 3,
        out_shape=[jax.ShapeDtypeStruct((n, LANES), F32)] * 3,
        compiler_params=_params("parallel"),
        name=name,
    )(pos, invf)


def _head_proj_kernel(h_ref, w_ref, gain_ref, c_ref, s1_ref, s2_ref, o_ref, r_ref,
                      *, transpose):
    half = ROT_DIM // 2
    res = jnp.dot(h_ref[...], w_ref[...], preferred_element_type=F32)
    heads = res.shape[1] // HEAD_DIM
    for hh in range(heads):
        r_ref[hh] = res[:, hh * HEAD_DIM:(hh + 1) * HEAD_DIM]

    def head(hh, carry):
        t = _rope(_rms(r_ref[hh], gain_ref[...]), c_ref[...], s1_ref[...], s2_ref[...], half)
        if transpose:
            o_ref[0, pl.ds(pl.multiple_of(hh * HEAD_DIM, HEAD_DIM), HEAD_DIM), :] = (
                t.T.astype(BF16))
        else:
            o_ref[0, hh] = t.astype(BF16)
        return carry

    lax.fori_loop(0, heads, head, 0)


def _head_proj(h, w, gain, tables, *, batch, transpose, name, tm=1024, tn=512):
    n, d = h.shape
    cols = w.shape[1]
    s = n // batch
    tm = min(tm, s)
    per_b = s // tm
    heads = tn // HEAD_DIM
    table_spec = pl.BlockSpec((tm, LANES), lambda i, j: (i, 0))
    if transpose:
        out_spec = pl.BlockSpec((1, tn, tm), lambda i, j: (i // per_b, j, i % per_b))
        out_shape = jax.ShapeDtypeStruct((batch, cols, s), BF16)
    else:
        out_spec = pl.BlockSpec((1, heads, tm, HEAD_DIM),
                                lambda i, j: (i // per_b, j, i % per_b, 0))
        out_shape = jax.ShapeDtypeStruct((batch, cols // HEAD_DIM, s, HEAD_DIM), BF16)
    return pl.pallas_call(
        functools.partial(_head_proj_kernel, transpose=transpose),
        grid=(n // tm, cols // tn),
        in_specs=[
            pl.BlockSpec((tm, d), lambda i, j: (i, 0)),
            pl.BlockSpec((d, tn), lambda i, j: (0, j)),
            pl.BlockSpec((1, HEAD_DIM), lambda i, j: (0, 0)),
            table_spec, table_spec, table_spec,
        ],
        out_specs=out_spec,
        out_shape=out_shape,
        scratch_shapes=[pltpu.VMEM((heads, tm, HEAD_DIM), F32)],
        compiler_params=_params("parallel", "arbitrary"),
        name=name,
    )(h, w, gain, *tables)


def _v_proj_kernel(h_ref, w_ref, o_ref):
    res = jnp.dot(h_ref[...], w_ref[...], preferred_element_type=F32)
    for a in range(res.shape[0] // ATT_T):
        o_ref[0, a] = res[a * ATT_T:(a + 1) * ATT_T, :].T.astype(BF16)


def _v_proj(h, w, *, batch, tm=1024, tn=512):
    n, d = h.shape
    cols = w.shape[1]
    s = n // batch
    tm = min(tm, s)
    per_b = s // tm
    return pl.pallas_call(
        _v_proj_kernel,
        grid=(n // tm, cols // tn),
        in_specs=[
            pl.BlockSpec((tm, d), lambda i, j: (i, 0)),
            pl.BlockSpec((d, tn), lambda i, j: (0, j)),
        ],
        out_specs=pl.BlockSpec((1, tm // ATT_T, tn, ATT_T),
                               lambda i, j: (i // per_b, i % per_b, j, 0)),
        out_shape=jax.ShapeDtypeStruct((batch, s // ATT_T, cols, ATT_T), BF16),
        compiler_params=_params("parallel", "arbitrary"),
        name="v_proj",
    )(h, w)


def _idx_kernel(h_ref, w_ref, kgain_ref, c_ref, s1_ref, s2_ref,
                qi_ref, kcat_ref, wi_ref, *, w_scale):
    half = IDX_ROT_DIM // 2
    qw = IDX_HEADS * IDX_DIM
    res = jnp.dot(h_ref[...], w_ref[...], preferred_element_type=F32)
    c, s1, s2 = c_ref[...], s1_ref[...], s2_ref[...]
    for t in range(qw // LANES):
        sl = slice(t * LANES, (t + 1) * LANES)
        q_t = _rope(res[:, sl], c, s1, s2, half).T
        qi_ref[0, 2 * t] = q_t[:IDX_DIM]
        qi_ref[0, 2 * t + 1] = q_t[IDX_DIM:]

    r = res[:, qw:qw + LANES]
    lane = lax.broadcasted_iota(I32, r.shape, 1)
    is_k = lane < IDX_DIM
    ms = jnp.sum(jnp.where(is_k, r * r, 0.0), axis=-1, keepdims=True) * (1.0 / IDX_DIM)
    kn = r * lax.rsqrt(ms + EPS) * kgain_ref[...]
    kr = _rope(kn, jnp.where(is_k, c, 1.0), jnp.where(is_k, s1, 0.0),
               jnp.where(is_k, s2, 0.0), half)
    hi = kr.astype(BF16).astype(F32)
    lo = kr - hi
    hi_lo = jnp.where(is_k, hi, pltpu.roll(lo, IDX_DIM, 1)).astype(BF16)
    kcat_ref[:, 0:LANES] = hi_lo
    kcat_ref[:, LANES:2 * LANES] = hi_lo
    is_w = (lane >= IDX_DIM) & (lane < IDX_DIM + IDX_HEADS)
    w_t = pltpu.roll(jnp.where(is_w, r * w_scale, 0.0), LANES - IDX_DIM, 1).T
    wi_ref[0] = w_t[:IDX_HEADS]


def _idx(h, w_idx, kgain, tables, *, batch, tm=512):
    n, d = h.shape
    cols = w_idx.shape[1]
    s = n // batch
    tm = min(tm, s)
    per_b = s // tm
    w_scale = (IDX_HEADS ** -0.5) * (IDX_DIM ** -0.5)
    return pl.pallas_call(
        functools.partial(_idx_kernel, w_scale=w_scale),
        grid=(n // tm,),
        in_specs=[
            pl.BlockSpec((tm, d), lambda i: (i, 0)),
            pl.BlockSpec((d, cols), lambda i: (0, 0)),
            pl.BlockSpec((1, LANES), lambda i: (0, 0)),
            pl.BlockSpec((tm, LANES), lambda i: (i, 0)),
            pl.BlockSpec((tm, LANES), lambda i: (i, 0)),
            pl.BlockSpec((tm, LANES), lambda i: (i, 0)),
        ],
        out_specs=[
            pl.BlockSpec((1, IDX_HEADS, IDX_DIM, tm),
                         lambda i: (i // per_b, 0, 0, i % per_b)),
            pl.BlockSpec((tm, 2 * LANES), lambda i: (i, 0)),
            pl.BlockSpec((1, IDX_HEADS, tm), lambda i: (i // per_b, 0, i % per_b)),
        ],
        out_shape=[
            jax.ShapeDtypeStruct((batch, IDX_HEADS, IDX_DIM, s), F32),
            jax.ShapeDtypeStruct((n, 2 * LANES), BF16),
            jax.ShapeDtypeStruct((batch, IDX_HEADS, s), F32),
        ],
        compiler_params=_params("parallel"),
        name="idx_proj",
    )(h, w_idx, kgain, *tables)


def _sgu_kernel(h_ref, w_ref, gv_ref, ws_ref, bs_ref, o_ref):
    tm = h_ref.shape[0]
    groups = tm // SGU_LEN
    res = jnp.dot(h_ref[...], w_ref[...], preferred_element_type=F32)
    u = jax.nn.gelu(res[:, :SGU_WIDTH])
    v = _rms(jax.nn.gelu(res[:, SGU_WIDTH:]), gv_ref[...]).astype(BF16)
    row = lax.broadcasted_iota(I32, (SGU_LEN, SGU_LEN), 0)
    col = lax.broadcasted_iota(I32, (SGU_LEN, SGU_LEN), 1)
    causal = (col // CHUNK) <= (row // CHUNK)
    for g in range(SGU_GROUPS):
        cs = slice(g * SGU_GROUP_DIM, (g + 1) * SGU_GROUP_DIM)
        wg = jnp.where(causal, ws_ref[g], 0.0).astype(BF16)
        vg = jnp.concatenate(
            [v[n * SGU_LEN:(n + 1) * SGU_LEN, cs] for n in range(groups)], axis=1)
        mixed = jnp.dot(wg, vg, preferred_element_type=F32) + bs_ref[g]
        for n in range(groups):
            rs = slice(n * SGU_LEN, (n + 1) * SGU_LEN)
            m = mixed[:, n * SGU_GROUP_DIM:(n + 1) * SGU_GROUP_DIM]
            o_ref[rs, cs] = (u[rs, cs] * m).astype(BF16)


def _sgu(h, w_sgu, gv, w_s, b_s, *, tm=512):
    n, d = h.shape
    tm = min(tm, n)
    return pl.pallas_call(
        _sgu_kernel,
        grid=(n // tm,),
        in_specs=[
            pl.BlockSpec((tm, d), lambda i: (i, 0)),
            pl.BlockSpec((d, 2 * SGU_WIDTH), lambda i: (0, 0)),
            pl.BlockSpec((1, SGU_WIDTH), lambda i: (0, 0)),
            pl.BlockSpec((SGU_GROUPS, SGU_LEN, SGU_LEN), lambda i: (0, 0, 0)),
            pl.BlockSpec((SGU_GROUPS, SGU_LEN, 1), lambda i: (0, 0, 0)),
        ],
        out_specs=pl.BlockSpec((tm, SGU_WIDTH), lambda i: (i, 0)),
        out_shape=jax.ShapeDtypeStruct((n, SGU_WIDTH), BF16),
        compiler_params=_params("parallel"),
        name="sgu_branch",
    )(h, w_sgu, gv, w_s, b_s)


def _gates_kernel(h_ref, w_ref, o_ref):
    res = jnp.dot(h_ref[...], w_ref[...], preferred_element_type=F32)
    o_ref[...] = jax.nn.sigmoid(res).astype(BF16)


def _gates(h, w_g, *, tm=1024, tn=1024):
    n, d = h.shape
    cols = w_g.shape[1]
    tm = min(tm, n)
    return pl.pallas_call(
        _gates_kernel,
        grid=(n // tm, cols // tn),
        in_specs=[
            pl.BlockSpec((tm, d), lambda i, j: (i, 0)),
            pl.BlockSpec((d, tn), lambda i, j: (0, j)),
        ],
        out_specs=pl.BlockSpec((tm, tn), lambda i, j: (i, j)),
        out_shape=jax.ShapeDtypeStruct((n, cols), BF16),
        compiler_params=_params("parallel", "arbitrary"),
        name="gates_proj",
    )(h, w_g)


def _select_kernel(qi_ref, wi_ref, kcat_ref, mask_ref, qcat_ref, s_ref, *, topk):
    t = SEL_T
    n_tiles = s_ref.shape[0]
    i = pl.program_id(1)
    n_live = i + 1

    for h in range(IDX_HEADS):
        q = qi_ref[0, h]
        hi = q.astype(BF16)
        lo = (q - hi.astype(F32)).astype(BF16)
        qcat_ref[h] = jnp.concatenate([hi, hi, lo, lo], axis=0)
    w = wi_ref[0]

    q_chunk = (i * t + lax.broadcasted_iota(I32, (t, t), 1)) // CHUNK
    k_row = lax.broadcasted_iota(I32, (t, t), 0)

    def score_tile(kt, mx, mn):
        kc = kcat_ref[0, pl.ds(pl.multiple_of(kt * t, t), t), :]
        acc = jnp.zeros((t, t), F32)
        for h in range(IDX_HEADS):
            d = jnp.dot(kc, qcat_ref[h], preferred_element_type=F32)
            acc = acc + jnp.maximum(d, 0.0) * w[h:h + 1, :]
        admissible = ((kt * t + k_row) // CHUNK) <= q_chunk
        s = jnp.where(admissible, acc, -jnp.inf)
        s_ref[kt] = s
        mx = jnp.maximum(mx, jnp.max(s, axis=0, keepdims=True))
        mn = jnp.minimum(mn, jnp.min(jnp.where(admissible, acc, jnp.inf),
                                     axis=0, keepdims=True))
        return mx, mn

    def score_body(j, carry):
        mx, mn = score_tile(2 * j, *carry)
        return score_tile(jnp.minimum(2 * j + 1, n_live - 1), mx, mn)

    mx, mn = lax.fori_loop(
        0, (n_live + 1) // 2, score_body,
        (jnp.full((1, t), -jnp.inf, F32), jnp.full((1, t), jnp.inf, F32)))

    n_adm = ((i * t + lax.broadcasted_iota(I32, (1, t), 1)) // CHUNK + 1) * CHUNK
    done0 = n_adm <= topk

    def count(c, strict=False):
        def add_tile(kt, weight, acc):
            s = s_ref[kt]
            ind = jnp.where(s > c if strict else s >= c, weight, 0.0)
            for r in range(t // COUNT_ROWS):
                acc = acc + ind[r * COUNT_ROWS:(r + 1) * COUNT_ROWS]
            return acc

        def count_body(j, acc):
            acc = add_tile(2 * j, 1.0, acc)
            second = 2 * j + 1
            return add_tile(jnp.minimum(second, n_live - 1),
                            jnp.where(second < n_live, 1.0, 0.0), acc)

        acc = lax.fori_loop(0, (n_live + 1) // 2, count_body,
                            jnp.zeros((COUNT_ROWS, t), F32))
        return jnp.sum(acc, axis=0, keepdims=True)

    def search_cond(st):
        return st[6] > 0.0

    def search_body(st):
        lo, hi, c_lo, thr, n_thr, live, _ = st
        mid = 0.5 * lo + 0.5 * hi
        stuck = (mid <= lo) | (mid >= hi)
        cnt = count(mid)
        finish = (live > 0.0) & (stuck | (cnt == topk))
        thr = jnp.where(finish, jnp.where(stuck, lo, mid), thr)
        n_thr = jnp.where(finish, jnp.where(stuck, c_lo, cnt), n_thr)
        live = jnp.where(finish, 0.0, live)
        above = cnt > topk
        lo = jnp.where(above, mid, lo)
        c_lo = jnp.where(above, cnt, c_lo)
        hi = jnp.where(cnt < topk, mid, hi)
        return lo, hi, c_lo, thr, n_thr, live, jnp.sum(live)

    hi0 = jnp.minimum(mx + jnp.maximum(jnp.abs(mx), ABOVE_MAX_FLOOR) * ABOVE_MAX_REL,
                      jnp.finfo(F32).max)
    live0 = jnp.where(done0, 0.0, 1.0)
    st0 = (mn, hi0, n_adm.astype(F32), jnp.full((1, t), -jnp.inf, F32),
           jnp.zeros((1, t), F32), live0, jnp.sum(live0))
    st = lax.while_loop(search_cond, search_body, st0)
    thr, n_thr = st[3], st[4]
    tied = n_thr > topk
    any_tied = jnp.sum(jnp.where(tied, 1.0, 0.0)) > 0.0

    @pl.when(jnp.logical_not(any_tied))
    def _():
        def mask_body(kt, carry):
            s = s_ref[kt]
            sel = (s >= thr) & (s > -jnp.inf)
            mask_ref[0, 0, kt] = jnp.where(sel, 1, 0).astype(jnp.int8)
            return carry

        lax.fori_loop(0, n_live, mask_body, 0)

    @pl.when(any_tied)
    def _():
        room = topk - count(thr, strict=True)
        tri = jnp.where(lax.broadcasted_iota(I32, (t, t), 0)
                        >= lax.broadcasted_iota(I32, (t, t), 1), 1.0, 0.0).astype(BF16)

        def tie_body(kt, seen):
            s = s_ref[kt]
            eq = (s == thr) & tied
            eq_f = jnp.where(eq, 1.0, 0.0)
            upto = jnp.dot(tri, eq_f.astype(BF16), preferred_element_type=F32)
            rank = seen + upto - eq_f
            keep = (eq & (rank < room)) | (jnp.logical_not(eq) & (s >= thr))
            sel = keep & (s > -jnp.inf)
            mask_ref[0, 0, kt] = jnp.where(sel, 1, 0).astype(jnp.int8)
            return seen + upto[t - 1:t, :]

        lax.fori_loop(0, n_live, tie_body, jnp.zeros((1, t), F32))

    def zero_body(kt, carry):
        mask_ref[0, 0, kt] = jnp.zeros((t, t), jnp.int8)
        return carry

    lax.fori_loop(n_live, n_tiles, zero_body, 0)


def _select(qi_t, wi_t, kcat, *, topk):
    b, heads, dh, s = qi_t.shape
    t = SEL_T
    nq = s // t
    return pl.pallas_call(
        functools.partial(_select_kernel, topk=topk),
        grid=(b, nq),
        in_specs=[
            pl.BlockSpec((1, heads, dh, t), lambda bi, i: (bi, 0, 0, i)),
            pl.BlockSpec((1, heads, t), lambda bi, i: (bi, 0, i)),
            pl.BlockSpec((1, s, 4 * IDX_DIM), lambda bi, i: (bi, 0, 0)),
        ],
        out_specs=pl.BlockSpec((1, 1, nq, t, t), lambda bi, i: (bi, i, 0, 0, 0)),
        out_shape=jax.ShapeDtypeStruct((b, nq, nq, t, t), jnp.int8),
        scratch_shapes=[
            pltpu.VMEM((heads, 4 * IDX_DIM, t), BF16),
            pltpu.VMEM((nq, t, t), F32),
        ],
        compiler_params=_params("parallel", "arbitrary"),
        name="index_select",
    )(qi_t, wi_t, kcat)


def _attn_kernel(qt_ref, k_ref, vt_ref, mask_ref, o_ref,
                 m_ref, acc_ref, s_ref):
    t = ATT_T
    i = pl.program_id(2)
    n = i + 1
    m_ref[...] = jnp.full(m_ref.shape, -jnp.inf, F32)
    acc_ref[...] = jnp.zeros(acc_ref.shape, F32)
    q_t = qt_ref[0]

    def logits(kt, slot):
        rows = pl.ds(pl.multiple_of(kt * t, t), t)
        for hh in range(ATT_HEADS):
            hs = slice(hh * HEAD_DIM, (hh + 1) * HEAD_DIM)
            s_ref[slot, hh] = jnp.dot(k_ref[0, hh, rows, :], q_t[hs, :],
                                      preferred_element_type=F32)

    def update(kt, slot, off):
        v_t = vt_ref[0, kt]
        sel = mask_ref[0, 0, kt].astype(I32) > off
        for hh in range(ATT_HEADS):
            hs = slice(hh * HEAD_DIM, (hh + 1) * HEAD_DIM)
            s = jnp.where(sel, s_ref[slot, hh], MASKED_LOGIT)
            m_prev = m_ref[hh]
            m_new = jnp.maximum(m_prev, jnp.max(s, axis=0, keepdims=True))
            p = jnp.exp2(s - m_new)
            alpha = jnp.exp2(m_prev - m_new)
            m_ref[hh] = m_new
            lhs = jnp.concatenate([v_t[hs, :], jnp.ones((DENOM_ROWS, t), BF16)], axis=0)
            pv = jnp.dot(lhs, p.astype(BF16), preferred_element_type=F32)
            acc_ref[hh] = alpha * acc_ref[hh] + pv

    logits(0, 0)

    def body(j, carry):
        first = 2 * j
        second = jnp.minimum(first + 1, n - 1)
        logits(second, 1)
        update(first, 0, 0)
        logits(jnp.minimum(first + 2, n - 1), 0)
        update(second, 1, jnp.where(first + 1 < n, 0, 1))
        return carry

    lax.fori_loop(0, (n + 1) // 2, body, 0)
    for hh in range(ATT_HEADS):
        acc = acc_ref[hh]
        o_ref[0, hh * HEAD_DIM:(hh + 1) * HEAD_DIM, :] = (
            acc[:HEAD_DIM] / acc[HEAD_DIM:HEAD_DIM + 1]).astype(BF16)


def _attn(q_t, k, v_t, mask):
    b, _, s, _ = k.shape
    t = ATT_T
    nq = s // t
    width = ATT_HEADS * HEAD_DIM
    return pl.pallas_call(
        _attn_kernel,
        grid=(b, N_HEADS // ATT_HEADS, nq),
        in_specs=[
            pl.BlockSpec((1, width, t), lambda bi, hp, i: (bi, hp, i)),
            pl.BlockSpec((1, ATT_HEADS, s, HEAD_DIM), lambda bi, hp, i: (bi, hp, 0, 0)),
            pl.BlockSpec((1, nq, width, t), lambda bi, hp, i: (bi, 0, hp, 0)),
            pl.BlockSpec((1, 1, nq, t, t), lambda bi, hp, i: (bi, i, 0, 0, 0)),
        ],
        out_specs=pl.BlockSpec((1, width, t), lambda bi, hp, i: (bi, hp, i)),
        out_shape=jax.ShapeDtypeStruct((b, ATT_WIDTH, s), BF16),
        scratch_shapes=[
            pltpu.VMEM((ATT_HEADS, 1, t), F32),
            pltpu.VMEM((ATT_HEADS, HEAD_DIM + DENOM_ROWS, t), F32),
            pltpu.VMEM((2, ATT_HEADS, t, t), F32),
        ],
        compiler_params=_params("parallel", "parallel", "arbitrary"),
        name="masked_attention",
    )(q_t, k, v_t, mask)


def _merge_kernel(ya_ref, yb_ref, g_ref, x_ref, wa_ref, wb_ref, wo_ref, o_ref):
    d = x_ref.shape[1]
    a = lax.dot_general(ya_ref[0], wa_ref[...], (((0,), (0,)), ((), ())),
                        preferred_element_type=F32)
    b = jnp.dot(yb_ref[...], wb_ref[...], preferred_element_type=F32)
    merged = g_ref[:, :d].astype(F32) * a + g_ref[:, d:].astype(F32) * b
    o_ref[...] = x_ref[...] + jnp.dot(merged.astype(BF16), wo_ref[...],
                                      preferred_element_type=F32)


def _merge(ya_t, yb, gates, x, wa, wb, wo, *, tm=256):
    n, d = x.shape
    batch, width, s = ya_t.shape
    tm = min(tm, s)
    per_b = s // tm
    return pl.pallas_call(
        _merge_kernel,
        grid=(n // tm,),
        in_specs=[
            pl.BlockSpec((1, width, tm), lambda i: (i // per_b, 0, i % per_b)),
            pl.BlockSpec((tm, yb.shape[1]), lambda i: (i, 0)),
            pl.BlockSpec((tm, 2 * d), lambda i: (i, 0)),
            pl.BlockSpec((tm, d), lambda i: (i, 0)),
            pl.BlockSpec(wa.shape, lambda i: (0, 0)),
            pl.BlockSpec(wb.shape, lambda i: (0, 0)),
            pl.BlockSpec(wo.shape, lambda i: (0, 0)),
        ],
        out_specs=pl.BlockSpec((tm, d), lambda i: (i, 0)),
        out_shape=jax.ShapeDtypeStruct((n, d), F32),
        compiler_params=_params("parallel"),
        name="merge_out",
    )(ya_t, yb, gates, x, wa, wb, wo)


def _inv_freq_lanes(rot_dim, period, live):
    inv = ROPE_THETA ** (-jnp.arange(0, rot_dim, 2, dtype=F32) / rot_dim)
    lane = jnp.arange(LANES)
    j = lane % period
    vals = inv[j % (rot_dim // 2)]
    return jnp.where((j < rot_dim) & (lane < live), vals, 0.0).astype(F32)[None, :]


def _layer(x, tables_att, tables_idx, p):
    b, s, d = x.shape
    n = b * s
    topk = min(TOPK_MAX, s // 4)
    x = x.reshape(n, d)

    x1, h = _ffn(x, p["norm_ffn1"][None], p["ffn1_w_gate"].astype(BF16),
                 p["ffn1_w_up"].astype(BF16), p["ffn1_w_down"].astype(BF16),
                 p["norm_mix"][None], emit_norm=True)

    w_in = p["w_in"]
    c_qkv = 3 * ATT_WIDTH
    c_qi = c_qkv + IDX_HEADS * IDX_DIM
    c_idx = c_qi + IDX_DIM + IDX_HEADS
    c_sgu = c_idx + 2 * SGU_WIDTH
    w_q = w_in[:, :ATT_WIDTH].astype(BF16)
    w_k = w_in[:, ATT_WIDTH:2 * ATT_WIDTH].astype(BF16)
    w_v = w_in[:, 2 * ATT_WIDTH:c_qkv].astype(BF16)
    w_idx = jnp.pad(w_in[:, c_qkv:c_idx],
                    ((0, 0), (0, LANES - IDX_DIM - IDX_HEADS))).astype(BF16)
    w_sgu = w_in[:, c_idx:c_sgu].astype(BF16)
    w_gate = w_in[:, c_sgu:].astype(BF16)

    q_gain = (p["q_norm"] * (HEAD_DIM ** -0.5 * LOG2_E))[None]
    q_t = _head_proj(h, w_q, q_gain, tables_att, batch=b, transpose=True, name="q_proj")
    k = _head_proj(h, w_k, p["k_norm"][None], tables_att, batch=b, transpose=False,
                   name="k_proj")
    v_t = _v_proj(h, w_v, batch=b)
    kgain = jnp.pad(p["idx_k_norm"], (0, LANES - IDX_DIM))[None]
    qi_t, kcat, wi_t = _idx(h, w_idx, kgain, tables_idx, batch=b)
    yb = _sgu(h, w_sgu, p["sgu_v_norm"][None], p["sgu_w_s"], p["sgu_b_s"][:, :, None])
    gates = _gates(h, w_gate)

    mask = _select(qi_t, wi_t, kcat.reshape(b, s, 4 * IDX_DIM), topk=topk)
    ya_t = _attn(q_t, k, v_t, mask)

    x2 = _merge(ya_t, yb, gates, x1, p["w_up_attn"].astype(BF16),
                p["w_up_sgu"].astype(BF16), p["w_out"].astype(BF16))
    out = _ffn(x2, p["norm_ffn2"][None], p["ffn2_w_gate"].astype(BF16),
               p["ffn2_w_up"].astype(BF16), p["ffn2_w_down"].astype(BF16),
               p["norm_ffn2"][None], emit_norm=False)
    return out.reshape(b, s, d)


def kernel(x, positions, norm_ffn1, ffn1_w_gate, ffn1_w_up, ffn1_w_down, norm_mix, w_in,
           q_norm, k_norm, idx_k_norm, sgu_v_norm, sgu_w_s, sgu_b_s, w_up_attn, w_up_sgu,
           w_out, norm_ffn2, ffn2_w_gate, ffn2_w_up, ffn2_w_down):
    params = dict(
        norm_ffn1=norm_ffn1, ffn1_w_gate=ffn1_w_gate, ffn1_w_up=ffn1_w_up,
        ffn1_w_down=ffn1_w_down, norm_mix=norm_mix, w_in=w_in, q_norm=q_norm,
        k_norm=k_norm, idx_k_norm=idx_k_norm, sgu_v_norm=sgu_v_norm, sgu_w_s=sgu_w_s,
        sgu_b_s=sgu_b_s, w_up_attn=w_up_attn, w_up_sgu=w_up_sgu, w_out=w_out,
        norm_ffn2=norm_ffn2, ffn2_w_gate=ffn2_w_gate, ffn2_w_up=ffn2_w_up,
        ffn2_w_down=ffn2_w_down)
    pos = positions.reshape(-1, 1).astype(I32)
    invf_att = _inv_freq_lanes(ROT_DIM, LANES, LANES)
    invf_idx = _inv_freq_lanes(IDX_ROT_DIM, IDX_DIM, LANES)
    tables_att = _rope_table(pos, invf_att, half=ROT_DIM // 2, period=LANES,
                             name="rope_table_att")
    tables_idx = _rope_table(pos, invf_idx, half=IDX_ROT_DIM // 2, period=IDX_DIM,
                             name="rope_table_idx")
    for l in range(w_in.shape[0]):
        x = _layer(x, tables_att, tables_idx, {k: v[l] for k, v in params.items()})
    return x
```

```python
import functools

import jax
import jax.numpy as jnp
from jax import lax
from jax.experimental import pallas as pl
from jax.experimental.pallas import tpu as pltpu

F32 = jnp.float32
BF16 = jnp.bfloat16
I32 = jnp.int32

EPS = 1e-6
CHUNK = 64
N_HEADS = 8
HEAD_DIM = 128
ATT_WIDTH = N_HEADS * HEAD_DIM
ROPE_THETA = 500000.0
ROT_DIM = HEAD_DIM // 4
IDX_HEADS = 16
IDX_DIM = 64
IDX_ROT_DIM = IDX_DIM // 4
TOPK_MAX = 256
SGU_LEN = 128
SGU_GROUPS = 8
SGU_GROUP_DIM = 128
SGU_WIDTH = SGU_GROUPS * SGU_GROUP_DIM

LANES = 128
MXU_DIM = 256
VMEM_LIMIT_BYTES = 56 * 1024 * 1024

SEL_T = MXU_DIM
MASK_ROWS = 128
COUNT_ROWS = 16
ATT_T = 256
ATT_HEADS = 4
DENOM_ROWS = 16
ABOVE_MAX_REL = 1e-6
ABOVE_MAX_FLOOR = 1e-30
MASKED_LOGIT = -1e30
LOG2_E = 1.4426950408889634


def _params(*sem):
    return pltpu.CompilerParams(dimension_semantics=sem,
                                vmem_limit_bytes=VMEM_LIMIT_BYTES)


def _rms(t, gain):
    ms = jnp.mean(t * t, axis=-1, keepdims=True)
    return t * lax.rsqrt(ms + EPS) * gain


def _rope(t, c, s1, s2, half):
    return (t * c + pltpu.roll(t, LANES - half, 1) * s1
            + pltpu.roll(t, half, 1) * s2)


def _ffn_kernel(x_ref, g_ref, wg_ref, wu_ref, wd_ref, g2_ref, o_ref, *rest,
                emit_norm):
    if emit_norm:
        h_out_ref, hn_ref = rest
    else:
        (hn_ref,) = rest
    j = pl.program_id(1)

    @pl.when(j == 0)
    def _():
        hn_ref[...] = _rms(x_ref[...], g_ref[...]).astype(BF16)
        o_ref[...] = jnp.zeros_like(o_ref)

    h = hn_ref[...]
    a = jnp.dot(h, wg_ref[...], preferred_element_type=F32)
    b = jnp.dot(h, wu_ref[...], preferred_element_type=F32)
    act = (a * jax.nn.sigmoid(a) * b).astype(BF16)
    o_ref[...] += jnp.dot(act, wd_ref[...], preferred_element_type=F32)

    @pl.when(j == pl.num_programs(1) - 1)
    def _():
        y = x_ref[...] + 0.5 * o_ref[...]
        o_ref[...] = y
        if emit_norm:
            h_out_ref[...] = _rms(y, g2_ref[...]).astype(BF16)


def _ffn(x, gain, w_gate, w_up, w_down, gain2, *, emit_norm, tm=512, tf=512):
    n, d = x.shape
    f = w_gate.shape[1]
    tm = min(tm, n)
    out_shape = [jax.ShapeDtypeStruct((n, d), F32)]
    out_specs = [pl.BlockSpec((tm, d), lambda i, j: (i, 0))]
    if emit_norm:
        out_shape.append(jax.ShapeDtypeStruct((n, d), BF16))
        out_specs.append(pl.BlockSpec((tm, d), lambda i, j: (i, 0)))
    res = pl.pallas_call(
        functools.partial(_ffn_kernel, emit_norm=emit_norm),
        grid=(n // tm, f // tf),
        in_specs=[
            pl.BlockSpec((tm, d), lambda i, j: (i, 0)),
            pl.BlockSpec((1, d), lambda i, j: (0, 0)),
            pl.BlockSpec((d, tf), lambda i, j: (0, j)),
            pl.BlockSpec((d, tf), lambda i, j: (0, j)),
            pl.BlockSpec((tf, d), lambda i, j: (j, 0)),
            pl.BlockSpec((1, d), lambda i, j: (0, 0)),
        ],
        out_specs=out_specs,
        out_shape=out_shape,
        scratch_shapes=[pltpu.VMEM((tm, d), BF16)],
        compiler_params=_params("parallel", "arbitrary"),
        name="ffn_norm" if emit_norm else "ffn",
    )(x, gain, w_gate, w_up, w_down, gain2)
    return res if emit_norm else res[0]


def _rope_table_kernel(pos_ref, invf_ref, ca_ref, s1a_ref, s2a_ref, ci_ref, s1i_ref, s2i_ref):
    ang = pos_ref[...].astype(F32) * invf_ref[...]
    c = jnp.cos(ang)
    s = jnp.sin(ang)
    lane = lax.broadcasted_iota(I32, ang.shape, 1)
    in_att = lane < ROT_DIM
    ca_ref[...] = jnp.where(in_att, c, 1.0)
    s1a_ref[...] = jnp.where(lane < ROT_DIM // 2, -s, 0.0)
    s2a_ref[...] = jnp.where((lane >= ROT_DIM // 2) & in_att, s, 0.0)
    first = lane < IDX_DIM
    c_i = jnp.where(first, pltpu.roll(c, LANES - ROT_DIM, 1), pltpu.roll(c, IDX_DIM - ROT_DIM, 1))
    s_i = jnp.where(first, pltpu.roll(s, LANES - ROT_DIM, 1), pltpu.roll(s, IDX_DIM - ROT_DIM, 1))
    j = lane & (IDX_DIM - 1)
    in_idx = j < IDX_ROT_DIM
    ci_ref[...] = jnp.where(in_idx, c_i, 1.0)
    s1i_ref[...] = jnp.where(j < IDX_ROT_DIM // 2, -s_i, 0.0)
    s2i_ref[...] = jnp.where((j >= IDX_ROT_DIM // 2) & in_idx, s_i, 0.0)


def _rope_table(pos, invf, *, tm=2048):
    n = pos.shape[0]
    tm = min(tm, n)
    spec = pl.BlockSpec((tm, LANES), lambda i: (i, 0))
    out = pl.pallas_call(
        _rope_table_kernel,
        grid=(n // tm,),
        in_specs=[pl.BlockSpec((tm, 1), lambda i: (i, 0)),
                  pl.BlockSpec((1, LANES), lambda i: (0, 0))],
        out_specs=[spec] * 6,
        out_shape=[jax.ShapeDtypeStruct((n, LANES), F32)] * 6,
        compiler_params=_params("parallel"),
        name="rope_tables",
    )(pos, invf)
    return out[:3], out[3:]


def _head_proj_kernel(h_ref, w_ref, gain_ref, c_ref, s1_ref, s2_ref, o_ref, r_ref,
                      *, transpose):
    half = ROT_DIM // 2
    res = jnp.dot(h_ref[...], w_ref[...], preferred_element_type=F32)
    heads = res.shape[1] // HEAD_DIM
    for hh in range(heads):
        r_ref[hh] = res[:, hh * HEAD_DIM:(hh + 1) * HEAD_DIM]

    def head(hh, carry):
        t = _rope(_rms(r_ref[hh], gain_ref[...]), c_ref[...], s1_ref[...], s2_ref[...], half)
        if transpose:
            o_ref[0, pl.ds(pl.multiple_of(hh * HEAD_DIM, HEAD_DIM), HEAD_DIM), :] = (
                t.T.astype(BF16))
        else:
            o_ref[0, hh] = t.astype(BF16)
        return carry

    lax.fori_loop(0, heads, head, 0)


def _head_proj(h, w, gain, tables, *, batch, transpose, name, tm=1024, tn=512):
    n, d = h.shape
    cols = w.shape[1]
    s = n // batch
    tm = min(tm, s)
    per_b = s // tm
    heads = tn // HEAD_DIM
    table_spec = pl.BlockSpec((tm, LANES), lambda i, j: (i, 0))
    if transpose:
        out_spec = pl.BlockSpec((1, tn, tm), lambda i, j: (i // per_b, j, i % per_b))
        out_shape = jax.ShapeDtypeStruct((batch, cols, s), BF16)
    else:
        out_spec = pl.BlockSpec((1, heads, tm, HEAD_DIM),
                                lambda i, j: (i // per_b, j, i % per_b, 0))
        out_shape = jax.ShapeDtypeStruct((batch, cols // HEAD_DIM, s, HEAD_DIM), BF16)
    return pl.pallas_call(
        functools.partial(_head_proj_kernel, transpose=transpose),
        grid=(n // tm, cols // tn),
        in_specs=[
            pl.BlockSpec((tm, d), lambda i, j: (i, 0)),
            pl.BlockSpec((d, tn), lambda i, j: (0, j)),
            pl.BlockSpec((1, HEAD_DIM), lambda i, j: (0, 0)),
            table_spec, table_spec, table_spec,
        ],
        out_specs=out_spec,
        out_shape=out_shape,
        scratch_shapes=[pltpu.VMEM((heads, tm, HEAD_DIM), F32)],
        compiler_params=_params("parallel", "arbitrary"),
        name=name,
    )(h, w, gain, *tables)


def _v_proj_kernel(h_ref, w_ref, o_ref):
    res = jnp.dot(h_ref[...], w_ref[...], preferred_element_type=F32)
    for a in range(res.shape[0] // ATT_T):
        o_ref[0, a] = res[a * ATT_T:(a + 1) * ATT_T, :].T.astype(BF16)


def _v_proj(h, w, *, batch, tm=1024, tn=512):
    n, d = h.shape
    cols = w.shape[1]
    s = n // batch
    tm = min(tm, s)
    per_b = s // tm
    return pl.pallas_call(
        _v_proj_kernel,
        grid=(n // tm, cols // tn),
        in_specs=[
            pl.BlockSpec((tm, d), lambda i, j: (i, 0)),
            pl.BlockSpec((d, tn), lambda i, j: (0, j)),
        ],
        out_specs=pl.BlockSpec((1, tm // ATT_T, tn, ATT_T),
                               lambda i, j: (i // per_b, i % per_b, j, 0)),
        out_shape=jax.ShapeDtypeStruct((batch, s // ATT_T, cols, ATT_T), BF16),
        compiler_params=_params("parallel", "arbitrary"),
        name="v_proj",
    )(h, w)


def _idx_kernel(h_ref, w_ref, kgain_ref, c_ref, s1_ref, s2_ref,
                qi_ref, kcat_ref, wi_ref, *, w_scale):
    half = IDX_ROT_DIM // 2
    qw = IDX_HEADS * IDX_DIM
    res = jnp.dot(h_ref[...], w_ref[...], preferred_element_type=F32)
    c, s1, s2 = c_ref[...], s1_ref[...], s2_ref[...]
    for t in range(qw // LANES):
        sl = slice(t * LANES, (t + 1) * LANES)
        q_t = _rope(res[:, sl], c, s1, s2, half).T
        qi_ref[0, 2 * t] = q_t[:IDX_DIM]
        qi_ref[0, 2 * t + 1] = q_t[IDX_DIM:]

    r = res[:, qw:qw + LANES]
    lane = lax.broadcasted_iota(I32, r.shape, 1)
    is_k = lane < IDX_DIM
    ms = jnp.sum(jnp.where(is_k, r * r, 0.0), axis=-1, keepdims=True) * (1.0 / IDX_DIM)
    kn = r * lax.rsqrt(ms + EPS) * kgain_ref[...]
    kr = _rope(kn, jnp.where(is_k, c, 1.0), jnp.where(is_k, s1, 0.0),
               jnp.where(is_k, s2, 0.0), half)
    hi = kr.astype(BF16).astype(F32)
    lo = kr - hi
    hi_lo = jnp.where(is_k, hi, pltpu.roll(lo, IDX_DIM, 1)).astype(BF16)
    kcat_ref[:, 0:LANES] = hi_lo
    kcat_ref[:, LANES:2 * LANES] = hi_lo
    is_w = (lane >= IDX_DIM) & (lane < IDX_DIM + IDX_HEADS)
    w_t = pltpu.roll(jnp.where(is_w, r * w_scale, 0.0), LANES - IDX_DIM, 1).T
    wi_ref[0] = w_t[:IDX_HEADS]


def _idx(h, w_idx, kgain, tables, *, batch, tm=512):
    n, d = h.shape
    cols = w_idx.shape[1]
    s = n // batch
    tm = min(tm, s)
    per_b = s // tm
    w_scale = (IDX_HEADS ** -0.5) * (IDX_DIM ** -0.5)
    return pl.pallas_call(
        functools.partial(_idx_kernel, w_scale=w_scale),
        grid=(n // tm,),
        in_specs=[
            pl.BlockSpec((tm, d), lambda i: (i, 0)),
            pl.BlockSpec((d, cols), lambda i: (0, 0)),
            pl.BlockSpec((1, LANES), lambda i: (0, 0)),
            pl.BlockSpec((tm, LANES), lambda i: (i, 0)),
            pl.BlockSpec((tm, LANES), lambda i: (i, 0)),
            pl.BlockSpec((tm, LANES), lambda i: (i, 0)),
        ],
        out_specs=[
            pl.BlockSpec((1, IDX_HEADS, IDX_DIM, tm),
                         lambda i: (i // per_b, 0, 0, i % per_b)),
            pl.BlockSpec((tm, 2 * LANES), lambda i: (i, 0)),
            pl.BlockSpec((1, IDX_HEADS, tm), lambda i: (i // per_b, 0, i % per_b)),
        ],
        out_shape=[
            jax.ShapeDtypeStruct((batch, IDX_HEADS, IDX_DIM, s), F32),
            jax.ShapeDtypeStruct((n, 2 * LANES), BF16),
            jax.ShapeDtypeStruct((batch, IDX_HEADS, s), F32),
        ],
        compiler_params=_params("parallel"),
        name="idx_proj",
    )(h, w_idx, kgain, *tables)


def _sgu_kernel(h_ref, w_ref, gv_ref, ws_ref, bs_ref, o_ref):
    tm = h_ref.shape[0]
    groups = tm // SGU_LEN
    res = jnp.dot(h_ref[...], w_ref[...], preferred_element_type=F32)
    u = jax.nn.gelu(res[:, :SGU_WIDTH])
    v = _rms(jax.nn.gelu(res[:, SGU_WIDTH:]), gv_ref[...]).astype(BF16)
    row = lax.broadcasted_iota(I32, (SGU_LEN, SGU_LEN), 0)
    col = lax.broadcasted_iota(I32, (SGU_LEN, SGU_LEN), 1)
    causal = (col // CHUNK) <= (row // CHUNK)
    for g in range(SGU_GROUPS):
        cs = slice(g * SGU_GROUP_DIM, (g + 1) * SGU_GROUP_DIM)
        wg = jnp.where(causal, ws_ref[g], 0.0).astype(BF16)
        vg = jnp.concatenate(
            [v[n * SGU_LEN:(n + 1) * SGU_LEN, cs] for n in range(groups)], axis=1)
        mixed = jnp.dot(wg, vg, preferred_element_type=F32) + bs_ref[g]
        for n in range(groups):
            rs = slice(n * SGU_LEN, (n + 1) * SGU_LEN)
            m = mixed[:, n * SGU_GROUP_DIM:(n + 1) * SGU_GROUP_DIM]
            o_ref[rs, cs] = (u[rs, cs] * m).astype(BF16)


def _sgu(h, w_sgu, gv, w_s, b_s, *, tm=512):
    n, d = h.shape
    tm = min(tm, n)
    return pl.pallas_call(
        _sgu_kernel,
        grid=(n // tm,),
        in_specs=[
            pl.BlockSpec((tm, d), lambda i: (i, 0)),
            pl.BlockSpec((d, 2 * SGU_WIDTH), lambda i: (0, 0)),
            pl.BlockSpec((1, SGU_WIDTH), lambda i: (0, 0)),
            pl.BlockSpec((SGU_GROUPS, SGU_LEN, SGU_LEN), lambda i: (0, 0, 0)),
            pl.BlockSpec((SGU_GROUPS, SGU_LEN, 1), lambda i: (0, 0, 0)),
        ],
        out_specs=pl.BlockSpec((tm, SGU_WIDTH), lambda i: (i, 0)),
        out_shape=jax.ShapeDtypeStruct((n, SGU_WIDTH), BF16),
        compiler_params=_params("parallel"),
        name="sgu_branch",
    )(h, w_sgu, gv, w_s, b_s)


def _gates_kernel(h_ref, w_ref, o_ref):
    res = jnp.dot(h_ref[...], w_ref[...], preferred_element_type=F32)
    o_ref[...] = jax.nn.sigmoid(res).astype(BF16)


def _gates(h, w_g, *, tm=1024, tn=1024):
    n, d = h.shape
    cols = w_g.shape[1]
    tm = min(tm, n)
    return pl.pallas_call(
        _gates_kernel,
        grid=(n // tm, cols // tn),
        in_specs=[
            pl.BlockSpec((tm, d), lambda i, j: (i, 0)),
            pl.BlockSpec((d, tn), lambda i, j: (0, j)),
        ],
        out_specs=pl.BlockSpec((tm, tn), lambda i, j: (i, j)),
        out_shape=jax.ShapeDtypeStruct((n, cols), BF16),
        compiler_params=_params("parallel", "arbitrary"),
        name="gates_proj",
    )(h, w_g)


def _select_kernel(qi_ref, wi_ref, kcat_ref, mask_ref, qcat_ref, s_ref, *, topk):
    t = SEL_T
    n_tiles = s_ref.shape[0]
    i = pl.program_id(1)
    n_live = i + 1

    for h in range(IDX_HEADS):
        q = qi_ref[0, h]
        hi = q.astype(BF16)
        lo = (q - hi.astype(F32)).astype(BF16)
        qcat_ref[h] = jnp.concatenate([hi, hi, lo, lo], axis=0)
    w = wi_ref[0]

    q_chunk = (i * t + lax.broadcasted_iota(I32, (t, t), 1)) // CHUNK
    k_row = lax.broadcasted_iota(I32, (t, t), 0)

    def score_tile(kt, mx, mn):
        kc = kcat_ref[0, pl.ds(pl.multiple_of(kt * t, t), t), :]
        acc = jnp.zeros((t, t), F32)
        for h in range(IDX_HEADS):
            d = jnp.dot(kc, qcat_ref[h], preferred_element_type=F32)
            acc = acc + jnp.maximum(d, 0.0) * w[h:h + 1, :]
        admissible = ((kt * t + k_row) // CHUNK) <= q_chunk
        s = jnp.where(admissible, acc, -jnp.inf)
        s_ref[kt] = s
        mx = jnp.maximum(mx, jnp.max(s, axis=0, keepdims=True))
        mn = jnp.minimum(mn, jnp.min(jnp.where(admissible, acc, jnp.inf),
                                     axis=0, keepdims=True))
        return mx, mn

    def score_body(j, carry):
        mx, mn = score_tile(2 * j, *carry)
        return score_tile(jnp.minimum(2 * j + 1, n_live - 1), mx, mn)

    mx, mn = lax.fori_loop(
        0, (n_live + 1) // 2, score_body,
        (jnp.full((1, t), -jnp.inf, F32), jnp.full((1, t), jnp.inf, F32)))

    n_adm = ((i * t + lax.broadcasted_iota(I32, (1, t), 1)) // CHUNK + 1) * CHUNK
    done0 = n_adm <= topk

    def count(c, strict=False):
        def add_tile(kt, weight, acc):
            s = s_ref[kt]
            ind = jnp.where(s > c if strict else s >= c, weight, 0.0)
            for r in range(t // COUNT_ROWS):
                acc = acc + ind[r * COUNT_ROWS:(r + 1) * COUNT_ROWS]
            return acc

        def count_body(j, acc):
            acc = add_tile(2 * j, 1.0, acc)
            second = 2 * j + 1
            return add_tile(jnp.minimum(second, n_live - 1),
                            jnp.where(second < n_live, 1.0, 0.0), acc)

        acc = lax.fori_loop(0, (n_live + 1) // 2, count_body,
                            jnp.zeros((COUNT_ROWS, t), F32))
        return jnp.sum(acc, axis=0, keepdims=True)

    def search_cond(st):
        return st[6] > 0.0

    def search_body(st):
        lo, hi, c_lo, thr, n_thr, live, _ = st
        mid = 0.5 * lo + 0.5 * hi
        stuck = (mid <= lo) | (mid >= hi)
        cnt = count(mid)
        finish = (live > 0.0) & (stuck | (cnt == topk))
        thr = jnp.where(finish, jnp.where(stuck, lo, mid), thr)
        n_thr = jnp.where(finish, jnp.where(stuck, c_lo, cnt), n_thr)
        live = jnp.where(finish, 0.0, live)
        above = cnt > topk
        lo = jnp.where(above, mid, lo)
        c_lo = jnp.where(above, cnt, c_lo)
        hi = jnp.where(cnt < topk, mid, hi)
        return lo, hi, c_lo, thr, n_thr, live, jnp.sum(live)

    hi0 = jnp.minimum(mx + jnp.maximum(jnp.abs(mx), ABOVE_MAX_FLOOR) * ABOVE_MAX_REL,
                      jnp.finfo(F32).max)
    live0 = jnp.where(done0, 0.0, 1.0)
    st0 = (mn, hi0, n_adm.astype(F32), jnp.full((1, t), -jnp.inf, F32),
           jnp.zeros((1, t), F32), live0, jnp.sum(live0))
    st = lax.while_loop(search_cond, search_body, st0)
    thr, n_thr = st[3], st[4]
    tied = n_thr > topk
    any_tied = jnp.sum(jnp.where(tied, 1.0, 0.0)) > 0.0

    @pl.when(jnp.logical_not(any_tied))
    def _():
        def mask_body(kt, carry):
            s = s_ref[kt]
            sel = (s >= thr) & (s > -jnp.inf)
            mask_ref[0, 0, kt] = jnp.where(sel, 1, 0).astype(jnp.int8)
            return carry

        lax.fori_loop(0, n_live, mask_body, 0)

    @pl.when(any_tied)
    def _():
        room = topk - count(thr, strict=True)
        tri = jnp.where(lax.broadcasted_iota(I32, (t, t), 0)
                        >= lax.broadcasted_iota(I32, (t, t), 1), 1.0, 0.0).astype(BF16)

        def tie_body(kt, seen):
            s = s_ref[kt]
            eq = (s == thr) & tied
            eq_f = jnp.where(eq, 1.0, 0.0)
            upto = jnp.dot(tri, eq_f.astype(BF16), preferred_element_type=F32)
            rank = seen + upto - eq_f
            keep = (eq & (rank < room)) | (jnp.logical_not(eq) & (s >= thr))
            sel = keep & (s > -jnp.inf)
            mask_ref[0, 0, kt] = jnp.where(sel, 1, 0).astype(jnp.int8)
            return seen + upto[t - 1:t, :]

        lax.fori_loop(0, n_live, tie_body, jnp.zeros((1, t), F32))

    def zero_body(kt, carry):
        mask_ref[0, 0, kt] = jnp.zeros((t, t), jnp.int8)
        return carry

    lax.fori_loop(n_live, n_tiles, zero_body, 0)


def _select(qi_t, wi_t, kcat, *, topk):
    b, heads, dh, s = qi_t.shape
    t = SEL_T
    nq = s // t
    return pl.pallas_call(
        functools.partial(_select_kernel, topk=topk),
        grid=(b, nq),
        in_specs=[
            pl.BlockSpec((1, heads, dh, t), lambda bi, i: (bi, 0, 0, i)),
            pl.BlockSpec((1, heads, t), lambda bi, i: (bi, 0, i)),
            pl.BlockSpec((1, s, 4 * IDX_DIM), lambda bi, i: (bi, 0, 0)),
        ],
        out_specs=pl.BlockSpec((1, 1, nq, t, t), lambda bi, i: (bi, i, 0, 0, 0)),
        out_shape=jax.ShapeDtypeStruct((b, nq, nq, t, t), jnp.int8),
        scratch_shapes=[
            pltpu.VMEM((heads, 4 * IDX_DIM, t), BF16),
            pltpu.VMEM((nq, t, t), F32),
        ],
        compiler_params=_params("parallel", "arbitrary"),
        name="index_select",
    )(qi_t, wi_t, kcat)


def _attn_kernel(qt_ref, k_ref, vt_ref, mask_ref, o_ref,
                 m_ref, acc_ref, s_ref):
    t = ATT_T
    i = pl.program_id(2)
    n = i + 1
    m_ref[...] = jnp.full(m_ref.shape, -jnp.inf, F32)
    acc_ref[...] = jnp.zeros(acc_ref.shape, F32)
    q_t = qt_ref[0]

    def logits(kt, slot):
        rows = pl.ds(pl.multiple_of(kt * t, t), t)
        for hh in range(ATT_HEADS):
            hs = slice(hh * HEAD_DIM, (hh + 1) * HEAD_DIM)
            s_ref[slot, hh] = jnp.dot(k_ref[0, hh, rows, :], q_t[hs, :],
                                      preferred_element_type=F32)

    def update(kt, slot, off):
        v_t = vt_ref[0, kt]
        sel = mask_ref[0, 0, kt].astype(I32) > off
        for hh in range(ATT_HEADS):
            hs = slice(hh * HEAD_DIM, (hh + 1) * HEAD_DIM)
            s = jnp.where(sel, s_ref[slot, hh], MASKED_LOGIT)
            m_prev = m_ref[hh]
            m_new = jnp.maximum(m_prev, jnp.max(s, axis=0, keepdims=True))
            p = jnp.exp2(s - m_new)
            alpha = jnp.exp2(m_prev - m_new)
            m_ref[hh] = m_new
            lhs = jnp.concatenate([v_t[hs, :], jnp.ones((DENOM_ROWS, t), BF16)], axis=0)
            pv = jnp.dot(lhs, p.astype(BF16), preferred_element_type=F32)
            acc_ref[hh] = alpha * acc_ref[hh] + pv

    logits(0, 0)

    def body(j, carry):
        first = 2 * j
        second = jnp.minimum(first + 1, n - 1)
        logits(second, 1)
        update(first, 0, 0)
        logits(jnp.minimum(first + 2, n - 1), 0)
        update(second, 1, jnp.where(first + 1 < n, 0, 1))
        return carry

    lax.fori_loop(0, (n + 1) // 2, body, 0)
    for hh in range(ATT_HEADS):
        acc = acc_ref[hh]
        o_ref[0, hh * HEAD_DIM:(hh + 1) * HEAD_DIM, :] = (
            acc[:HEAD_DIM] / acc[HEAD_DIM:HEAD_DIM + 1]).astype(BF16)


def _attn(q_t, k, v_t, mask):
    b, _, s, _ = k.shape
    t = ATT_T
    nq = s // t
    width = ATT_HEADS * HEAD_DIM
    return pl.pallas_call(
        _attn_kernel,
        grid=(b, N_HEADS // ATT_HEADS, nq),
        in_specs=[
            pl.BlockSpec((1, width, t), lambda bi, hp, i: (bi, hp, i)),
            pl.BlockSpec((1, ATT_HEADS, s, HEAD_DIM), lambda bi, hp, i: (bi, hp, 0, 0)),
            pl.BlockSpec((1, nq, width, t), lambda bi, hp, i: (bi, 0, hp, 0)),
            pl.BlockSpec((1, 1, nq, t, t), lambda bi, hp, i: (bi, i, 0, 0, 0)),
        ],
        out_specs=pl.BlockSpec((1, width, t), lambda bi, hp, i: (bi, hp, i)),
        out_shape=jax.ShapeDtypeStruct((b, ATT_WIDTH, s), BF16),
        scratch_shapes=[
            pltpu.VMEM((ATT_HEADS, 1, t), F32),
            pltpu.VMEM((ATT_HEADS, HEAD_DIM + DENOM_ROWS, t), F32),
            pltpu.VMEM((2, ATT_HEADS, t, t), F32),
        ],
        compiler_params=_params("parallel", "parallel", "arbitrary"),
        name="masked_attention",
    )(q_t, k, v_t, mask)


def _merge_kernel(ya_ref, yb_ref, g_ref, x_ref, wa_ref, wb_ref, wo_ref, o_ref):
    d = x_ref.shape[1]
    a = lax.dot_general(ya_ref[0], wa_ref[...], (((0,), (0,)), ((), ())),
                        preferred_element_type=F32)
    b = jnp.dot(yb_ref[...], wb_ref[...], preferred_element_type=F32)
    merged = g_ref[:, :d].astype(F32) * a + g_ref[:, d:].astype(F32) * b
    o_ref[...] = x_ref[...] + jnp.dot(merged.astype(BF16), wo_ref[...],
                                      preferred_element_type=F32)


def _merge(ya_t, yb, gates, x, wa, wb, wo, *, tm=256):
    n, d = x.shape
    batch, width, s = ya_t.shape
    tm = min(tm, s)
    per_b = s // tm
    return pl.pallas_call(
        _merge_kernel,
        grid=(n // tm,),
        in_specs=[
            pl.BlockSpec((1, width, tm), lambda i: (i // per_b, 0, i % per_b)),
            pl.BlockSpec((tm, yb.shape[1]), lambda i: (i, 0)),
            pl.BlockSpec((tm, 2 * d), lambda i: (i, 0)),
            pl.BlockSpec((tm, d), lambda i: (i, 0)),
            pl.BlockSpec(wa.shape, lambda i: (0, 0)),
            pl.BlockSpec(wb.shape, lambda i: (0, 0)),
            pl.BlockSpec(wo.shape, lambda i: (0, 0)),
        ],
        out_specs=pl.BlockSpec((tm, d), lambda i: (i, 0)),
        out_shape=jax.ShapeDtypeStruct((n, d), F32),
        compiler_params=_params("parallel"),
        name="merge_out",
    )(ya_t, yb, gates, x, wa, wb, wo)


def _inv_freq_lanes():
    def inv(rot_dim):
        f = ROPE_THETA ** (-jnp.arange(0, rot_dim, 2, dtype=F32) / rot_dim)
        return jnp.concatenate([f, f])
    lanes = jnp.concatenate([inv(ROT_DIM), inv(IDX_ROT_DIM),
                             jnp.zeros((LANES - ROT_DIM - IDX_ROT_DIM,), F32)])
    return lanes[None, :]


def _layer(x, tables_att, tables_idx, p):
    b, s, d = x.shape
    n = b * s
    topk = min(TOPK_MAX, s // 4)
    x = x.reshape(n, d)

    x1, h = _ffn(x, p["norm_ffn1"][None], p["ffn1_w_gate"].astype(BF16),
                 p["ffn1_w_up"].astype(BF16), p["ffn1_w_down"].astype(BF16),
                 p["norm_mix"][None], emit_norm=True)

    w_in = p["w_in"]
    c_qkv = 3 * ATT_WIDTH
    c_qi = c_qkv + IDX_HEADS * IDX_DIM
    c_idx = c_qi + IDX_DIM + IDX_HEADS
    c_sgu = c_idx + 2 * SGU_WIDTH
    w_q = w_in[:, :ATT_WIDTH].astype(BF16)
    w_k = w_in[:, ATT_WIDTH:2 * ATT_WIDTH].astype(BF16)
    w_v = w_in[:, 2 * ATT_WIDTH:c_qkv].astype(BF16)
    w_idx = jnp.pad(w_in[:, c_qkv:c_idx],
                    ((0, 0), (0, LANES - IDX_DIM - IDX_HEADS))).astype(BF16)
    w_sgu = w_in[:, c_idx:c_sgu].astype(BF16)
    w_gate = w_in[:, c_sgu:].astype(BF16)

    q_gain = (p["q_norm"] * (HEAD_DIM ** -0.5 * LOG2_E))[None]
    q_t = _head_proj(h, w_q, q_gain, tables_att, batch=b, transpose=True, name="q_proj")
    k = _head_proj(h, w_k, p["k_norm"][None], tables_att, batch=b, transpose=False,
                   name="k_proj")
    v_t = _v_proj(h, w_v, batch=b)
    kgain = jnp.pad(p["idx_k_norm"], (0, LANES - IDX_DIM))[None]
    qi_t, kcat, wi_t = _idx(h, w_idx, kgain, tables_idx, batch=b)
    yb = _sgu(h, w_sgu, p["sgu_v_norm"][None], p["sgu_w_s"], p["sgu_b_s"][:, :, None])
    gates = _gates(h, w_gate)

    mask = _select(qi_t, wi_t, kcat.reshape(b, s, 4 * IDX_DIM), topk=topk)
    ya_t = _attn(q_t, k, v_t, mask)

    x2 = _merge(ya_t, yb, gates, x1, p["w_up_attn"].astype(BF16),
                p["w_up_sgu"].astype(BF16), p["w_out"].astype(BF16))
    out = _ffn(x2, p["norm_ffn2"][None], p["ffn2_w_gate"].astype(BF16),
               p["ffn2_w_up"].astype(BF16), p["ffn2_w_down"].astype(BF16),
               p["norm_ffn2"][None], emit_norm=False)
    return out.reshape(b, s, d)


def kernel(x, positions, norm_ffn1, ffn1_w_gate, ffn1_w_up, ffn1_w_down, norm_mix, w_in,
           q_norm, k_norm, idx_k_norm, sgu_v_norm, sgu_w_s, sgu_b_s, w_up_attn, w_up_sgu,
           w_out, norm_ffn2, ffn2_w_gate, ffn2_w_up, ffn2_w_down):
    params = dict(
        norm_ffn1=norm_ffn1, ffn1_w_gate=ffn1_w_gate, ffn1_w_up=ffn1_w_up,
        ffn1_w_down=ffn1_w_down, norm_mix=norm_mix, w_in=w_in, q_norm=q_norm,
        k_norm=k_norm, idx_k_norm=idx_k_norm, sgu_v_norm=sgu_v_norm, sgu_w_s=sgu_w_s,
        sgu_b_s=sgu_b_s, w_up_attn=w_up_attn, w_up_sgu=w_up_sgu, w_out=w_out,
        norm_ffn2=norm_ffn2, ffn2_w_gate=ffn2_w_gate, ffn2_w_up=ffn2_w_up,
        ffn2_w_down=ffn2_w_down)
    pos = positions.reshape(-1, 1).astype(I32)
    tables_att, tables_idx = _rope_table(pos, _inv_freq_lanes())
    for l in range(w_in.shape[0]):
        x = _layer(x, tables_att, tables_idx, {k: v[l] for k, v in params.items()})
    return x
```

```python
import functools

import jax
import jax.numpy as jnp
from jax import lax
from jax.experimental import pallas as pl
from jax.experimental.pallas import tpu as pltpu

F32 = jnp.float32
BF16 = jnp.bfloat16
I32 = jnp.int32

EPS = 1e-6
CHUNK = 64
N_HEADS = 8
HEAD_DIM = 128
ATT_WIDTH = N_HEADS * HEAD_DIM
ROPE_THETA = 500000.0
ROT_DIM = HEAD_DIM // 4
IDX_HEADS = 16
IDX_DIM = 64
IDX_ROT_DIM = IDX_DIM // 4
TOPK_MAX = 256
SGU_LEN = 128
SGU_GROUPS = 8
SGU_GROUP_DIM = 128
SGU_WIDTH = SGU_GROUPS * SGU_GROUP_DIM

LANES = 128
MXU_DIM = 256
VMEM_LIMIT_BYTES = 56 * 1024 * 1024

SEL_T = MXU_DIM
COUNT_ROWS = 16
ATT_T = 256
ATT_HEADS = 4
DENOM_ROWS = 16
ABOVE_MAX_REL = 1e-6
ABOVE_MAX_FLOOR = 1e-30
MASKED_LOGIT = -1e30
LOG2_E = 1.4426950408889634


def _params(*sem):
    return pltpu.CompilerParams(dimension_semantics=sem,
                                vmem_limit_bytes=VMEM_LIMIT_BYTES)


def _rms(t, gain):
    ms = jnp.mean(t * t, axis=-1, keepdims=True)
    return t * lax.rsqrt(ms + EPS) * gain


def _rope(t, c, s1, s2, half):
    return (t * c + pltpu.roll(t, LANES - half, 1) * s1
            + pltpu.roll(t, half, 1) * s2)


def _ffn_kernel(x_ref, g_ref, wg_ref, wu_ref, wd_ref, g2_ref, o_ref, *rest,
                emit_norm):
    if emit_norm:
        h_out_ref, hn_ref = rest
    else:
        (hn_ref,) = rest
    j = pl.program_id(1)

    @pl.when(j == 0)
    def _():
        hn_ref[...] = _rms(x_ref[...], g_ref[...]).astype(BF16)
        o_ref[...] = jnp.zeros_like(o_ref)

    h = hn_ref[...]
    a = jnp.dot(h, wg_ref[...], preferred_element_type=F32)
    b = jnp.dot(h, wu_ref[...], preferred_element_type=F32)
    act = (a * jax.nn.sigmoid(a) * b).astype(BF16)
    o_ref[...] += jnp.dot(act, wd_ref[...], preferred_element_type=F32)

    @pl.when(j == pl.num_programs(1) - 1)
    def _():
        y = x_ref[...] + 0.5 * o_ref[...]
        o_ref[...] = y
        if emit_norm:
            h_out_ref[...] = _rms(y, g2_ref[...]).astype(BF16)


def _ffn(x, gain, w_gate, w_up, w_down, gain2, *, emit_norm, tm=512, tf=512):
    n, d = x.shape
    f = w_gate.shape[1]
    tm = min(tm, n)
    out_shape = [jax.ShapeDtypeStruct((n, d), F32)]
    out_specs = [pl.BlockSpec((tm, d), lambda i, j: (i, 0))]
    if emit_norm:
        out_shape.append(jax.ShapeDtypeStruct((n, d), BF16))
        out_specs.append(pl.BlockSpec((tm, d), lambda i, j: (i, 0)))
    res = pl.pallas_call(
        functools.partial(_ffn_kernel, emit_norm=emit_norm),
        grid=(n // tm, f // tf),
        in_specs=[
            pl.BlockSpec((tm, d), lambda i, j: (i, 0)),
            pl.BlockSpec((1, d), lambda i, j: (0, 0)),
            pl.BlockSpec((d, tf), lambda i, j: (0, j)),
            pl.BlockSpec((d, tf), lambda i, j: (0, j)),
            pl.BlockSpec((tf, d), lambda i, j: (j, 0)),
            pl.BlockSpec((1, d), lambda i, j: (0, 0)),
        ],
        out_specs=out_specs,
        out_shape=out_shape,
        scratch_shapes=[pltpu.VMEM((tm, d), BF16)],
        compiler_params=_params("parallel", "arbitrary"),
        name="ffn_norm" if emit_norm else "ffn",
    )(x, gain, w_gate, w_up, w_down, gain2)
    return res if emit_norm else res[0]


def _rope_table_kernel(pos_ref, invf_ref, ca_ref, s1a_ref, s2a_ref, ci_ref, s1i_ref, s2i_ref):
    ang = pos_ref[...].astype(F32) * invf_ref[...]
    c = jnp.cos(ang)
    s = jnp.sin(ang)
    lane = lax.broadcasted_iota(I32, ang.shape, 1)
    in_att = lane < ROT_DIM
    ca_ref[...] = jnp.where(in_att, c, 1.0)
    s1a_ref[...] = jnp.where(lane < ROT_DIM // 2, -s, 0.0)
    s2a_ref[...] = jnp.where((lane >= ROT_DIM // 2) & in_att, s, 0.0)
    first = lane < IDX_DIM
    c_i = jnp.where(first, pltpu.roll(c, LANES - ROT_DIM, 1), pltpu.roll(c, IDX_DIM - ROT_DIM, 1))
    s_i = jnp.where(first, pltpu.roll(s, LANES - ROT_DIM, 1), pltpu.roll(s, IDX_DIM - ROT_DIM, 1))
    j = lane & (IDX_DIM - 1)
    in_idx = j < IDX_ROT_DIM
    ci_ref[...] = jnp.where(in_idx, c_i, 1.0)
    s1i_ref[...] = jnp.where(j < IDX_ROT_DIM // 2, -s_i, 0.0)
    s2i_ref[...] = jnp.where((j >= IDX_ROT_DIM // 2) & in_idx, s_i, 0.0)


def _rope_table(pos, invf, *, tm=2048):
    n = pos.shape[0]
    tm = min(tm, n)
    spec = pl.BlockSpec((tm, LANES), lambda i: (i, 0))
    out = pl.pallas_call(
        _rope_table_kernel,
        grid=(n // tm,),
        in_specs=[pl.BlockSpec((tm, 1), lambda i: (i, 0)),
                  pl.BlockSpec((1, LANES), lambda i: (0, 0))],
        out_specs=[spec] * 6,
        out_shape=[jax.ShapeDtypeStruct((n, LANES), F32)] * 6,
        compiler_params=_params("parallel"),
        name="rope_tables",
    )(pos, invf)
    return out[:3], out[3:]


def _head_proj_kernel(h_ref, w_ref, gain_ref, c_ref, s1_ref, s2_ref, o_ref, r_ref,
                      *, transpose):
    half = ROT_DIM // 2
    res = jnp.dot(h_ref[...], w_ref[...], preferred_element_type=F32)
    heads = res.shape[1] // HEAD_DIM
    for hh in range(heads):
        r_ref[hh] = res[:, hh * HEAD_DIM:(hh + 1) * HEAD_DIM]

    def head(hh, carry):
        t = _rope(_rms(r_ref[hh], gain_ref[...]), c_ref[...], s1_ref[...], s2_ref[...], half)
        if transpose:
            o_ref[0, pl.ds(pl.multiple_of(hh * HEAD_DIM, HEAD_DIM), HEAD_DIM), :] = (
                t.T.astype(BF16))
        else:
            o_ref[0, hh] = t.astype(BF16)
        return carry

    lax.fori_loop(0, heads, head, 0)


def _head_proj(h, w, gain, tables, *, batch, transpose, name, tm=1024, tn=512):
    n, d = h.shape
    cols = w.shape[1]
    s = n // batch
    tm = min(tm, s)
    per_b = s // tm
    heads = tn // HEAD_DIM
    table_spec = pl.BlockSpec((tm, LANES), lambda i, j: (i, 0))
    if transpose:
        out_spec = pl.BlockSpec((1, tn, tm), lambda i, j: (i // per_b, j, i % per_b))
        out_shape = jax.ShapeDtypeStruct((batch, cols, s), BF16)
    else:
        out_spec = pl.BlockSpec((1, heads, tm, HEAD_DIM),
                                lambda i, j: (i // per_b, j, i % per_b, 0))
        out_shape = jax.ShapeDtypeStruct((batch, cols // HEAD_DIM, s, HEAD_DIM), BF16)
    return pl.pallas_call(
        functools.partial(_head_proj_kernel, transpose=transpose),
        grid=(n // tm, cols // tn),
        in_specs=[
            pl.BlockSpec((tm, d), lambda i, j: (i, 0)),
            pl.BlockSpec((d, tn), lambda i, j: (0, j)),
            pl.BlockSpec((1, HEAD_DIM), lambda i, j: (0, 0)),
            table_spec, table_spec, table_spec,
        ],
        out_specs=out_spec,
        out_shape=out_shape,
        scratch_shapes=[pltpu.VMEM((heads, tm, HEAD_DIM), F32)],
        compiler_params=_params("parallel", "arbitrary"),
        name=name,
    )(h, w, gain, *tables)


def _v_proj_kernel(h_ref, w_ref, o_ref):
    res = jnp.dot(h_ref[...], w_ref[...], preferred_element_type=F32)
    for a in range(res.shape[0] // ATT_T):
        o_ref[0, a] = res[a * ATT_T:(a + 1) * ATT_T, :].T.astype(BF16)


def _v_proj(h, w, *, batch, tm=1024, tn=512):
    n, d = h.shape
    cols = w.shape[1]
    s = n // batch
    tm = min(tm, s)
    per_b = s // tm
    return pl.pallas_call(
        _v_proj_kernel,
        grid=(n // tm, cols // tn),
        in_specs=[
            pl.BlockSpec((tm, d), lambda i, j: (i, 0)),
            pl.BlockSpec((d, tn), lambda i, j: (0, j)),
        ],
        out_specs=pl.BlockSpec((1, tm // ATT_T, tn, ATT_T),
                               lambda i, j: (i // per_b, i % per_b, j, 0)),
        out_shape=jax.ShapeDtypeStruct((batch, s // ATT_T, cols, ATT_T), BF16),
        compiler_params=_params("parallel", "arbitrary"),
        name="v_proj",
    )(h, w)


def _idx_kernel(h_ref, w_ref, kgain_ref, c_ref, s1_ref, s2_ref,
                qi_ref, kcat_ref, wi_ref, *, w_scale):
    half = IDX_ROT_DIM // 2
    qw = IDX_HEADS * IDX_DIM
    res = jnp.dot(h_ref[...], w_ref[...], preferred_element_type=F32)
    c, s1, s2 = c_ref[...], s1_ref[...], s2_ref[...]
    for t in range(qw // LANES):
        sl = slice(t * LANES, (t + 1) * LANES)
        q_t = _rope(res[:, sl], c, s1, s2, half).T
        qi_ref[0, 2 * t] = q_t[:IDX_DIM]
        qi_ref[0, 2 * t + 1] = q_t[IDX_DIM:]

    r = res[:, qw:qw + LANES]
    lane = lax.broadcasted_iota(I32, r.shape, 1)
    is_k = lane < IDX_DIM
    ms = jnp.sum(jnp.where(is_k, r * r, 0.0), axis=-1, keepdims=True) * (1.0 / IDX_DIM)
    kn = r * lax.rsqrt(ms + EPS) * kgain_ref[...]
    kr = _rope(kn, jnp.where(is_k, c, 1.0), jnp.where(is_k, s1, 0.0),
               jnp.where(is_k, s2, 0.0), half)
    hi = kr.astype(BF16).astype(F32)
    lo = kr - hi
    hi_lo = jnp.where(is_k, hi, pltpu.roll(lo, IDX_DIM, 1)).astype(BF16)
    kcat_ref[:, 0:LANES] = hi_lo
    kcat_ref[:, LANES:2 * LANES] = hi_lo
    is_w = (lane >= IDX_DIM) & (lane < IDX_DIM + IDX_HEADS)
    w_t = pltpu.roll(jnp.where(is_w, r * w_scale, 0.0), LANES - IDX_DIM, 1).T
    wi_ref[0] = w_t[:IDX_HEADS]


def _idx(h, w_idx, kgain, tables, *, batch, tm=512):
    n, d = h.shape
    cols = w_idx.shape[1]
    s = n // batch
    tm = min(tm, s)
    per_b = s // tm
    w_scale = (IDX_HEADS ** -0.5) * (IDX_DIM ** -0.5)
    return pl.pallas_call(
        functools.partial(_idx_kernel, w_scale=w_scale),
        grid=(n // tm,),
        in_specs=[
            pl.BlockSpec((tm, d), lambda i: (i, 0)),
            pl.BlockSpec((d, cols), lambda i: (0, 0)),
            pl.BlockSpec((1, LANES), lambda i: (0, 0)),
            pl.BlockSpec((tm, LANES), lambda i: (i, 0)),
            pl.BlockSpec((tm, LANES), lambda i: (i, 0)),
            pl.BlockSpec((tm, LANES), lambda i: (i, 0)),
        ],
        out_specs=[
            pl.BlockSpec((1, IDX_HEADS, IDX_DIM, tm),
                         lambda i: (i // per_b, 0, 0, i % per_b)),
            pl.BlockSpec((tm, 2 * LANES), lambda i: (i, 0)),
            pl.BlockSpec((1, IDX_HEADS, tm), lambda i: (i // per_b, 0, i % per_b)),
        ],
        out_shape=[
            jax.ShapeDtypeStruct((batch, IDX_HEADS, IDX_DIM, s), F32),
            jax.ShapeDtypeStruct((n, 2 * LANES), BF16),
            jax.ShapeDtypeStruct((batch, IDX_HEADS, s), F32),
        ],
        compiler_params=_params("parallel"),
        name="idx_proj",
    )(h, w_idx, kgain, *tables)


def _sgu_kernel(h_ref, w_ref, gv_ref, ws_ref, bs_ref, o_ref):
    tm = h_ref.shape[0]
    groups = tm // SGU_LEN
    res = jnp.dot(h_ref[...], w_ref[...], preferred_element_type=F32)
    u = jax.nn.gelu(res[:, :SGU_WIDTH])
    v = _rms(jax.nn.gelu(res[:, SGU_WIDTH:]), gv_ref[...]).astype(BF16)
    row = lax.broadcasted_iota(I32, (SGU_LEN, SGU_LEN), 0)
    col = lax.broadcasted_iota(I32, (SGU_LEN, SGU_LEN), 1)
    causal = (col // CHUNK) <= (row // CHUNK)
    for g in range(SGU_GROUPS):
        cs = slice(g * SGU_GROUP_DIM, (g + 1) * SGU_GROUP_DIM)
        wg = jnp.where(causal, ws_ref[g], 0.0).astype(BF16)
        vg = jnp.concatenate(
            [v[n * SGU_LEN:(n + 1) * SGU_LEN, cs] for n in range(groups)], axis=1)
        mixed = jnp.dot(wg, vg, preferred_element_type=F32) + bs_ref[g]
        for n in range(groups):
            rs = slice(n * SGU_LEN, (n + 1) * SGU_LEN)
            m = mixed[:, n * SGU_GROUP_DIM:(n + 1) * SGU_GROUP_DIM]
            o_ref[rs, cs] = (u[rs, cs] * m).astype(BF16)


def _sgu(h, w_sgu, gv, w_s, b_s, *, tm=512):
    n, d = h.shape
    tm = min(tm, n)
    return pl.pallas_call(
        _sgu_kernel,
        grid=(n // tm,),
        in_specs=[
            pl.BlockSpec((tm, d), lambda i: (i, 0)),
            pl.BlockSpec((d, 2 * SGU_WIDTH), lambda i: (0, 0)),
            pl.BlockSpec((1, SGU_WIDTH), lambda i: (0, 0)),
            pl.BlockSpec((SGU_GROUPS, SGU_LEN, SGU_LEN), lambda i: (0, 0, 0)),
            pl.BlockSpec((SGU_GROUPS, SGU_LEN, 1), lambda i: (0, 0, 0)),
        ],
        out_specs=pl.BlockSpec((tm, SGU_WIDTH), lambda i: (i, 0)),
        out_shape=jax.ShapeDtypeStruct((n, SGU_WIDTH), BF16),
        compiler_params=_params("parallel"),
        name="sgu_branch",
    )(h, w_sgu, gv, w_s, b_s)


def _gates_kernel(h_ref, w_ref, o_ref):
    res = jnp.dot(h_ref[...], w_ref[...], preferred_element_type=F32)
    o_ref[...] = jax.nn.sigmoid(res).astype(BF16)


def _gates(h, w_g, *, tm=1024, tn=1024):
    n, d = h.shape
    cols = w_g.shape[1]
    tm = min(tm, n)
    return pl.pallas_call(
        _gates_kernel,
        grid=(n // tm, cols // tn),
        in_specs=[
            pl.BlockSpec((tm, d), lambda i, j: (i, 0)),
            pl.BlockSpec((d, tn), lambda i, j: (0, j)),
        ],
        out_specs=pl.BlockSpec((tm, tn), lambda i, j: (i, j)),
        out_shape=jax.ShapeDtypeStruct((n, cols), BF16),
        compiler_params=_params("parallel", "arbitrary"),
        name="gates_proj",
    )(h, w_g)


def _select_kernel(qi_ref, wi_ref, kcat_ref, mask_ref, qcat_ref, s_ref, *, topk):
    t = SEL_T
    n_tiles = s_ref.shape[0]
    i = pl.program_id(1)
    n_live = i + 1

    for h in range(IDX_HEADS):
        q = qi_ref[0, h]
        hi = q.astype(BF16)
        lo = (q - hi.astype(F32)).astype(BF16)
        qcat_ref[h] = jnp.concatenate([hi, hi, lo, lo], axis=0)
    w = wi_ref[0]

    def score_tile(kt, mx, mn, diagonal):
        kc = kcat_ref[0, pl.ds(pl.multiple_of(kt * t, t), t), :]
        acc = jnp.zeros((t, t), F32)
        for h in range(IDX_HEADS):
            d = jnp.dot(kc, qcat_ref[h], preferred_element_type=F32)
            acc = acc + jnp.maximum(d, 0.0) * w[h:h + 1, :]
        if diagonal:
            admissible = ((lax.broadcasted_iota(I32, (t, t), 0) // CHUNK)
                          <= (lax.broadcasted_iota(I32, (t, t), 1) // CHUNK))
            s = jnp.where(admissible, acc, -jnp.inf)
            low = jnp.where(admissible, acc, jnp.inf)
        else:
            s = low = acc
        s_ref[kt] = s
        mx = jnp.maximum(mx, jnp.max(s, axis=0, keepdims=True))
        mn = jnp.minimum(mn, jnp.min(low, axis=0, keepdims=True))
        return mx, mn

    def score_body(j, carry):
        mx, mn = score_tile(2 * j, *carry, diagonal=False)
        return score_tile(jnp.minimum(2 * j + 1, i - 1), mx, mn, diagonal=False)

    mx, mn = lax.fori_loop(
        0, (i + 1) // 2, score_body,
        (jnp.full((1, t), -jnp.inf, F32), jnp.full((1, t), jnp.inf, F32)))
    mx, mn = score_tile(i, mx, mn, diagonal=True)

    n_adm = ((i * t + lax.broadcasted_iota(I32, (1, t), 1)) // CHUNK + 1) * CHUNK
    done0 = n_adm <= topk

    def count(c, strict=False):
        def add_tile(kt, weight, acc):
            s = s_ref[kt]
            ind = jnp.where(s > c if strict else s >= c, weight, 0.0)
            for r in range(t // COUNT_ROWS):
                acc = acc + ind[r * COUNT_ROWS:(r + 1) * COUNT_ROWS]
            return acc

        def count_body(j, acc):
            acc = add_tile(2 * j, 1.0, acc)
            second = 2 * j + 1
            return add_tile(jnp.minimum(second, n_live - 1),
                            jnp.where(second < n_live, 1.0, 0.0), acc)

        acc = lax.fori_loop(0, (n_live + 1) // 2, count_body,
                            jnp.zeros((COUNT_ROWS, t), F32))
        return jnp.sum(acc, axis=0, keepdims=True)

    def search_cond(st):
        return st[6] > 0.0

    def search_body(st):
        lo, hi, c_lo, thr, n_thr, live, _ = st
        mid = 0.5 * lo + 0.5 * hi
        stuck = (mid <= lo) | (mid >= hi)
        cnt = count(mid)
        finish = (live > 0.0) & (stuck | (cnt == topk))
        thr = jnp.where(finish, jnp.where(stuck, lo, mid), thr)
        n_thr = jnp.where(finish, jnp.where(stuck, c_lo, cnt), n_thr)
        live = jnp.where(finish, 0.0, live)
        above = cnt > topk
        lo = jnp.where(above, mid, lo)
        c_lo = jnp.where(above, cnt, c_lo)
        hi = jnp.where(cnt < topk, mid, hi)
        return lo, hi, c_lo, thr, n_thr, live, jnp.sum(live)

    hi0 = jnp.minimum(mx + jnp.maximum(jnp.abs(mx), ABOVE_MAX_FLOOR) * ABOVE_MAX_REL,
                      jnp.finfo(F32).max)
    live0 = jnp.where(done0, 0.0, 1.0)
    st0 = (mn, hi0, n_adm.astype(F32), jnp.full((1, t), -jnp.inf, F32),
           jnp.zeros((1, t), F32), live0, jnp.sum(live0))
    st = lax.while_loop(search_cond, search_body, st0)
    thr, n_thr = st[3], st[4]
    tied = n_thr > topk
    any_tied = jnp.sum(jnp.where(tied, 1.0, 0.0)) > 0.0

    @pl.when(jnp.logical_not(any_tied))
    def _():
        def mask_body(kt, carry):
            s = s_ref[kt]
            sel = (s >= thr) & (s > -jnp.inf)
            mask_ref[0, 0, kt] = jnp.where(sel, 1, 0).astype(jnp.int8)
            return carry

        lax.fori_loop(0, n_live, mask_body, 0)

    @pl.when(any_tied)
    def _():
        room = topk - count(thr, strict=True)
        tri = jnp.where(lax.broadcasted_iota(I32, (t, t), 0)
                        >= lax.broadcasted_iota(I32, (t, t), 1), 1.0, 0.0).astype(BF16)

        def tie_body(kt, seen):
            s = s_ref[kt]
            eq = (s == thr) & tied
            eq_f = jnp.where(eq, 1.0, 0.0)
            upto = jnp.dot(tri, eq_f.astype(BF16), preferred_element_type=F32)
            rank = seen + upto - eq_f
            keep = (eq & (rank < room)) | (jnp.logical_not(eq) & (s >= thr))
            sel = keep & (s > -jnp.inf)
            mask_ref[0, 0, kt] = jnp.where(sel, 1, 0).astype(jnp.int8)
            return seen + upto[t - 1:t, :]

        lax.fori_loop(0, n_live, tie_body, jnp.zeros((1, t), F32))

    def zero_body(kt, carry):
        mask_ref[0, 0, kt] = jnp.zeros((t, t), jnp.int8)
        return carry

    lax.fori_loop(n_live, n_tiles, zero_body, 0)


def _select(qi_t, wi_t, kcat, *, topk):
    b, heads, dh, s = qi_t.shape
    t = SEL_T
    nq = s // t
    return pl.pallas_call(
        functools.partial(_select_kernel, topk=topk),
        grid=(b, nq),
        in_specs=[
            pl.BlockSpec((1, heads, dh, t), lambda bi, i: (bi, 0, 0, i)),
            pl.BlockSpec((1, heads, t), lambda bi, i: (bi, 0, i)),
            pl.BlockSpec((1, s, 4 * IDX_DIM), lambda bi, i: (bi, 0, 0)),
        ],
        out_specs=pl.BlockSpec((1, 1, nq, t, t), lambda bi, i: (bi, i, 0, 0, 0)),
        out_shape=jax.ShapeDtypeStruct((b, nq, nq, t, t), jnp.int8),
        scratch_shapes=[
            pltpu.VMEM((heads, 4 * IDX_DIM, t), BF16),
            pltpu.VMEM((nq, t, t), F32),
        ],
        compiler_params=_params("parallel", "arbitrary"),
        name="index_select",
    )(qi_t, wi_t, kcat)


def _attn_kernel(qt_ref, k_ref, vt_ref, mask_ref, o_ref,
                 m_ref, acc_ref, s_ref):
    t = ATT_T
    i = pl.program_id(2)
    n = i + 1
    m_ref[...] = jnp.full(m_ref.shape, -jnp.inf, F32)
    acc_ref[...] = jnp.zeros(acc_ref.shape, F32)
    q_t = qt_ref[0]

    def logits(kt, slot):
        rows = pl.ds(pl.multiple_of(kt * t, t), t)
        for hh in range(ATT_HEADS):
            hs = slice(hh * HEAD_DIM, (hh + 1) * HEAD_DIM)
            s_ref[slot, hh] = jnp.dot(k_ref[0, hh, rows, :], q_t[hs, :],
                                      preferred_element_type=F32)

    def update(kt, slot, off):
        v_t = vt_ref[0, kt]
        sel = mask_ref[0, 0, kt].astype(I32) > off
        for hh in range(ATT_HEADS):
            hs = slice(hh * HEAD_DIM, (hh + 1) * HEAD_DIM)
            s = jnp.where(sel, s_ref[slot, hh], MASKED_LOGIT)
            m_prev = m_ref[hh]
            m_new = jnp.maximum(m_prev, jnp.max(s, axis=0, keepdims=True))
            p = jnp.exp2(s - m_new)
            alpha = jnp.exp2(m_prev - m_new)
            m_ref[hh] = m_new
            lhs = jnp.concatenate([v_t[hs, :], jnp.ones((DENOM_ROWS, t), BF16)], axis=0)
            pv = jnp.dot(lhs, p.astype(BF16), preferred_element_type=F32)
            acc_ref[hh] = alpha * acc_ref[hh] + pv

    logits(0, 0)

    def body(j, carry):
        first = 2 * j
        second = jnp.minimum(first + 1, n - 1)
        logits(second, 1)
        update(first, 0, 0)
        logits(jnp.minimum(first + 2, n - 1), 0)
        update(second, 1, jnp.where(first + 1 < n, 0, 1))
        return carry

    lax.fori_loop(0, (n + 1) // 2, body, 0)
    for hh in range(ATT_HEADS):
        acc = acc_ref[hh]
        o_ref[0, hh * HEAD_DIM:(hh + 1) * HEAD_DIM, :] = (
            acc[:HEAD_DIM] / acc[HEAD_DIM:HEAD_DIM + 1]).astype(BF16)


def _attn(q_t, k, v_t, mask):
    b, _, s, _ = k.shape
    t = ATT_T
    nq = s // t
    width = ATT_HEADS * HEAD_DIM
    return pl.pallas_call(
        _attn_kernel,
        grid=(b, N_HEADS // ATT_HEADS, nq),
        in_specs=[
            pl.BlockSpec((1, width, t), lambda bi, hp, i: (bi, hp, i)),
            pl.BlockSpec((1, ATT_HEADS, s, HEAD_DIM), lambda bi, hp, i: (bi, hp, 0, 0)),
            pl.BlockSpec((1, nq, width, t), lambda bi, hp, i: (bi, 0, hp, 0)),
            pl.BlockSpec((1, 1, nq, t, t), lambda bi, hp, i: (bi, i, 0, 0, 0)),
        ],
        out_specs=pl.BlockSpec((1, width, t), lambda bi, hp, i: (bi, hp, i)),
        out_shape=jax.ShapeDtypeStruct((b, ATT_WIDTH, s), BF16),
        scratch_shapes=[
            pltpu.VMEM((ATT_HEADS, 1, t), F32),
            pltpu.VMEM((ATT_HEADS, HEAD_DIM + DENOM_ROWS, t), F32),
            pltpu.VMEM((2, ATT_HEADS, t, t), F32),
        ],
        compiler_params=_params("parallel", "parallel", "arbitrary"),
        name="masked_attention",
    )(q_t, k, v_t, mask)


def _merge_kernel(ya_ref, yb_ref, g_ref, x_ref, wa_ref, wb_ref, wo_ref, o_ref):
    d = x_ref.shape[1]
    a = lax.dot_general(ya_ref[0], wa_ref[...], (((0,), (0,)), ((), ())),
                        preferred_element_type=F32)
    b = jnp.dot(yb_ref[...], wb_ref[...], preferred_element_type=F32)
    merged = g_ref[:, :d].astype(F32) * a + g_ref[:, d:].astype(F32) * b
    o_ref[...] = x_ref[...] + jnp.dot(merged.astype(BF16), wo_ref[...],
                                      preferred_element_type=F32)


def _merge(ya_t, yb, gates, x, wa, wb, wo, *, tm=256):
    n, d = x.shape
    batch, width, s = ya_t.shape
    tm = min(tm, s)
    per_b = s // tm
    return pl.pallas_call(
        _merge_kernel,
        grid=(n // tm,),
        in_specs=[
            pl.BlockSpec((1, width, tm), lambda i: (i // per_b, 0, i % per_b)),
            pl.BlockSpec((tm, yb.shape[1]), lambda i: (i, 0)),
            pl.BlockSpec((tm, 2 * d), lambda i: (i, 0)),
            pl.BlockSpec((tm, d), lambda i: (i, 0)),
            pl.BlockSpec(wa.shape, lambda i: (0, 0)),
            pl.BlockSpec(wb.shape, lambda i: (0, 0)),
            pl.BlockSpec(wo.shape, lambda i: (0, 0)),
        ],
        out_specs=pl.BlockSpec((tm, d), lambda i: (i, 0)),
        out_shape=jax.ShapeDtypeStruct((n, d), F32),
        compiler_params=_params("parallel"),
        name="merge_out",
    )(ya_t, yb, gates, x, wa, wb, wo)


def _inv_freq_lanes():
    def inv(rot_dim):
        f = ROPE_THETA ** (-jnp.arange(0, rot_dim, 2, dtype=F32) / rot_dim)
        return jnp.concatenate([f, f])
    lanes = jnp.concatenate([inv(ROT_DIM), inv(IDX_ROT_DIM),
                             jnp.zeros((LANES - ROT_DIM - IDX_ROT_DIM,), F32)])
    return lanes[None, :]


def _layer(x, tables_att, tables_idx, p):
    b, s, d = x.shape
    n = b * s
    topk = min(TOPK_MAX, s // 4)
    x = x.reshape(n, d)

    x1, h = _ffn(x, p["norm_ffn1"][None], p["ffn1_w_gate"].astype(BF16),
                 p["ffn1_w_up"].astype(BF16), p["ffn1_w_down"].astype(BF16),
                 p["norm_mix"][None], emit_norm=True)

    w_in = p["w_in"]
    c_qkv = 3 * ATT_WIDTH
    c_qi = c_qkv + IDX_HEADS * IDX_DIM
    c_idx = c_qi + IDX_DIM + IDX_HEADS
    c_sgu = c_idx + 2 * SGU_WIDTH
    w_q = w_in[:, :ATT_WIDTH].astype(BF16)
    w_k = w_in[:, ATT_WIDTH:2 * ATT_WIDTH].astype(BF16)
    w_v = w_in[:, 2 * ATT_WIDTH:c_qkv].astype(BF16)
    w_idx = jnp.pad(w_in[:, c_qkv:c_idx],
                    ((0, 0), (0, LANES - IDX_DIM - IDX_HEADS))).astype(BF16)
    w_sgu = w_in[:, c_idx:c_sgu].astype(BF16)
    w_gate = w_in[:, c_sgu:].astype(BF16)

    q_gain = (p["q_norm"] * (HEAD_DIM ** -0.5 * LOG2_E))[None]
    q_t = _head_proj(h, w_q, q_gain, tables_att, batch=b, transpose=True, name="q_proj")
    k = _head_proj(h, w_k, p["k_norm"][None], tables_att, batch=b, transpose=False,
                   name="k_proj")
    v_t = _v_proj(h, w_v, batch=b)
    kgain = jnp.pad(p["idx_k_norm"], (0, LANES - IDX_DIM))[None]
    qi_t, kcat, wi_t = _idx(h, w_idx, kgain, tables_idx, batch=b)
    yb = _sgu(h, w_sgu, p["sgu_v_norm"][None], p["sgu_w_s"], p["sgu_b_s"][:, :, None])
    gates = _gates(h, w_gate)

    mask = _select(qi_t, wi_t, kcat.reshape(b, s, 4 * IDX_DIM), topk=topk)
    ya_t = _attn(q_t, k, v_t, mask)

    x2 = _merge(ya_t, yb, gates, x1, p["w_up_attn"].astype(BF16),
                p["w_up_sgu"].astype(BF16), p["w_out"].astype(BF16))
    out = _ffn(x2, p["norm_ffn2"][None], p["ffn2_w_gate"].astype(BF16),
               p["ffn2_w_up"].astype(BF16), p["ffn2_w_down"].astype(BF16),
               p["norm_ffn2"][None], emit_norm=False)
    return out.reshape(b, s, d)


def kernel(x, positions, norm_ffn1, ffn1_w_gate, ffn1_w_up, ffn1_w_down, norm_mix, w_in,
           q_norm, k_norm, idx_k_norm, sgu_v_norm, sgu_w_s, sgu_b_s, w_up_attn, w_up_sgu,
           w_out, norm_ffn2, ffn2_w_gate, ffn2_w_up, ffn2_w_down):
    params = dict(
        norm_ffn1=norm_ffn1, ffn1_w_gate=ffn1_w_gate, ffn1_w_up=ffn1_w_up,
        ffn1_w_down=ffn1_w_down, norm_mix=norm_mix, w_in=w_in, q_norm=q_norm,
        k_norm=k_norm, idx_k_norm=idx_k_norm, sgu_v_norm=sgu_v_norm, sgu_w_s=sgu_w_s,
        sgu_b_s=sgu_b_s, w_up_attn=w_up_attn, w_up_sgu=w_up_sgu, w_out=w_out,
        norm_ffn2=norm_ffn2, ffn2_w_gate=ffn2_w_gate, ffn2_w_up=ffn2_w_up,
        ffn2_w_down=ffn2_w_down)
    pos = positions.reshape(-1, 1).astype(I32)
    tables_att, tables_idx = _rope_table(pos, _inv_freq_lanes())
    for l in range(w_in.shape[0]):
        x = _layer(x, tables_att, tables_idx, {k: v[l] for k, v in params.items()})
    return x
```

```python
import functools

import jax
import jax.numpy as jnp
from jax import lax
from jax.experimental import pallas as pl
from jax.experimental.pallas import tpu as pltpu

F32 = jnp.float32
BF16 = jnp.bfloat16
I32 = jnp.int32

EPS = 1e-6
CHUNK = 64
N_HEADS = 8
HEAD_DIM = 128
ATT_WIDTH = N_HEADS * HEAD_DIM
ROPE_THETA = 500000.0
ROT_DIM = HEAD_DIM // 4
IDX_HEADS = 16
IDX_DIM = 64
IDX_ROT_DIM = IDX_DIM // 4
TOPK_MAX = 256
SGU_LEN = 128
SGU_GROUPS = 8
SGU_GROUP_DIM = 128
SGU_WIDTH = SGU_GROUPS * SGU_GROUP_DIM

LANES = 128
MXU_DIM = 256
VMEM_LIMIT_BYTES = 56 * 1024 * 1024

SEL_T = MXU_DIM
COUNT_ROWS = 16
UNCHECKED_PASSES = 14
ATT_T = 256
ATT_HEADS = 4
DENOM_ROWS = 16
ABOVE_MAX_REL = 1e-6
ABOVE_MAX_FLOOR = 1e-30
MASKED_LOGIT = -1e30
LOG2_E = 1.4426950408889634


def _params(*sem):
    return pltpu.CompilerParams(dimension_semantics=sem,
                                vmem_limit_bytes=VMEM_LIMIT_BYTES)


def _rms(t, gain):
    ms = jnp.mean(t * t, axis=-1, keepdims=True)
    return t * lax.rsqrt(ms + EPS) * gain


def _rope(t, c, s1, s2, half):
    return (t * c + pltpu.roll(t, LANES - half, 1) * s1
            + pltpu.roll(t, half, 1) * s2)


def _ffn_kernel(x_ref, g_ref, wg_ref, wu_ref, wd_ref, g2_ref, o_ref, *rest,
                emit_norm):
    if emit_norm:
        h_out_ref, hn_ref = rest
    else:
        (hn_ref,) = rest
    j = pl.program_id(1)

    @pl.when(j == 0)
    def _():
        hn_ref[...] = _rms(x_ref[...], g_ref[...]).astype(BF16)
        o_ref[...] = jnp.zeros_like(o_ref)

    h = hn_ref[...]
    a = jnp.dot(h, wg_ref[...], preferred_element_type=F32)
    b = jnp.dot(h, wu_ref[...], preferred_element_type=F32)
    act = (a * jax.nn.sigmoid(a) * b).astype(BF16)
    o_ref[...] += jnp.dot(act, wd_ref[...], preferred_element_type=F32)

    @pl.when(j == pl.num_programs(1) - 1)
    def _():
        y = x_ref[...] + 0.5 * o_ref[...]
        o_ref[...] = y
        if emit_norm:
            h_out_ref[...] = _rms(y, g2_ref[...]).astype(BF16)


def _ffn(x, gain, w_gate, w_up, w_down, gain2, *, emit_norm, tm=512, tf=512):
    n, d = x.shape
    f = w_gate.shape[1]
    tm = min(tm, n)
    out_shape = [jax.ShapeDtypeStruct((n, d), F32)]
    out_specs = [pl.BlockSpec((tm, d), lambda i, j: (i, 0))]
    if emit_norm:
        out_shape.append(jax.ShapeDtypeStruct((n, d), BF16))
        out_specs.append(pl.BlockSpec((tm, d), lambda i, j: (i, 0)))
    res = pl.pallas_call(
        functools.partial(_ffn_kernel, emit_norm=emit_norm),
        grid=(n // tm, f // tf),
        in_specs=[
            pl.BlockSpec((tm, d), lambda i, j: (i, 0)),
            pl.BlockSpec((1, d), lambda i, j: (0, 0)),
            pl.BlockSpec((d, tf), lambda i, j: (0, j)),
            pl.BlockSpec((d, tf), lambda i, j: (0, j)),
            pl.BlockSpec((tf, d), lambda i, j: (j, 0)),
            pl.BlockSpec((1, d), lambda i, j: (0, 0)),
        ],
        out_specs=out_specs,
        out_shape=out_shape,
        scratch_shapes=[pltpu.VMEM((tm, d), BF16)],
        compiler_params=_params("parallel", "arbitrary"),
        name="ffn_norm" if emit_norm else "ffn",
    )(x, gain, w_gate, w_up, w_down, gain2)
    return res if emit_norm else res[0]


def _rope_table_kernel(pos_ref, invf_ref, ca_ref, s1a_ref, s2a_ref, ci_ref, s1i_ref, s2i_ref):
    ang = pos_ref[...].astype(F32) * invf_ref[...]
    c = jnp.cos(ang)
    s = jnp.sin(ang)
    lane = lax.broadcasted_iota(I32, ang.shape, 1)
    in_att = lane < ROT_DIM
    ca_ref[...] = jnp.where(in_att, c, 1.0)
    s1a_ref[...] = jnp.where(lane < ROT_DIM // 2, -s, 0.0)
    s2a_ref[...] = jnp.where((lane >= ROT_DIM // 2) & in_att, s, 0.0)
    first = lane < IDX_DIM
    c_i = jnp.where(first, pltpu.roll(c, LANES - ROT_DIM, 1), pltpu.roll(c, IDX_DIM - ROT_DIM, 1))
    s_i = jnp.where(first, pltpu.roll(s, LANES - ROT_DIM, 1), pltpu.roll(s, IDX_DIM - ROT_DIM, 1))
    j = lane & (IDX_DIM - 1)
    in_idx = j < IDX_ROT_DIM
    ci_ref[...] = jnp.where(in_idx, c_i, 1.0)
    s1i_ref[...] = jnp.where(j < IDX_ROT_DIM // 2, -s_i, 0.0)
    s2i_ref[...] = jnp.where((j >= IDX_ROT_DIM // 2) & in_idx, s_i, 0.0)


def _rope_table(pos, invf, *, tm=2048):
    n = pos.shape[0]
    tm = min(tm, n)
    spec = pl.BlockSpec((tm, LANES), lambda i: (i, 0))
    out = pl.pallas_call(
        _rope_table_kernel,
        grid=(n // tm,),
        in_specs=[pl.BlockSpec((tm, 1), lambda i: (i, 0)),
                  pl.BlockSpec((1, LANES), lambda i: (0, 0))],
        out_specs=[spec] * 6,
        out_shape=[jax.ShapeDtypeStruct((n, LANES), F32)] * 6,
        compiler_params=_params("parallel"),
        name="rope_tables",
    )(pos, invf)
    return out[:3], out[3:]


def _head_proj_kernel(h_ref, w_ref, gain_ref, c_ref, s1_ref, s2_ref, o_ref, r_ref,
                      *, transpose):
    half = ROT_DIM // 2
    res = jnp.dot(h_ref[...], w_ref[...], preferred_element_type=F32)
    heads = res.shape[1] // HEAD_DIM
    for hh in range(heads):
        r_ref[hh] = res[:, hh * HEAD_DIM:(hh + 1) * HEAD_DIM]

    def head(hh, carry):
        t = _rope(_rms(r_ref[hh], gain_ref[...]), c_ref[...], s1_ref[...], s2_ref[...], half)
        if transpose:
            o_ref[0, pl.ds(pl.multiple_of(hh * HEAD_DIM, HEAD_DIM), HEAD_DIM), :] = (
                t.T.astype(BF16))
        else:
            o_ref[0, hh] = t.astype(BF16)
        return carry

    lax.fori_loop(0, heads, head, 0)


def _head_proj(h, w, gain, tables, *, batch, transpose, name, tm=1024, tn=512):
    n, d = h.shape
    cols = w.shape[1]
    s = n // batch
    tm = min(tm, s)
    per_b = s // tm
    heads = tn // HEAD_DIM
    table_spec = pl.BlockSpec((tm, LANES), lambda i, j: (i, 0))
    if transpose:
        out_spec = pl.BlockSpec((1, tn, tm), lambda i, j: (i // per_b, j, i % per_b))
        out_shape = jax.ShapeDtypeStruct((batch, cols, s), BF16)
    else:
        out_spec = pl.BlockSpec((1, heads, tm, HEAD_DIM),
                                lambda i, j: (i // per_b, j, i % per_b, 0))
        out_shape = jax.ShapeDtypeStruct((batch, cols // HEAD_DIM, s, HEAD_DIM), BF16)
    return pl.pallas_call(
        functools.partial(_head_proj_kernel, transpose=transpose),
        grid=(n // tm, cols // tn),
        in_specs=[
            pl.BlockSpec((tm, d), lambda i, j: (i, 0)),
            pl.BlockSpec((d, tn), lambda i, j: (0, j)),
            pl.BlockSpec((1, HEAD_DIM), lambda i, j: (0, 0)),
            table_spec, table_spec, table_spec,
        ],
        out_specs=out_spec,
        out_shape=out_shape,
        scratch_shapes=[pltpu.VMEM((heads, tm, HEAD_DIM), F32)],
        compiler_params=_params("parallel", "arbitrary"),
        name=name,
    )(h, w, gain, *tables)


def _v_proj_kernel(h_ref, w_ref, o_ref):
    res = jnp.dot(h_ref[...], w_ref[...], preferred_element_type=F32)
    for a in range(res.shape[0] // ATT_T):
        o_ref[0, a] = res[a * ATT_T:(a + 1) * ATT_T, :].T.astype(BF16)


def _v_proj(h, w, *, batch, tm=1024, tn=512):
    n, d = h.shape
    cols = w.shape[1]
    s = n // batch
    tm = min(tm, s)
    per_b = s // tm
    return pl.pallas_call(
        _v_proj_kernel,
        grid=(n // tm, cols // tn),
        in_specs=[
            pl.BlockSpec((tm, d), lambda i, j: (i, 0)),
            pl.BlockSpec((d, tn), lambda i, j: (0, j)),
        ],
        out_specs=pl.BlockSpec((1, tm // ATT_T, tn, ATT_T),
                               lambda i, j: (i // per_b, i % per_b, j, 0)),
        out_shape=jax.ShapeDtypeStruct((batch, s // ATT_T, cols, ATT_T), BF16),
        compiler_params=_params("parallel", "arbitrary"),
        name="v_proj",
    )(h, w)


def _idx_kernel(h_ref, w_ref, kgain_ref, c_ref, s1_ref, s2_ref,
                qi_ref, kcat_ref, wi_ref, *, w_scale):
    half = IDX_ROT_DIM // 2
    qw = IDX_HEADS * IDX_DIM
    res = jnp.dot(h_ref[...], w_ref[...], preferred_element_type=F32)
    c, s1, s2 = c_ref[...], s1_ref[...], s2_ref[...]
    for t in range(qw // LANES):
        sl = slice(t * LANES, (t + 1) * LANES)
        q_t = _rope(res[:, sl], c, s1, s2, half).T
        qi_ref[0, 2 * t] = q_t[:IDX_DIM]
        qi_ref[0, 2 * t + 1] = q_t[IDX_DIM:]

    r = res[:, qw:qw + LANES]
    lane = lax.broadcasted_iota(I32, r.shape, 1)
    is_k = lane < IDX_DIM
    ms = jnp.sum(jnp.where(is_k, r * r, 0.0), axis=-1, keepdims=True) * (1.0 / IDX_DIM)
    kn = r * lax.rsqrt(ms + EPS) * kgain_ref[...]
    kr = _rope(kn, jnp.where(is_k, c, 1.0), jnp.where(is_k, s1, 0.0),
               jnp.where(is_k, s2, 0.0), half)
    hi = kr.astype(BF16).astype(F32)
    lo = kr - hi
    hi_lo = jnp.where(is_k, hi, pltpu.roll(lo, IDX_DIM, 1)).astype(BF16)
    kcat_ref[:, 0:LANES] = hi_lo
    kcat_ref[:, LANES:2 * LANES] = hi_lo
    is_w = (lane >= IDX_DIM) & (lane < IDX_DIM + IDX_HEADS)
    w_t = pltpu.roll(jnp.where(is_w, r * w_scale, 0.0), LANES - IDX_DIM, 1).T
    wi_ref[0] = w_t[:IDX_HEADS]


def _idx(h, w_idx, kgain, tables, *, batch, tm=512):
    n, d = h.shape
    cols = w_idx.shape[1]
    s = n // batch
    tm = min(tm, s)
    per_b = s // tm
    w_scale = (IDX_HEADS ** -0.5) * (IDX_DIM ** -0.5)
    return pl.pallas_call(
        functools.partial(_idx_kernel, w_scale=w_scale),
        grid=(n // tm,),
        in_specs=[
            pl.BlockSpec((tm, d), lambda i: (i, 0)),
            pl.BlockSpec((d, cols), lambda i: (0, 0)),
            pl.BlockSpec((1, LANES), lambda i: (0, 0)),
            pl.BlockSpec((tm, LANES), lambda i: (i, 0)),
            pl.BlockSpec((tm, LANES), lambda i: (i, 0)),
            pl.BlockSpec((tm, LANES), lambda i: (i, 0)),
        ],
        out_specs=[
            pl.BlockSpec((1, IDX_HEADS, IDX_DIM, tm),
                         lambda i: (i // per_b, 0, 0, i % per_b)),
            pl.BlockSpec((tm, 2 * LANES), lambda i: (i, 0)),
            pl.BlockSpec((1, IDX_HEADS, tm), lambda i: (i // per_b, 0, i % per_b)),
        ],
        out_shape=[
            jax.ShapeDtypeStruct((batch, IDX_HEADS, IDX_DIM, s), F32),
            jax.ShapeDtypeStruct((n, 2 * LANES), BF16),
            jax.ShapeDtypeStruct((batch, IDX_HEADS, s), F32),
        ],
        compiler_params=_params("parallel"),
        name="idx_proj",
    )(h, w_idx, kgain, *tables)


def _sgu_kernel(h_ref, w_ref, gv_ref, ws_ref, bs_ref, o_ref):
    tm = h_ref.shape[0]
    groups = tm // SGU_LEN
    res = jnp.dot(h_ref[...], w_ref[...], preferred_element_type=F32)
    u = jax.nn.gelu(res[:, :SGU_WIDTH])
    v = _rms(jax.nn.gelu(res[:, SGU_WIDTH:]), gv_ref[...]).astype(BF16)
    row = lax.broadcasted_iota(I32, (SGU_LEN, SGU_LEN), 0)
    col = lax.broadcasted_iota(I32, (SGU_LEN, SGU_LEN), 1)
    causal = (col // CHUNK) <= (row // CHUNK)
    for g in range(SGU_GROUPS):
        cs = slice(g * SGU_GROUP_DIM, (g + 1) * SGU_GROUP_DIM)
        wg = jnp.where(causal, ws_ref[g], 0.0).astype(BF16)
        vg = jnp.concatenate(
            [v[n * SGU_LEN:(n + 1) * SGU_LEN, cs] for n in range(groups)], axis=1)
        mixed = jnp.dot(wg, vg, preferred_element_type=F32) + bs_ref[g]
        for n in range(groups):
            rs = slice(n * SGU_LEN, (n + 1) * SGU_LEN)
            m = mixed[:, n * SGU_GROUP_DIM:(n + 1) * SGU_GROUP_DIM]
            o_ref[rs, cs] = (u[rs, cs] * m).astype(BF16)


def _sgu(h, w_sgu, gv, w_s, b_s, *, tm=512):
    n, d = h.shape
    tm = min(tm, n)
    return pl.pallas_call(
        _sgu_kernel,
        grid=(n // tm,),
        in_specs=[
            pl.BlockSpec((tm, d), lambda i: (i, 0)),
            pl.BlockSpec((d, 2 * SGU_WIDTH), lambda i: (0, 0)),
            pl.BlockSpec((1, SGU_WIDTH), lambda i: (0, 0)),
            pl.BlockSpec((SGU_GROUPS, SGU_LEN, SGU_LEN), lambda i: (0, 0, 0)),
            pl.BlockSpec((SGU_GROUPS, SGU_LEN, 1), lambda i: (0, 0, 0)),
        ],
        out_specs=pl.BlockSpec((tm, SGU_WIDTH), lambda i: (i, 0)),
        out_shape=jax.ShapeDtypeStruct((n, SGU_WIDTH), BF16),
        compiler_params=_params("parallel"),
        name="sgu_branch",
    )(h, w_sgu, gv, w_s, b_s)


def _gates_kernel(h_ref, w_ref, o_ref):
    res = jnp.dot(h_ref[...], w_ref[...], preferred_element_type=F32)
    o_ref[...] = jax.nn.sigmoid(res).astype(BF16)


def _gates(h, w_g, *, tm=1024, tn=1024):
    n, d = h.shape
    cols = w_g.shape[1]
    tm = min(tm, n)
    return pl.pallas_call(
        _gates_kernel,
        grid=(n // tm, cols // tn),
        in_specs=[
            pl.BlockSpec((tm, d), lambda i, j: (i, 0)),
            pl.BlockSpec((d, tn), lambda i, j: (0, j)),
        ],
        out_specs=pl.BlockSpec((tm, tn), lambda i, j: (i, j)),
        out_shape=jax.ShapeDtypeStruct((n, cols), BF16),
        compiler_params=_params("parallel", "arbitrary"),
        name="gates_proj",
    )(h, w_g)


def _select_kernel(qi_ref, wi_ref, kcat_ref, mask_ref, qcat_ref, s_ref, *, topk):
    t = SEL_T
    n_tiles = s_ref.shape[0]
    i = pl.program_id(1)
    n_live = i + 1

    for h in range(IDX_HEADS):
        q = qi_ref[0, h]
        hi = q.astype(BF16)
        lo = (q - hi.astype(F32)).astype(BF16)
        qcat_ref[h] = jnp.concatenate([hi, hi, lo, lo], axis=0)
    w = wi_ref[0]

    def score_tile(kt, mx, mn, diagonal):
        kc = kcat_ref[0, pl.ds(pl.multiple_of(kt * t, t), t), :]
        acc = jnp.zeros((t, t), F32)
        for h in range(IDX_HEADS):
            d = jnp.dot(kc, qcat_ref[h], preferred_element_type=F32)
            acc = acc + jnp.maximum(d, 0.0) * w[h:h + 1, :]
        if diagonal:
            admissible = ((lax.broadcasted_iota(I32, (t, t), 0) // CHUNK)
                          <= (lax.broadcasted_iota(I32, (t, t), 1) // CHUNK))
            s = jnp.where(admissible, acc, -jnp.inf)
            low = jnp.where(admissible, acc, jnp.inf)
        else:
            s = low = acc
        s_ref[kt] = s
        mx = jnp.maximum(mx, jnp.max(s, axis=0, keepdims=True))
        mn = jnp.minimum(mn, jnp.min(low, axis=0, keepdims=True))
        return mx, mn

    def score_body(j, carry):
        mx, mn = score_tile(2 * j, *carry, diagonal=False)
        return score_tile(jnp.minimum(2 * j + 1, i - 1), mx, mn, diagonal=False)

    mx, mn = lax.fori_loop(
        0, (i + 1) // 2, score_body,
        (jnp.full((1, t), -jnp.inf, F32), jnp.full((1, t), jnp.inf, F32)))
    mx, mn = score_tile(i, mx, mn, diagonal=True)

    n_adm = ((i * t + lax.broadcasted_iota(I32, (1, t), 1)) // CHUNK + 1) * CHUNK
    done0 = n_adm <= topk

    def count(c, strict=False):
        def add_tile(kt, weight, acc):
            s = s_ref[kt]
            ind = jnp.where(s > c if strict else s >= c, weight, 0.0)
            for r in range(t // COUNT_ROWS):
                acc = acc + ind[r * COUNT_ROWS:(r + 1) * COUNT_ROWS]
            return acc

        def count_body(j, acc):
            acc = add_tile(2 * j, 1.0, acc)
            second = 2 * j + 1
            return add_tile(jnp.minimum(second, n_live - 1),
                            jnp.where(second < n_live, 1.0, 0.0), acc)

        acc = lax.fori_loop(0, (n_live + 1) // 2, count_body,
                            jnp.zeros((COUNT_ROWS, t), F32))
        return jnp.sum(acc, axis=0, keepdims=True)

    def search_step(st):
        lo, hi, c_lo, thr, n_thr, live = st
        mid = 0.5 * lo + 0.5 * hi
        stuck = (mid <= lo) | (mid >= hi)
        cnt = count(mid)
        finish = (live > 0.0) & (stuck | (cnt == topk))
        thr = jnp.where(finish, jnp.where(stuck, lo, mid), thr)
        n_thr = jnp.where(finish, jnp.where(stuck, c_lo, cnt), n_thr)
        live = jnp.where(finish, 0.0, live)
        above = cnt > topk
        lo = jnp.where(above, mid, lo)
        c_lo = jnp.where(above, cnt, c_lo)
        hi = jnp.where(cnt < topk, mid, hi)
        return lo, hi, c_lo, thr, n_thr, live

    def search_body(st):
        st = search_step(st[:-1])
        return st + (jnp.sum(st[-1]),)

    hi0 = jnp.minimum(mx + jnp.maximum(jnp.abs(mx), ABOVE_MAX_FLOOR) * ABOVE_MAX_REL,
                      jnp.finfo(F32).max)
    live0 = jnp.where(done0, 0.0, 1.0)
    st = (mn, hi0, n_adm.astype(F32), jnp.full((1, t), -jnp.inf, F32),
          jnp.zeros((1, t), F32), live0)
    st = lax.fori_loop(0, UNCHECKED_PASSES, lambda _, s_: search_step(s_), st)
    st = lax.while_loop(lambda s_: s_[-1] > 0.0, search_body, st + (jnp.sum(st[-1]),))
    thr, n_thr = st[3], st[4]
    tied = n_thr > topk
    any_tied = jnp.sum(jnp.where(tied, 1.0, 0.0)) > 0.0

    @pl.when(jnp.logical_not(any_tied))
    def _():
        def mask_body(kt, carry):
            s = s_ref[kt]
            sel = (s >= thr) & (s > -jnp.inf)
            mask_ref[0, 0, kt] = jnp.where(sel, 1, 0).astype(jnp.int8)
            return carry

        lax.fori_loop(0, n_live, mask_body, 0)

    @pl.when(any_tied)
    def _():
        room = topk - count(thr, strict=True)
        tri = jnp.where(lax.broadcasted_iota(I32, (t, t), 0)
                        >= lax.broadcasted_iota(I32, (t, t), 1), 1.0, 0.0).astype(BF16)

        def tie_body(kt, seen):
            s = s_ref[kt]
            eq = (s == thr) & tied
            eq_f = jnp.where(eq, 1.0, 0.0)
            upto = jnp.dot(tri, eq_f.astype(BF16), preferred_element_type=F32)
            rank = seen + upto - eq_f
            keep = (eq & (rank < room)) | (jnp.logical_not(eq) & (s >= thr))
            sel = keep & (s > -jnp.inf)
            mask_ref[0, 0, kt] = jnp.where(sel, 1, 0).astype(jnp.int8)
            return seen + upto[t - 1:t, :]

        lax.fori_loop(0, n_live, tie_body, jnp.zeros((1, t), F32))

    def zero_body(kt, carry):
        mask_ref[0, 0, kt] = jnp.zeros((t, t), jnp.int8)
        return carry

    lax.fori_loop(n_live, n_tiles, zero_body, 0)


def _select(qi_t, wi_t, kcat, *, topk):
    b, heads, dh, s = qi_t.shape
    t = SEL_T
    nq = s // t
    return pl.pallas_call(
        functools.partial(_select_kernel, topk=topk),
        grid=(b, nq),
        in_specs=[
            pl.BlockSpec((1, heads, dh, t), lambda bi, i: (bi, 0, 0, i)),
            pl.BlockSpec((1, heads, t), lambda bi, i: (bi, 0, i)),
            pl.BlockSpec((1, s, 4 * IDX_DIM), lambda bi, i: (bi, 0, 0)),
        ],
        out_specs=pl.BlockSpec((1, 1, nq, t, t), lambda bi, i: (bi, i, 0, 0, 0)),
        out_shape=jax.ShapeDtypeStruct((b, nq, nq, t, t), jnp.int8),
        scratch_shapes=[
            pltpu.VMEM((heads, 4 * IDX_DIM, t), BF16),
            pltpu.VMEM((nq, t, t), F32),
        ],
        compiler_params=_params("parallel", "arbitrary"),
        name="index_select",
    )(qi_t, wi_t, kcat)


def _attn_kernel(qt_ref, k_ref, vt_ref, mask_ref, o_ref,
                 m_ref, acc_ref, s_ref):
    t = ATT_T
    i = pl.program_id(2)
    n = i + 1
    m_ref[...] = jnp.full(m_ref.shape, -jnp.inf, F32)
    acc_ref[...] = jnp.zeros(acc_ref.shape, F32)
    q_t = qt_ref[0]

    def logits(kt, slot):
        rows = pl.ds(pl.multiple_of(kt * t, t), t)
        for hh in range(ATT_HEADS):
            hs = slice(hh * HEAD_DIM, (hh + 1) * HEAD_DIM)
            s_ref[slot, hh] = jnp.dot(k_ref[0, hh, rows, :], q_t[hs, :],
                                      preferred_element_type=F32)

    def update(kt, slot, off):
        v_t = vt_ref[0, kt]
        sel = mask_ref[0, 0, kt].astype(I32) > off
        for hh in range(ATT_HEADS):
            hs = slice(hh * HEAD_DIM, (hh + 1) * HEAD_DIM)
            s = jnp.where(sel, s_ref[slot, hh], MASKED_LOGIT)
            m_prev = m_ref[hh]
            m_new = jnp.maximum(m_prev, jnp.max(s, axis=0, keepdims=True))
            p = jnp.exp2(s - m_new)
            alpha = jnp.exp2(m_prev - m_new)
            m_ref[hh] = m_new
            lhs = jnp.concatenate([v_t[hs, :], jnp.ones((DENOM_ROWS, t), BF16)], axis=0)
            pv = jnp.dot(lhs, p.astype(BF16), preferred_element_type=F32)
            acc_ref[hh] = alpha * acc_ref[hh] + pv

    logits(0, 0)

    def body(j, carry):
        first = 2 * j
        second = jnp.minimum(first + 1, n - 1)
        logits(second, 1)
        update(first, 0, 0)
        logits(jnp.minimum(first + 2, n - 1), 0)
        update(second, 1, jnp.where(first + 1 < n, 0, 1))
        return carry

    lax.fori_loop(0, (n + 1) // 2, body, 0)
    for hh in range(ATT_HEADS):
        acc = acc_ref[hh]
        o_ref[0, hh * HEAD_DIM:(hh + 1) * HEAD_DIM, :] = (
            acc[:HEAD_DIM] / acc[HEAD_DIM:HEAD_DIM + 1]).astype(BF16)


def _attn(q_t, k, v_t, mask):
    b, _, s, _ = k.shape
    t = ATT_T
    nq = s // t
    width = ATT_HEADS * HEAD_DIM
    return pl.pallas_call(
        _attn_kernel,
        grid=(b, N_HEADS // ATT_HEADS, nq),
        in_specs=[
            pl.BlockSpec((1, width, t), lambda bi, hp, i: (bi, hp, i)),
            pl.BlockSpec((1, ATT_HEADS, s, HEAD_DIM), lambda bi, hp, i: (bi, hp, 0, 0)),
            pl.BlockSpec((1, nq, width, t), lambda bi, hp, i: (bi, 0, hp, 0)),
            pl.BlockSpec((1, 1, nq, t, t), lambda bi, hp, i: (bi, i, 0, 0, 0)),
        ],
        out_specs=pl.BlockSpec((1, width, t), lambda bi, hp, i: (bi, hp, i)),
        out_shape=jax.ShapeDtypeStruct((b, ATT_WIDTH, s), BF16),
        scratch_shapes=[
            pltpu.VMEM((ATT_HEADS, 1, t), F32),
            pltpu.VMEM((ATT_HEADS, HEAD_DIM + DENOM_ROWS, t), F32),
            pltpu.VMEM((2, ATT_HEADS, t, t), F32),
        ],
        compiler_params=_params("parallel", "parallel", "arbitrary"),
        name="masked_attention",
    )(q_t, k, v_t, mask)


def _merge_kernel(ya_ref, yb_ref, g_ref, x_ref, wa_ref, wb_ref, wo_ref, o_ref):
    d = x_ref.shape[1]
    a = lax.dot_general(ya_ref[0], wa_ref[...], (((0,), (0,)), ((), ())),
                        preferred_element_type=F32)
    b = jnp.dot(yb_ref[...], wb_ref[...], preferred_element_type=F32)
    merged = g_ref[:, :d].astype(F32) * a + g_ref[:, d:].astype(F32) * b
    o_ref[...] = x_ref[...] + jnp.dot(merged.astype(BF16), wo_ref[...],
                                      preferred_element_type=F32)


def _merge(ya_t, yb, gates, x, wa, wb, wo, *, tm=256):
    n, d = x.shape
    batch, width, s = ya_t.shape
    tm = min(tm, s)
    per_b = s // tm
    return pl.pallas_call(
        _merge_kernel,
        grid=(n // tm,),
        in_specs=[
            pl.BlockSpec((1, width, tm), lambda i: (i // per_b, 0, i % per_b)),
            pl.BlockSpec((tm, yb.shape[1]), lambda i: (i, 0)),
            pl.BlockSpec((tm, 2 * d), lambda i: (i, 0)),
            pl.BlockSpec((tm, d), lambda i: (i, 0)),
            pl.BlockSpec(wa.shape, lambda i: (0, 0)),
            pl.BlockSpec(wb.shape, lambda i: (0, 0)),
            pl.BlockSpec(wo.shape, lambda i: (0, 0)),
        ],
        out_specs=pl.BlockSpec((tm, d), lambda i: (i, 0)),
        out_shape=jax.ShapeDtypeStruct((n, d), F32),
        compiler_params=_params("parallel"),
        name="merge_out",
    )(ya_t, yb, gates, x, wa, wb, wo)


def _inv_freq_lanes():
    def inv(rot_dim):
        f = ROPE_THETA ** (-jnp.arange(0, rot_dim, 2, dtype=F32) / rot_dim)
        return jnp.concatenate([f, f])
    lanes = jnp.concatenate([inv(ROT_DIM), inv(IDX_ROT_DIM),
                             jnp.zeros((LANES - ROT_DIM - IDX_ROT_DIM,), F32)])
    return lanes[None, :]


def _layer(x, tables_att, tables_idx, p):
    b, s, d = x.shape
    n = b * s
    topk = min(TOPK_MAX, s // 4)
    x = x.reshape(n, d)

    x1, h = _ffn(x, p["norm_ffn1"][None], p["ffn1_w_gate"].astype(BF16),
                 p["ffn1_w_up"].astype(BF16), p["ffn1_w_down"].astype(BF16),
                 p["norm_mix"][None], emit_norm=True)

    w_in = p["w_in"]
    c_qkv = 3 * ATT_WIDTH
    c_qi = c_qkv + IDX_HEADS * IDX_DIM
    c_idx = c_qi + IDX_DIM + IDX_HEADS
    c_sgu = c_idx + 2 * SGU_WIDTH
    w_q = w_in[:, :ATT_WIDTH].astype(BF16)
    w_k = w_in[:, ATT_WIDTH:2 * ATT_WIDTH].astype(BF16)
    w_v = w_in[:, 2 * ATT_WIDTH:c_qkv].astype(BF16)
    w_idx = jnp.pad(w_in[:, c_qkv:c_idx],
                    ((0, 0), (0, LANES - IDX_DIM - IDX_HEADS))).astype(BF16)
    w_sgu = w_in[:, c_idx:c_sgu].astype(BF16)
    w_gate = w_in[:, c_sgu:].astype(BF16)

    q_gain = (p["q_norm"] * (HEAD_DIM ** -0.5 * LOG2_E))[None]
    q_t = _head_proj(h, w_q, q_gain, tables_att, batch=b, transpose=True, name="q_proj")
    k = _head_proj(h, w_k, p["k_norm"][None], tables_att, batch=b, transpose=False,
                   name="k_proj")
    v_t = _v_proj(h, w_v, batch=b)
    kgain = jnp.pad(p["idx_k_norm"], (0, LANES - IDX_DIM))[None]
    qi_t, kcat, wi_t = _idx(h, w_idx, kgain, tables_idx, batch=b)
    yb = _sgu(h, w_sgu, p["sgu_v_norm"][None], p["sgu_w_s"], p["sgu_b_s"][:, :, None])
    gates = _gates(h, w_gate)

    mask = _select(qi_t, wi_t, kcat.reshape(b, s, 4 * IDX_DIM), topk=topk)
    ya_t = _attn(q_t, k, v_t, mask)

    x2 = _merge(ya_t, yb, gates, x1, p["w_up_attn"].astype(BF16),
                p["w_up_sgu"].astype(BF16), p["w_out"].astype(BF16))
    out = _ffn(x2, p["norm_ffn2"][None], p["ffn2_w_gate"].astype(BF16),
               p["ffn2_w_up"].astype(BF16), p["ffn2_w_down"].astype(BF16),
               p["norm_ffn2"][None], emit_norm=False)
    return out.reshape(b, s, d)


def kernel(x, positions, norm_ffn1, ffn1_w_gate, ffn1_w_up, ffn1_w_down, norm_mix, w_in,
           q_norm, k_norm, idx_k_norm, sgu_v_norm, sgu_w_s, sgu_b_s, w_up_attn, w_up_sgu,
           w_out, norm_ffn2, ffn2_w_gate, ffn2_w_up, ffn2_w_down):
    params = dict(
        norm_ffn1=norm_ffn1, ffn1_w_gate=ffn1_w_gate, ffn1_w_up=ffn1_w_up,
        ffn1_w_down=ffn1_w_down, norm_mix=norm_mix, w_in=w_in, q_norm=q_norm,
        k_norm=k_norm, idx_k_norm=idx_k_norm, sgu_v_norm=sgu_v_norm, sgu_w_s=sgu_w_s,
        sgu_b_s=sgu_b_s, w_up_attn=w_up_attn, w_up_sgu=w_up_sgu, w_out=w_out,
        norm_ffn2=norm_ffn2, ffn2_w_gate=ffn2_w_gate, ffn2_w_up=ffn2_w_up,
        ffn2_w_down=ffn2_w_down)
    pos = positions.reshape(-1, 1).astype(I32)
    tables_att, tables_idx = _rope_table(pos, _inv_freq_lanes())
    for l in range(w_in.shape[0]):
        x = _layer(x, tables_att, tables_idx, {k: v[l] for k, v in params.items()})
    return x
```

```python
import functools

import jax
import jax.numpy as jnp
from jax import lax
from jax.experimental import pallas as pl
from jax.experimental.pallas import tpu as pltpu

F32 = jnp.float32
BF16 = jnp.bfloat16
I32 = jnp.int32

EPS = 1e-6
CHUNK = 64
N_HEADS = 8
HEAD_DIM = 128
ATT_WIDTH = N_HEADS * HEAD_DIM
ROPE_THETA = 500000.0
ROT_DIM = HEAD_DIM // 4
IDX_HEADS = 16
IDX_DIM = 64
IDX_ROT_DIM = IDX_DIM // 4
TOPK_MAX = 256
SGU_LEN = 128
SGU_GROUPS = 8
SGU_GROUP_DIM = 128
SGU_WIDTH = SGU_GROUPS * SGU_GROUP_DIM

LANES = 128
MXU_DIM = 256
VMEM_LIMIT_BYTES = 56 * 1024 * 1024

SEL_T = MXU_DIM
COUNT_ROWS = 16
UNCHECKED_PASSES = 16
ATT_T = 256
ATT_Q = 256
ATT_HEADS = 4
DENOM_ROWS = 16
ABOVE_MAX_REL = 1e-6
ABOVE_MAX_FLOOR = 1e-30
MASKED_LOGIT = -1e30
LOG2_E = 1.4426950408889634


def _params(*sem):
    return pltpu.CompilerParams(dimension_semantics=sem,
                                vmem_limit_bytes=VMEM_LIMIT_BYTES)


def _rms(t, gain):
    ms = jnp.mean(t * t, axis=-1, keepdims=True)
    return t * lax.rsqrt(ms + EPS) * gain


def _rope(t, c, s1, s2, half):
    return (t * c + pltpu.roll(t, LANES - half, 1) * s1
            + pltpu.roll(t, half, 1) * s2)


def _ffn_kernel(x_ref, g_ref, wg_ref, wu_ref, wd_ref, g2_ref, o_ref, *rest,
                emit_norm):
    if emit_norm:
        h_out_ref, hn_ref = rest
    else:
        (hn_ref,) = rest
    j = pl.program_id(1)

    @pl.when(j == 0)
    def _():
        hn_ref[...] = _rms(x_ref[...], g_ref[...]).astype(BF16)
        o_ref[...] = jnp.zeros_like(o_ref)

    h = hn_ref[...]
    a = jnp.dot(h, wg_ref[...], preferred_element_type=F32)
    b = jnp.dot(h, wu_ref[...], preferred_element_type=F32)
    act = (a * jax.nn.sigmoid(a) * b).astype(BF16)
    o_ref[...] += jnp.dot(act, wd_ref[...], preferred_element_type=F32)

    @pl.when(j == pl.num_programs(1) - 1)
    def _():
        y = x_ref[...] + 0.5 * o_ref[...]
        o_ref[...] = y
        if emit_norm:
            h_out_ref[...] = _rms(y, g2_ref[...]).astype(BF16)


def _ffn(x, gain, w_gate, w_up, w_down, gain2, *, emit_norm, tm=512, tf=512):
    n, d = x.shape
    f = w_gate.shape[1]
    tm = min(tm, n)
    out_shape = [jax.ShapeDtypeStruct((n, d), F32)]
    out_specs = [pl.BlockSpec((tm, d), lambda i, j: (i, 0))]
    if emit_norm:
        out_shape.append(jax.ShapeDtypeStruct((n, d), BF16))
        out_specs.append(pl.BlockSpec((tm, d), lambda i, j: (i, 0)))
    res = pl.pallas_call(
        functools.partial(_ffn_kernel, emit_norm=emit_norm),
        grid=(n // tm, f // tf),
        in_specs=[
            pl.BlockSpec((tm, d), lambda i, j: (i, 0)),
            pl.BlockSpec((1, d), lambda i, j: (0, 0)),
            pl.BlockSpec((d, tf), lambda i, j: (0, j)),
            pl.BlockSpec((d, tf), lambda i, j: (0, j)),
            pl.BlockSpec((tf, d), lambda i, j: (j, 0)),
            pl.BlockSpec((1, d), lambda i, j: (0, 0)),
        ],
        out_specs=out_specs,
        out_shape=out_shape,
        scratch_shapes=[pltpu.VMEM((tm, d), BF16)],
        compiler_params=_params("parallel", "arbitrary"),
        name="ffn_norm" if emit_norm else "ffn",
    )(x, gain, w_gate, w_up, w_down, gain2)
    return res if emit_norm else res[0]


def _rope_table_kernel(pos_ref, invf_ref, ca_ref, s1a_ref, s2a_ref, ci_ref, s1i_ref, s2i_ref):
    ang = pos_ref[...].astype(F32) * invf_ref[...]
    c = jnp.cos(ang)
    s = jnp.sin(ang)
    lane = lax.broadcasted_iota(I32, ang.shape, 1)
    in_att = lane < ROT_DIM
    ca_ref[...] = jnp.where(in_att, c, 1.0)
    s1a_ref[...] = jnp.where(lane < ROT_DIM // 2, -s, 0.0)
    s2a_ref[...] = jnp.where((lane >= ROT_DIM // 2) & in_att, s, 0.0)
    first = lane < IDX_DIM
    c_i = jnp.where(first, pltpu.roll(c, LANES - ROT_DIM, 1), pltpu.roll(c, IDX_DIM - ROT_DIM, 1))
    s_i = jnp.where(first, pltpu.roll(s, LANES - ROT_DIM, 1), pltpu.roll(s, IDX_DIM - ROT_DIM, 1))
    j = lane & (IDX_DIM - 1)
    in_idx = j < IDX_ROT_DIM
    ci_ref[...] = jnp.where(in_idx, c_i, 1.0)
    s1i_ref[...] = jnp.where(j < IDX_ROT_DIM // 2, -s_i, 0.0)
    s2i_ref[...] = jnp.where((j >= IDX_ROT_DIM // 2) & in_idx, s_i, 0.0)


def _rope_table(pos, invf, *, tm=2048):
    n = pos.shape[0]
    tm = min(tm, n)
    spec = pl.BlockSpec((tm, LANES), lambda i: (i, 0))
    out = pl.pallas_call(
        _rope_table_kernel,
        grid=(n // tm,),
        in_specs=[pl.BlockSpec((tm, 1), lambda i: (i, 0)),
                  pl.BlockSpec((1, LANES), lambda i: (0, 0))],
        out_specs=[spec] * 6,
        out_shape=[jax.ShapeDtypeStruct((n, LANES), F32)] * 6,
        compiler_params=_params("parallel"),
        name="rope_tables",
    )(pos, invf)
    return out[:3], out[3:]


def _head_proj_kernel(h_ref, w_ref, gain_ref, c_ref, s1_ref, s2_ref, o_ref, r_ref,
                      *, transpose):
    half = ROT_DIM // 2
    res = jnp.dot(h_ref[...], w_ref[...], preferred_element_type=F32)
    heads = res.shape[1] // HEAD_DIM
    for hh in range(heads):
        r_ref[hh] = res[:, hh * HEAD_DIM:(hh + 1) * HEAD_DIM]

    def head(hh, carry):
        t = _rope(_rms(r_ref[hh], gain_ref[...]), c_ref[...], s1_ref[...], s2_ref[...], half)
        if transpose:
            o_ref[0, pl.ds(pl.multiple_of(hh * HEAD_DIM, HEAD_DIM), HEAD_DIM), :] = (
                t.T.astype(BF16))
        else:
            o_ref[0, hh] = t.astype(BF16)
        return carry

    lax.fori_loop(0, heads, head, 0)


def _head_proj(h, w, gain, tables, *, batch, transpose, name, tm=1024, tn=512):
    n, d = h.shape
    cols = w.shape[1]
    s = n // batch
    tm = min(tm, s)
    per_b = s // tm
    heads = tn // HEAD_DIM
    table_spec = pl.BlockSpec((tm, LANES), lambda i, j: (i, 0))
    if transpose:
        out_spec = pl.BlockSpec((1, tn, tm), lambda i, j: (i // per_b, j, i % per_b))
        out_shape = jax.ShapeDtypeStruct((batch, cols, s), BF16)
    else:
        out_spec = pl.BlockSpec((1, heads, tm, HEAD_DIM),
                                lambda i, j: (i // per_b, j, i % per_b, 0))
        out_shape = jax.ShapeDtypeStruct((batch, cols // HEAD_DIM, s, HEAD_DIM), BF16)
    return pl.pallas_call(
        functools.partial(_head_proj_kernel, transpose=transpose),
        grid=(n // tm, cols // tn),
        in_specs=[
            pl.BlockSpec((tm, d), lambda i, j: (i, 0)),
            pl.BlockSpec((d, tn), lambda i, j: (0, j)),
            pl.BlockSpec((1, HEAD_DIM), lambda i, j: (0, 0)),
            table_spec, table_spec, table_spec,
        ],
        out_specs=out_spec,
        out_shape=out_shape,
        scratch_shapes=[pltpu.VMEM((heads, tm, HEAD_DIM), F32)],
        compiler_params=_params("parallel", "arbitrary"),
        name=name,
    )(h, w, gain, *tables)


def _v_proj_kernel(h_ref, w_ref, o_ref):
    res = jnp.dot(h_ref[...], w_ref[...], preferred_element_type=F32)
    for a in range(res.shape[0] // ATT_T):
        o_ref[0, a] = res[a * ATT_T:(a + 1) * ATT_T, :].T.astype(BF16)


def _v_proj(h, w, *, batch, tm=1024, tn=512):
    n, d = h.shape
    cols = w.shape[1]
    s = n // batch
    tm = min(tm, s)
    per_b = s // tm
    return pl.pallas_call(
        _v_proj_kernel,
        grid=(n // tm, cols // tn),
        in_specs=[
            pl.BlockSpec((tm, d), lambda i, j: (i, 0)),
            pl.BlockSpec((d, tn), lambda i, j: (0, j)),
        ],
        out_specs=pl.BlockSpec((1, tm // ATT_T, tn, ATT_T),
                               lambda i, j: (i // per_b, i % per_b, j, 0)),
        out_shape=jax.ShapeDtypeStruct((batch, s // ATT_T, cols, ATT_T), BF16),
        compiler_params=_params("parallel", "arbitrary"),
        name="v_proj",
    )(h, w)


def _idx_kernel(h_ref, w_ref, kgain_ref, c_ref, s1_ref, s2_ref,
                qi_ref, kcat_ref, wi_ref, *, w_scale):
    half = IDX_ROT_DIM // 2
    qw = IDX_HEADS * IDX_DIM
    res = jnp.dot(h_ref[...], w_ref[...], preferred_element_type=F32)
    c, s1, s2 = c_ref[...], s1_ref[...], s2_ref[...]
    for t in range(qw // LANES):
        sl = slice(t * LANES, (t + 1) * LANES)
        q_t = _rope(res[:, sl], c, s1, s2, half).T
        qi_ref[0, 2 * t] = q_t[:IDX_DIM]
        qi_ref[0, 2 * t + 1] = q_t[IDX_DIM:]

    r = res[:, qw:qw + LANES]
    lane = lax.broadcasted_iota(I32, r.shape, 1)
    is_k = lane < IDX_DIM
    ms = jnp.sum(jnp.where(is_k, r * r, 0.0), axis=-1, keepdims=True) * (1.0 / IDX_DIM)
    kn = r * lax.rsqrt(ms + EPS) * kgain_ref[...]
    kr = _rope(kn, jnp.where(is_k, c, 1.0), jnp.where(is_k, s1, 0.0),
               jnp.where(is_k, s2, 0.0), half)
    hi = kr.astype(BF16).astype(F32)
    lo = kr - hi
    hi_lo = jnp.where(is_k, hi, pltpu.roll(lo, IDX_DIM, 1)).astype(BF16)
    kcat_ref[:, 0:LANES] = hi_lo
    kcat_ref[:, LANES:2 * LANES] = hi_lo
    is_w = (lane >= IDX_DIM) & (lane < IDX_DIM + IDX_HEADS)
    w_t = pltpu.roll(jnp.where(is_w, r * w_scale, 0.0), LANES - IDX_DIM, 1).T
    wi_ref[0] = w_t[:IDX_HEADS]


def _idx(h, w_idx, kgain, tables, *, batch, tm=512):
    n, d = h.shape
    cols = w_idx.shape[1]
    s = n // batch
    tm = min(tm, s)
    per_b = s // tm
    w_scale = (IDX_HEADS ** -0.5) * (IDX_DIM ** -0.5)
    return pl.pallas_call(
        functools.partial(_idx_kernel, w_scale=w_scale),
        grid=(n // tm,),
        in_specs=[
            pl.BlockSpec((tm, d), lambda i: (i, 0)),
            pl.BlockSpec((d, cols), lambda i: (0, 0)),
            pl.BlockSpec((1, LANES), lambda i: (0, 0)),
            pl.BlockSpec((tm, LANES), lambda i: (i, 0)),
            pl.BlockSpec((tm, LANES), lambda i: (i, 0)),
            pl.BlockSpec((tm, LANES), lambda i: (i, 0)),
        ],
        out_specs=[
            pl.BlockSpec((1, IDX_HEADS, IDX_DIM, tm),
                         lambda i: (i // per_b, 0, 0, i % per_b)),
            pl.BlockSpec((tm, 2 * LANES), lambda i: (i, 0)),
            pl.BlockSpec((1, IDX_HEADS, tm), lambda i: (i // per_b, 0, i % per_b)),
        ],
        out_shape=[
            jax.ShapeDtypeStruct((batch, IDX_HEADS, IDX_DIM, s), F32),
            jax.ShapeDtypeStruct((n, 2 * LANES), BF16),
            jax.ShapeDtypeStruct((batch, IDX_HEADS, s), F32),
        ],
        compiler_params=_params("parallel"),
        name="idx_proj",
    )(h, w_idx, kgain, *tables)


def _sgu_kernel(h_ref, w_ref, gv_ref, ws_ref, bs_ref, o_ref):
    tm = h_ref.shape[0]
    groups = tm // SGU_LEN
    res = jnp.dot(h_ref[...], w_ref[...], preferred_element_type=F32)
    u = jax.nn.gelu(res[:, :SGU_WIDTH])
    v = _rms(jax.nn.gelu(res[:, SGU_WIDTH:]), gv_ref[...]).astype(BF16)
    row = lax.broadcasted_iota(I32, (SGU_LEN, SGU_LEN), 0)
    col = lax.broadcasted_iota(I32, (SGU_LEN, SGU_LEN), 1)
    causal = (col // CHUNK) <= (row // CHUNK)
    for g in range(SGU_GROUPS):
        cs = slice(g * SGU_GROUP_DIM, (g + 1) * SGU_GROUP_DIM)
        wg = jnp.where(causal, ws_ref[g], 0.0).astype(BF16)
        vg = jnp.concatenate(
            [v[n * SGU_LEN:(n + 1) * SGU_LEN, cs] for n in range(groups)], axis=1)
        mixed = jnp.dot(wg, vg, preferred_element_type=F32) + bs_ref[g]
        for n in range(groups):
            rs = slice(n * SGU_LEN, (n + 1) * SGU_LEN)
            m = mixed[:, n * SGU_GROUP_DIM:(n + 1) * SGU_GROUP_DIM]
            o_ref[rs, cs] = (u[rs, cs] * m).astype(BF16)


def _sgu(h, w_sgu, gv, w_s, b_s, *, tm=512):
    n, d = h.shape
    tm = min(tm, n)
    return pl.pallas_call(
        _sgu_kernel,
        grid=(n // tm,),
        in_specs=[
            pl.BlockSpec((tm, d), lambda i: (i, 0)),
            pl.BlockSpec((d, 2 * SGU_WIDTH), lambda i: (0, 0)),
            pl.BlockSpec((1, SGU_WIDTH), lambda i: (0, 0)),
            pl.BlockSpec((SGU_GROUPS, SGU_LEN, SGU_LEN), lambda i: (0, 0, 0)),
            pl.BlockSpec((SGU_GROUPS, SGU_LEN, 1), lambda i: (0, 0, 0)),
        ],
        out_specs=pl.BlockSpec((tm, SGU_WIDTH), lambda i: (i, 0)),
        out_shape=jax.ShapeDtypeStruct((n, SGU_WIDTH), BF16),
        compiler_params=_params("parallel"),
        name="sgu_branch",
    )(h, w_sgu, gv, w_s, b_s)


def _gates_kernel(h_ref, w_ref, o_ref):
    res = jnp.dot(h_ref[...], w_ref[...], preferred_element_type=F32)
    o_ref[...] = jax.nn.sigmoid(res).astype(BF16)


def _gates(h, w_g, *, tm=1024, tn=1024):
    n, d = h.shape
    cols = w_g.shape[1]
    tm = min(tm, n)
    return pl.pallas_call(
        _gates_kernel,
        grid=(n // tm, cols // tn),
        in_specs=[
            pl.BlockSpec((tm, d), lambda i, j: (i, 0)),
            pl.BlockSpec((d, tn), lambda i, j: (0, j)),
        ],
        out_specs=pl.BlockSpec((tm, tn), lambda i, j: (i, j)),
        out_shape=jax.ShapeDtypeStruct((n, cols), BF16),
        compiler_params=_params("parallel", "arbitrary"),
        name="gates_proj",
    )(h, w_g)


def _select_kernel(qi_ref, wi_ref, kcat_ref, mask_ref, qcat_ref, s_ref, *, topk):
    t = SEL_T
    n_tiles = s_ref.shape[0]
    i = pl.program_id(1)
    n_live = i + 1

    for h in range(IDX_HEADS):
        q = qi_ref[0, h]
        hi = q.astype(BF16)
        lo = (q - hi.astype(F32)).astype(BF16)
        qcat_ref[h] = jnp.concatenate([hi, hi, lo, lo], axis=0)
    w = wi_ref[0]

    def score_tile(kt, mx, mn, diagonal):
        kc = kcat_ref[0, pl.ds(pl.multiple_of(kt * t, t), t), :]
        acc = jnp.zeros((t, t), F32)
        for h in range(IDX_HEADS):
            d = jnp.dot(kc, qcat_ref[h], preferred_element_type=F32)
            acc = acc + jnp.maximum(d, 0.0) * w[h:h + 1, :]
        if diagonal:
            admissible = ((lax.broadcasted_iota(I32, (t, t), 0) // CHUNK)
                          <= (lax.broadcasted_iota(I32, (t, t), 1) // CHUNK))
            s = jnp.where(admissible, acc, -jnp.inf)
            low = jnp.where(admissible, acc, jnp.inf)
        else:
            s = low = acc
        s_ref[kt] = s
        mx = jnp.maximum(mx, jnp.max(s, axis=0, keepdims=True))
        mn = jnp.minimum(mn, jnp.min(low, axis=0, keepdims=True))
        return mx, mn

    def score_body(j, carry):
        mx, mn = score_tile(2 * j, *carry, diagonal=False)
        return score_tile(jnp.minimum(2 * j + 1, i - 1), mx, mn, diagonal=False)

    mx, mn = lax.fori_loop(
        0, (i + 1) // 2, score_body,
        (jnp.full((1, t), -jnp.inf, F32), jnp.full((1, t), jnp.inf, F32)))
    mx, mn = score_tile(i, mx, mn, diagonal=True)

    n_adm = ((i * t + lax.broadcasted_iota(I32, (1, t), 1)) // CHUNK + 1) * CHUNK
    done0 = n_adm <= topk

    def count(c, strict=False):
        def add_tile(kt, weight, acc):
            s = s_ref[kt]
            ind = jnp.where(s > c if strict else s >= c, weight, 0.0)
            for r in range(t // COUNT_ROWS):
                acc = acc + ind[r * COUNT_ROWS:(r + 1) * COUNT_ROWS]
            return acc

        def count_body(j, acc):
            acc = add_tile(2 * j, 1.0, acc)
            second = 2 * j + 1
            return add_tile(jnp.minimum(second, n_live - 1),
                            jnp.where(second < n_live, 1.0, 0.0), acc)

        acc = lax.fori_loop(0, (n_live + 1) // 2, count_body,
                            jnp.zeros((COUNT_ROWS, t), F32))
        return jnp.sum(acc, axis=0, keepdims=True)

    def search_step(st):
        lo, hi, c_lo, thr, n_thr, live = st
        mid = 0.5 * lo + 0.5 * hi
        stuck = (mid <= lo) | (mid >= hi)
        cnt = count(mid)
        finish = (live > 0.0) & (stuck | (cnt == topk))
        thr = jnp.where(finish, jnp.where(stuck, lo, mid), thr)
        n_thr = jnp.where(finish, jnp.where(stuck, c_lo, cnt), n_thr)
        live = jnp.where(finish, 0.0, live)
        above = cnt > topk
        lo = jnp.where(above, mid, lo)
        c_lo = jnp.where(above, cnt, c_lo)
        hi = jnp.where(cnt < topk, mid, hi)
        return lo, hi, c_lo, thr, n_thr, live

    def search_body(st):
        st = search_step(st[:-1])
        return st + (jnp.sum(st[-1]),)

    hi0 = jnp.minimum(mx + jnp.maximum(jnp.abs(mx), ABOVE_MAX_FLOOR) * ABOVE_MAX_REL,
                      jnp.finfo(F32).max)
    live0 = jnp.where(done0, 0.0, 1.0)
    st = (mn, hi0, n_adm.astype(F32), jnp.full((1, t), -jnp.inf, F32),
          jnp.zeros((1, t), F32), live0)
    st = lax.fori_loop(0, UNCHECKED_PASSES, lambda _, s_: search_step(s_), st)
    st = lax.while_loop(lambda s_: s_[-1] > 0.0, search_body, st + (jnp.sum(st[-1]),))
    thr, n_thr = st[3], st[4]
    tied = n_thr > topk
    any_tied = jnp.sum(jnp.where(tied, 1.0, 0.0)) > 0.0

    @pl.when(jnp.logical_not(any_tied))
    def _():
        def mask_body(kt, carry):
            s = s_ref[kt]
            sel = (s >= thr) & (s > -jnp.inf)
            mask_ref[0, 0, kt] = jnp.where(sel, 1, 0).astype(jnp.int8)
            return carry

        lax.fori_loop(0, n_live, mask_body, 0)

    @pl.when(any_tied)
    def _():
        room = topk - count(thr, strict=True)
        tri = jnp.where(lax.broadcasted_iota(I32, (t, t), 0)
                        >= lax.broadcasted_iota(I32, (t, t), 1), 1.0, 0.0).astype(BF16)

        def tie_body(kt, seen):
            s = s_ref[kt]
            eq = (s == thr) & tied
            eq_f = jnp.where(eq, 1.0, 0.0)
            upto = jnp.dot(tri, eq_f.astype(BF16), preferred_element_type=F32)
            rank = seen + upto - eq_f
            keep = (eq & (rank < room)) | (jnp.logical_not(eq) & (s >= thr))
            sel = keep & (s > -jnp.inf)
            mask_ref[0, 0, kt] = jnp.where(sel, 1, 0).astype(jnp.int8)
            return seen + upto[t - 1:t, :]

        lax.fori_loop(0, n_live, tie_body, jnp.zeros((1, t), F32))

    def zero_body(kt, carry):
        mask_ref[0, 0, kt] = jnp.zeros((t, t), jnp.int8)
        return carry

    lax.fori_loop(n_live, n_tiles, zero_body, 0)


def _select(qi_t, wi_t, kcat, *, topk):
    b, heads, dh, s = qi_t.shape
    t = SEL_T
    nq = s // t
    return pl.pallas_call(
        functools.partial(_select_kernel, topk=topk),
        grid=(b, nq),
        in_specs=[
            pl.BlockSpec((1, heads, dh, t), lambda bi, i: (bi, 0, 0, i)),
            pl.BlockSpec((1, heads, t), lambda bi, i: (bi, 0, i)),
            pl.BlockSpec((1, s, 4 * IDX_DIM), lambda bi, i: (bi, 0, 0)),
        ],
        out_specs=pl.BlockSpec((1, 1, nq, t, t), lambda bi, i: (bi, i, 0, 0, 0)),
        out_shape=jax.ShapeDtypeStruct((b, nq, nq, t, t), jnp.int8),
        scratch_shapes=[
            pltpu.VMEM((heads, 4 * IDX_DIM, t), BF16),
            pltpu.VMEM((nq, t, t), F32),
        ],
        compiler_params=_params("parallel", "arbitrary"),
        name="index_select",
    )(qi_t, wi_t, kcat)


def _attn_kernel(qt_ref, k_ref, vt_ref, mask_ref, o_ref,
                 m_ref, acc_ref, s_ref):
    t = ATT_T
    i = pl.program_id(2)
    n = (i + 1) * (ATT_Q // t)
    m_ref[...] = jnp.full(m_ref.shape, -jnp.inf, F32)
    acc_ref[...] = jnp.zeros(acc_ref.shape, F32)
    q_t = qt_ref[0]

    def logits(kt, slot):
        rows = pl.ds(pl.multiple_of(kt * t, t), t)
        for hh in range(ATT_HEADS):
            hs = slice(hh * HEAD_DIM, (hh + 1) * HEAD_DIM)
            s_ref[slot, hh] = jnp.dot(k_ref[0, hh, rows, :], q_t[hs, :],
                                      preferred_element_type=F32)

    def update(kt, slot, off):
        v_t = vt_ref[0, kt]
        sel = jnp.concatenate([mask_ref[0, a, kt] for a in range(ATT_Q // t)],
                              axis=1).astype(I32) > off
        for hh in range(ATT_HEADS):
            hs = slice(hh * HEAD_DIM, (hh + 1) * HEAD_DIM)
            s = jnp.where(sel, s_ref[slot, hh], MASKED_LOGIT)
            m_prev = m_ref[hh]
            m_new = jnp.maximum(m_prev, jnp.max(s, axis=0, keepdims=True))
            p = jnp.exp2(s - m_new)
            alpha = jnp.exp2(m_prev - m_new)
            m_ref[hh] = m_new
            lhs = jnp.concatenate([v_t[hs, :], jnp.ones((DENOM_ROWS, t), BF16)], axis=0)
            pv = jnp.dot(lhs, p.astype(BF16), preferred_element_type=F32)
            acc_ref[hh] = alpha * acc_ref[hh] + pv

    logits(0, 0)

    def body(j, carry):
        first = 2 * j
        second = jnp.minimum(first + 1, n - 1)
        logits(second, 1)
        update(first, 0, 0)
        logits(jnp.minimum(first + 2, n - 1), 0)
        update(second, 1, jnp.where(first + 1 < n, 0, 1))
        return carry

    lax.fori_loop(0, (n + 1) // 2, body, 0)
    for hh in range(ATT_HEADS):
        acc = acc_ref[hh]
        o_ref[0, hh * HEAD_DIM:(hh + 1) * HEAD_DIM, :] = (
            acc[:HEAD_DIM] / acc[HEAD_DIM:HEAD_DIM + 1]).astype(BF16)


def _attn(q_t, k, v_t, mask):
    b, _, s, _ = k.shape
    t = ATT_T
    tq = ATT_Q
    nk = s // t
    width = ATT_HEADS * HEAD_DIM
    return pl.pallas_call(
        _attn_kernel,
        grid=(b, N_HEADS // ATT_HEADS, s // tq),
        in_specs=[
            pl.BlockSpec((1, width, tq), lambda bi, hp, i: (bi, hp, i)),
            pl.BlockSpec((1, ATT_HEADS, s, HEAD_DIM), lambda bi, hp, i: (bi, hp, 0, 0)),
            pl.BlockSpec((1, nk, width, t), lambda bi, hp, i: (bi, 0, hp, 0)),
            pl.BlockSpec((1, tq // t, nk, t, t), lambda bi, hp, i: (bi, i, 0, 0, 0)),
        ],
        out_specs=pl.BlockSpec((1, width, tq), lambda bi, hp, i: (bi, hp, i)),
        out_shape=jax.ShapeDtypeStruct((b, ATT_WIDTH, s), BF16),
        scratch_shapes=[
            pltpu.VMEM((ATT_HEADS, 1, tq), F32),
            pltpu.VMEM((ATT_HEADS, HEAD_DIM + DENOM_ROWS, tq), F32),
            pltpu.VMEM((2, ATT_HEADS, t, tq), F32),
        ],
        compiler_params=_params("parallel", "parallel", "arbitrary"),
        name="masked_attention",
    )(q_t, k, v_t, mask)


def _merge_kernel(ya_ref, yb_ref, g_ref, x_ref, wa_ref, wb_ref, wo_ref, o_ref):
    d = x_ref.shape[1]
    a = lax.dot_general(ya_ref[0], wa_ref[...], (((0,), (0,)), ((), ())),
                        preferred_element_type=F32)
    b = jnp.dot(yb_ref[...], wb_ref[...], preferred_element_type=F32)
    merged = g_ref[:, :d].astype(F32) * a + g_ref[:, d:].astype(F32) * b
    o_ref[...] = x_ref[...] + jnp.dot(merged.astype(BF16), wo_ref[...],
                                      preferred_element_type=F32)


def _merge(ya_t, yb, gates, x, wa, wb, wo, *, tm=256):
    n, d = x.shape
    batch, width, s = ya_t.shape
    tm = min(tm, s)
    per_b = s // tm
    return pl.pallas_call(
        _merge_kernel,
        grid=(n // tm,),
        in_specs=[
            pl.BlockSpec((1, width, tm), lambda i: (i // per_b, 0, i % per_b)),
            pl.BlockSpec((tm, yb.shape[1]), lambda i: (i, 0)),
            pl.BlockSpec((tm, 2 * d), lambda i: (i, 0)),
            pl.BlockSpec((tm, d), lambda i: (i, 0)),
            pl.BlockSpec(wa.shape, lambda i: (0, 0)),
            pl.BlockSpec(wb.shape, lambda i: (0, 0)),
            pl.BlockSpec(wo.shape, lambda i: (0, 0)),
        ],
        out_specs=pl.BlockSpec((tm, d), lambda i: (i, 0)),
        out_shape=jax.ShapeDtypeStruct((n, d), F32),
        compiler_params=_params("parallel"),
        name="merge_out",
    )(ya_t, yb, gates, x, wa, wb, wo)


def _inv_freq_lanes():
    def inv(rot_dim):
        f = ROPE_THETA ** (-jnp.arange(0, rot_dim, 2, dtype=F32) / rot_dim)
        return jnp.concatenate([f, f])
    lanes = jnp.concatenate([inv(ROT_DIM), inv(IDX_ROT_DIM),
                             jnp.zeros((LANES - ROT_DIM - IDX_ROT_DIM,), F32)])
    return lanes[None, :]


def _layer(x, tables_att, tables_idx, p):
    b, s, d = x.shape
    n = b * s
    topk = min(TOPK_MAX, s // 4)
    x = x.reshape(n, d)

    x1, h = _ffn(x, p["norm_ffn1"][None], p["ffn1_w_gate"].astype(BF16),
                 p["ffn1_w_up"].astype(BF16), p["ffn1_w_down"].astype(BF16),
                 p["norm_mix"][None], emit_norm=True)

    w_in = p["w_in"]
    c_qkv = 3 * ATT_WIDTH
    c_qi = c_qkv + IDX_HEADS * IDX_DIM
    c_idx = c_qi + IDX_DIM + IDX_HEADS
    c_sgu = c_idx + 2 * SGU_WIDTH
    w_q = w_in[:, :ATT_WIDTH].astype(BF16)
    w_k = w_in[:, ATT_WIDTH:2 * ATT_WIDTH].astype(BF16)
    w_v = w_in[:, 2 * ATT_WIDTH:c_qkv].astype(BF16)
    w_idx = jnp.pad(w_in[:, c_qkv:c_idx],
                    ((0, 0), (0, LANES - IDX_DIM - IDX_HEADS))).astype(BF16)
    w_sgu = w_in[:, c_idx:c_sgu].astype(BF16)
    w_gate = w_in[:, c_sgu:].astype(BF16)

    q_gain = (p["q_norm"] * (HEAD_DIM ** -0.5 * LOG2_E))[None]
    q_t = _head_proj(h, w_q, q_gain, tables_att, batch=b, transpose=True, name="q_proj")
    k = _head_proj(h, w_k, p["k_norm"][None], tables_att, batch=b, transpose=False,
                   name="k_proj")
    v_t = _v_proj(h, w_v, batch=b)
    kgain = jnp.pad(p["idx_k_norm"], (0, LANES - IDX_DIM))[None]
    qi_t, kcat, wi_t = _idx(h, w_idx, kgain, tables_idx, batch=b)
    yb = _sgu(h, w_sgu, p["sgu_v_norm"][None], p["sgu_w_s"], p["sgu_b_s"][:, :, None])
    gates = _gates(h, w_gate)

    mask = _select(qi_t, wi_t, kcat.reshape(b, s, 4 * IDX_DIM), topk=topk)
    ya_t = _attn(q_t, k, v_t, mask)

    x2 = _merge(ya_t, yb, gates, x1, p["w_up_attn"].astype(BF16),
                p["w_up_sgu"].astype(BF16), p["w_out"].astype(BF16))
    out = _ffn(x2, p["norm_ffn2"][None], p["ffn2_w_gate"].astype(BF16),
               p["ffn2_w_up"].astype(BF16), p["ffn2_w_down"].astype(BF16),
               p["norm_ffn2"][None], emit_norm=False)
    return out.reshape(b, s, d)


def kernel(x, positions, norm_ffn1, ffn1_w_gate, ffn1_w_up, ffn1_w_down, norm_mix, w_in,
           q_norm, k_norm, idx_k_norm, sgu_v_norm, sgu_w_s, sgu_b_s, w_up_attn, w_up_sgu,
           w_out, norm_ffn2, ffn2_w_gate, ffn2_w_up, ffn2_w_down):
    params = dict(
        norm_ffn1=norm_ffn1, ffn1_w_gate=ffn1_w_gate, ffn1_w_up=ffn1_w_up,
        ffn1_w_down=ffn1_w_down, norm_mix=norm_mix, w_in=w_in, q_norm=q_norm,
        k_norm=k_norm, idx_k_norm=idx_k_norm, sgu_v_norm=sgu_v_norm, sgu_w_s=sgu_w_s,
        sgu_b_s=sgu_b_s, w_up_attn=w_up_attn, w_up_sgu=w_up_sgu, w_out=w_out,
        norm_ffn2=norm_ffn2, ffn2_w_gate=ffn2_w_gate, ffn2_w_up=ffn2_w_up,
        ffn2_w_down=ffn2_w_down)
    pos = positions.reshape(-1, 1).astype(I32)
    tables_att, tables_idx = _rope_table(pos, _inv_freq_lanes())
    for l in range(w_in.shape[0]):
        x = _layer(x, tables_att, tables_idx, {k: v[l] for k, v in params.items()})
    return x
```

```python
import functools

import jax
import jax.numpy as jnp
from jax import lax
from jax.experimental import pallas as pl
from jax.experimental.pallas import tpu as pltpu

F32 = jnp.float32
BF16 = jnp.bfloat16
I32 = jnp.int32

EPS = 1e-6
CHUNK = 64
N_HEADS = 8
HEAD_DIM = 128
ATT_WIDTH = N_HEADS * HEAD_DIM
ROPE_THETA = 500000.0
ROT_DIM = HEAD_DIM // 4
IDX_HEADS = 16
IDX_DIM = 64
IDX_ROT_DIM = IDX_DIM // 4
TOPK_MAX = 256
SGU_LEN = 128
SGU_GROUPS = 8
SGU_GROUP_DIM = 128
SGU_WIDTH = SGU_GROUPS * SGU_GROUP_DIM

LANES = 128
MXU_DIM = 256
VMEM_LIMIT_BYTES = 56 * 1024 * 1024

SEL_T = MXU_DIM
COUNT_ROWS = 16
UNCHECKED_PASSES = 16
MAX_PASSES = 320
ATT_T = 256
ATT_Q = 256
ATT_HEADS = 4
DENOM_ROWS = 16
ABOVE_MAX_REL = 1e-6
ABOVE_MAX_FLOOR = 1e-30
MASKED_LOGIT = -1e30
LOG2_E = 1.4426950408889634


def _params(*sem):
    return pltpu.CompilerParams(dimension_semantics=sem,
                                vmem_limit_bytes=VMEM_LIMIT_BYTES)


def _rms(t, gain):
    ms = jnp.mean(t * t, axis=-1, keepdims=True)
    return t * lax.rsqrt(ms + EPS) * gain


def _rope(t, c, s1, s2, half):
    return (t * c + pltpu.roll(t, LANES - half, 1) * s1
            + pltpu.roll(t, half, 1) * s2)


def _ffn_kernel(x_ref, g_ref, wg_ref, wu_ref, wd_ref, g2_ref, o_ref, *rest,
                emit_norm):
    if emit_norm:
        h_out_ref, hn_ref = rest
    else:
        (hn_ref,) = rest
    j = pl.program_id(1)

    @pl.when(j == 0)
    def _():
        hn_ref[...] = _rms(x_ref[...], g_ref[...]).astype(BF16)
        o_ref[...] = jnp.zeros_like(o_ref)

    h = hn_ref[...]
    a = jnp.dot(h, wg_ref[...], preferred_element_type=F32)
    b = jnp.dot(h, wu_ref[...], preferred_element_type=F32)
    act = (a * jax.nn.sigmoid(a) * b).astype(BF16)
    o_ref[...] += jnp.dot(act, wd_ref[...], preferred_element_type=F32)

    @pl.when(j == pl.num_programs(1) - 1)
    def _():
        y = x_ref[...] + 0.5 * o_ref[...]
        o_ref[...] = y
        if emit_norm:
            h_out_ref[...] = _rms(y, g2_ref[...]).astype(BF16)


def _ffn(x, gain, w_gate, w_up, w_down, gain2, *, emit_norm, tm=512, tf=512):
    n, d = x.shape
    f = w_gate.shape[1]
    tm = min(tm, n)
    out_shape = [jax.ShapeDtypeStruct((n, d), F32)]
    out_specs = [pl.BlockSpec((tm, d), lambda i, j: (i, 0))]
    if emit_norm:
        out_shape.append(jax.ShapeDtypeStruct((n, d), BF16))
        out_specs.append(pl.BlockSpec((tm, d), lambda i, j: (i, 0)))
    res = pl.pallas_call(
        functools.partial(_ffn_kernel, emit_norm=emit_norm),
        grid=(n // tm, f // tf),
        in_specs=[
            pl.BlockSpec((tm, d), lambda i, j: (i, 0)),
            pl.BlockSpec((1, d), lambda i, j: (0, 0)),
            pl.BlockSpec((d, tf), lambda i, j: (0, j)),
            pl.BlockSpec((d, tf), lambda i, j: (0, j)),
            pl.BlockSpec((tf, d), lambda i, j: (j, 0)),
            pl.BlockSpec((1, d), lambda i, j: (0, 0)),
        ],
        out_specs=out_specs,
        out_shape=out_shape,
        scratch_shapes=[pltpu.VMEM((tm, d), BF16)],
        compiler_params=_params("parallel", "arbitrary"),
        name="ffn_norm" if emit_norm else "ffn",
    )(x, gain, w_gate, w_up, w_down, gain2)
    return res if emit_norm else res[0]


def _rope_table_kernel(pos_ref, invf_ref, ca_ref, s1a_ref, s2a_ref, ci_ref, s1i_ref, s2i_ref):
    ang = pos_ref[...].astype(F32) * invf_ref[...]
    c = jnp.cos(ang)
    s = jnp.sin(ang)
    lane = lax.broadcasted_iota(I32, ang.shape, 1)
    in_att = lane < ROT_DIM
    ca_ref[...] = jnp.where(in_att, c, 1.0)
    s1a_ref[...] = jnp.where(lane < ROT_DIM // 2, -s, 0.0)
    s2a_ref[...] = jnp.where((lane >= ROT_DIM // 2) & in_att, s, 0.0)
    first = lane < IDX_DIM
    c_i = jnp.where(first, pltpu.roll(c, LANES - ROT_DIM, 1), pltpu.roll(c, IDX_DIM - ROT_DIM, 1))
    s_i = jnp.where(first, pltpu.roll(s, LANES - ROT_DIM, 1), pltpu.roll(s, IDX_DIM - ROT_DIM, 1))
    j = lane & (IDX_DIM - 1)
    in_idx = j < IDX_ROT_DIM
    ci_ref[...] = jnp.where(in_idx, c_i, 1.0)
    s1i_ref[...] = jnp.where(j < IDX_ROT_DIM // 2, -s_i, 0.0)
    s2i_ref[...] = jnp.where((j >= IDX_ROT_DIM // 2) & in_idx, s_i, 0.0)


def _rope_table(pos, invf, *, tm=2048):
    n = pos.shape[0]
    tm = min(tm, n)
    spec = pl.BlockSpec((tm, LANES), lambda i: (i, 0))
    out = pl.pallas_call(
        _rope_table_kernel,
        grid=(n // tm,),
        in_specs=[pl.BlockSpec((tm, 1), lambda i: (i, 0)),
                  pl.BlockSpec((1, LANES), lambda i: (0, 0))],
        out_specs=[spec] * 6,
        out_shape=[jax.ShapeDtypeStruct((n, LANES), F32)] * 6,
        compiler_params=_params("parallel"),
        name="rope_tables",
    )(pos, invf)
    return out[:3], out[3:]


def _head_proj_kernel(h_ref, w_ref, gain_ref, c_ref, s1_ref, s2_ref, o_ref, r_ref,
                      *, transpose):
    half = ROT_DIM // 2
    res = jnp.dot(h_ref[...], w_ref[...], preferred_element_type=F32)
    heads = res.shape[1] // HEAD_DIM
    for hh in range(heads):
        r_ref[hh] = res[:, hh * HEAD_DIM:(hh + 1) * HEAD_DIM]

    def head(hh, carry):
        t = _rope(_rms(r_ref[hh], gain_ref[...]), c_ref[...], s1_ref[...], s2_ref[...], half)
        if transpose:
            o_ref[0, pl.ds(pl.multiple_of(hh * HEAD_DIM, HEAD_DIM), HEAD_DIM), :] = (
                t.T.astype(BF16))
        else:
            o_ref[0, hh] = t.astype(BF16)
        return carry

    lax.fori_loop(0, heads, head, 0)


def _head_proj(h, w, gain, tables, *, batch, transpose, name, tm=1024, tn=512):
    n, d = h.shape
    cols = w.shape[1]
    s = n // batch
    tm = min(tm, s)
    per_b = s // tm
    heads = tn // HEAD_DIM
    table_spec = pl.BlockSpec((tm, LANES), lambda i, j: (i, 0))
    if transpose:
        out_spec = pl.BlockSpec((1, tn, tm), lambda i, j: (i // per_b, j, i % per_b))
        out_shape = jax.ShapeDtypeStruct((batch, cols, s), BF16)
    else:
        out_spec = pl.BlockSpec((1, heads, tm, HEAD_DIM),
                                lambda i, j: (i // per_b, j, i % per_b, 0))
        out_shape = jax.ShapeDtypeStruct((batch, cols // HEAD_DIM, s, HEAD_DIM), BF16)
    return pl.pallas_call(
        functools.partial(_head_proj_kernel, transpose=transpose),
        grid=(n // tm, cols // tn),
        in_specs=[
            pl.BlockSpec((tm, d), lambda i, j: (i, 0)),
            pl.BlockSpec((d, tn), lambda i, j: (0, j)),
            pl.BlockSpec((1, HEAD_DIM), lambda i, j: (0, 0)),
            table_spec, table_spec, table_spec,
        ],
        out_specs=out_spec,
        out_shape=out_shape,
        scratch_shapes=[pltpu.VMEM((heads, tm, HEAD_DIM), F32)],
        compiler_params=_params("parallel", "arbitrary"),
        name=name,
    )(h, w, gain, *tables)


def _v_proj_kernel(h_ref, w_ref, o_ref):
    res = jnp.dot(h_ref[...], w_ref[...], preferred_element_type=F32)
    for a in range(res.shape[0] // ATT_T):
        o_ref[0, a] = res[a * ATT_T:(a + 1) * ATT_T, :].T.astype(BF16)


def _v_proj(h, w, *, batch, tm=1024, tn=512):
    n, d = h.shape
    cols = w.shape[1]
    s = n // batch
    tm = min(tm, s)
    per_b = s // tm
    return pl.pallas_call(
        _v_proj_kernel,
        grid=(n // tm, cols // tn),
        in_specs=[
            pl.BlockSpec((tm, d), lambda i, j: (i, 0)),
            pl.BlockSpec((d, tn), lambda i, j: (0, j)),
        ],
        out_specs=pl.BlockSpec((1, tm // ATT_T, tn, ATT_T),
                               lambda i, j: (i // per_b, i % per_b, j, 0)),
        out_shape=jax.ShapeDtypeStruct((batch, s // ATT_T, cols, ATT_T), BF16),
        compiler_params=_params("parallel", "arbitrary"),
        name="v_proj",
    )(h, w)


def _idx_kernel(h_ref, w_ref, kgain_ref, c_ref, s1_ref, s2_ref,
                qi_ref, kcat_ref, wi_ref, *, w_scale):
    half = IDX_ROT_DIM // 2
    qw = IDX_HEADS * IDX_DIM
    res = jnp.dot(h_ref[...], w_ref[...], preferred_element_type=F32)
    c, s1, s2 = c_ref[...], s1_ref[...], s2_ref[...]
    for t in range(qw // LANES):
        sl = slice(t * LANES, (t + 1) * LANES)
        q_t = _rope(res[:, sl], c, s1, s2, half).T
        qi_ref[0, 2 * t] = q_t[:IDX_DIM]
        qi_ref[0, 2 * t + 1] = q_t[IDX_DIM:]

    r = res[:, qw:qw + LANES]
    lane = lax.broadcasted_iota(I32, r.shape, 1)
    is_k = lane < IDX_DIM
    ms = jnp.sum(jnp.where(is_k, r * r, 0.0), axis=-1, keepdims=True) * (1.0 / IDX_DIM)
    kn = r * lax.rsqrt(ms + EPS) * kgain_ref[...]
    kr = _rope(kn, jnp.where(is_k, c, 1.0), jnp.where(is_k, s1, 0.0),
               jnp.where(is_k, s2, 0.0), half)
    hi = kr.astype(BF16).astype(F32)
    lo = kr - hi
    hi_lo = jnp.where(is_k, hi, pltpu.roll(lo, IDX_DIM, 1)).astype(BF16)
    kcat_ref[:, 0:LANES] = hi_lo
    kcat_ref[:, LANES:2 * LANES] = hi_lo
    is_w = (lane >= IDX_DIM) & (lane < IDX_DIM + IDX_HEADS)
    w_t = pltpu.roll(jnp.where(is_w, r * w_scale, 0.0), LANES - IDX_DIM, 1).T
    wi_ref[0] = w_t[:IDX_HEADS]


def _idx(h, w_idx, kgain, tables, *, batch, tm=512):
    n, d = h.shape
    cols = w_idx.shape[1]
    s = n // batch
    tm = min(tm, s)
    per_b = s // tm
    w_scale = (IDX_HEADS ** -0.5) * (IDX_DIM ** -0.5)
    return pl.pallas_call(
        functools.partial(_idx_kernel, w_scale=w_scale),
        grid=(n // tm,),
        in_specs=[
            pl.BlockSpec((tm, d), lambda i: (i, 0)),
            pl.BlockSpec((d, cols), lambda i: (0, 0)),
            pl.BlockSpec((1, LANES), lambda i: (0, 0)),
            pl.BlockSpec((tm, LANES), lambda i: (i, 0)),
            pl.BlockSpec((tm, LANES), lambda i: (i, 0)),
            pl.BlockSpec((tm, LANES), lambda i: (i, 0)),
        ],
        out_specs=[
            pl.BlockSpec((1, IDX_HEADS, IDX_DIM, tm),
                         lambda i: (i // per_b, 0, 0, i % per_b)),
            pl.BlockSpec((tm, 2 * LANES), lambda i: (i, 0)),
            pl.BlockSpec((1, IDX_HEADS, tm), lambda i: (i // per_b, 0, i % per_b)),
        ],
        out_shape=[
            jax.ShapeDtypeStruct((batch, IDX_HEADS, IDX_DIM, s), F32),
            jax.ShapeDtypeStruct((n, 2 * LANES), BF16),
            jax.ShapeDtypeStruct((batch, IDX_HEADS, s), F32),
        ],
        compiler_params=_params("parallel"),
        name="idx_proj",
    )(h, w_idx, kgain, *tables)


def _sgu_kernel(h_ref, w_ref, gv_ref, ws_ref, bs_ref, o_ref):
    tm = h_ref.shape[0]
    groups = tm // SGU_LEN
    res = jnp.dot(h_ref[...], w_ref[...], preferred_element_type=F32)
    u = jax.nn.gelu(res[:, :SGU_WIDTH])
    v = _rms(jax.nn.gelu(res[:, SGU_WIDTH:]), gv_ref[...]).astype(BF16)
    row = lax.broadcasted_iota(I32, (SGU_LEN, SGU_LEN), 0)
    col = lax.broadcasted_iota(I32, (SGU_LEN, SGU_LEN), 1)
    causal = (col // CHUNK) <= (row // CHUNK)
    for g in range(SGU_GROUPS):
        cs = slice(g * SGU_GROUP_DIM, (g + 1) * SGU_GROUP_DIM)
        wg = jnp.where(causal, ws_ref[g], 0.0).astype(BF16)
        vg = jnp.concatenate(
            [v[n * SGU_LEN:(n + 1) * SGU_LEN, cs] for n in range(groups)], axis=1)
        mixed = jnp.dot(wg, vg, preferred_element_type=F32) + bs_ref[g]
        for n in range(groups):
            rs = slice(n * SGU_LEN, (n + 1) * SGU_LEN)
            m = mixed[:, n * SGU_GROUP_DIM:(n + 1) * SGU_GROUP_DIM]
            o_ref[rs, cs] = (u[rs, cs] * m).astype(BF16)


def _sgu(h, w_sgu, gv, w_s, b_s, *, tm=512):
    n, d = h.shape
    tm = min(tm, n)
    return pl.pallas_call(
        _sgu_kernel,
        grid=(n // tm,),
        in_specs=[
            pl.BlockSpec((tm, d), lambda i: (i, 0)),
            pl.BlockSpec((d, 2 * SGU_WIDTH), lambda i: (0, 0)),
            pl.BlockSpec((1, SGU_WIDTH), lambda i: (0, 0)),
            pl.BlockSpec((SGU_GROUPS, SGU_LEN, SGU_LEN), lambda i: (0, 0, 0)),
            pl.BlockSpec((SGU_GROUPS, SGU_LEN, 1), lambda i: (0, 0, 0)),
        ],
        out_specs=pl.BlockSpec((tm, SGU_WIDTH), lambda i: (i, 0)),
        out_shape=jax.ShapeDtypeStruct((n, SGU_WIDTH), BF16),
        compiler_params=_params("parallel"),
        name="sgu_branch",
    )(h, w_sgu, gv, w_s, b_s)


def _gates_kernel(h_ref, w_ref, o_ref):
    res = jnp.dot(h_ref[...], w_ref[...], preferred_element_type=F32)
    o_ref[...] = jax.nn.sigmoid(res).astype(BF16)


def _gates(h, w_g, *, tm=1024, tn=1024):
    n, d = h.shape
    cols = w_g.shape[1]
    tm = min(tm, n)
    return pl.pallas_call(
        _gates_kernel,
        grid=(n // tm, cols // tn),
        in_specs=[
            pl.BlockSpec((tm, d), lambda i, j: (i, 0)),
            pl.BlockSpec((d, tn), lambda i, j: (0, j)),
        ],
        out_specs=pl.BlockSpec((tm, tn), lambda i, j: (i, j)),
        out_shape=jax.ShapeDtypeStruct((n, cols), BF16),
        compiler_params=_params("parallel", "arbitrary"),
        name="gates_proj",
    )(h, w_g)


def _select_kernel(qi_ref, wi_ref, kcat_ref, mask_ref, qcat_ref, s_ref, *, topk):
    t = SEL_T
    n_tiles = s_ref.shape[0]
    i = pl.program_id(1)
    n_live = i + 1

    for h in range(IDX_HEADS):
        q = qi_ref[0, h]
        hi = q.astype(BF16)
        lo = (q - hi.astype(F32)).astype(BF16)
        qcat_ref[h] = jnp.concatenate([hi, hi, lo, lo], axis=0)
    w = wi_ref[0]

    def score_tile(kt, mx, mn, diagonal):
        kc = kcat_ref[0, pl.ds(pl.multiple_of(kt * t, t), t), :]
        acc = jnp.zeros((t, t), F32)
        for h in range(IDX_HEADS):
            d = jnp.dot(kc, qcat_ref[h], preferred_element_type=F32)
            acc = acc + jnp.maximum(d, 0.0) * w[h:h + 1, :]
        if diagonal:
            admissible = ((lax.broadcasted_iota(I32, (t, t), 0) // CHUNK)
                          <= (lax.broadcasted_iota(I32, (t, t), 1) // CHUNK))
            s = jnp.where(admissible, acc, -jnp.inf)
            low = jnp.where(admissible, acc, jnp.inf)
        else:
            s = low = acc
        s_ref[kt] = s
        mx = jnp.maximum(mx, jnp.max(s, axis=0, keepdims=True))
        mn = jnp.minimum(mn, jnp.min(low, axis=0, keepdims=True))
        return mx, mn

    def score_body(j, carry):
        mx, mn = score_tile(2 * j, *carry, diagonal=False)
        return score_tile(jnp.minimum(2 * j + 1, i - 1), mx, mn, diagonal=False)

    mx, mn = lax.fori_loop(
        0, (i + 1) // 2, score_body,
        (jnp.full((1, t), -jnp.inf, F32), jnp.full((1, t), jnp.inf, F32)))
    mx, mn = score_tile(i, mx, mn, diagonal=True)

    n_adm = ((i * t + lax.broadcasted_iota(I32, (1, t), 1)) // CHUNK + 1) * CHUNK
    done0 = n_adm <= topk

    def count(c, strict=False):
        def add_tile(kt, weight, acc):
            s = s_ref[kt]
            ind = jnp.where(s > c if strict else s >= c, weight, 0.0)
            for r in range(t // COUNT_ROWS):
                acc = acc + ind[r * COUNT_ROWS:(r + 1) * COUNT_ROWS]
            return acc

        def count_body(j, acc):
            acc = add_tile(2 * j, 1.0, acc)
            second = 2 * j + 1
            return add_tile(jnp.minimum(second, n_live - 1),
                            jnp.where(second < n_live, 1.0, 0.0), acc)

        acc = lax.fori_loop(0, (n_live + 1) // 2, count_body,
                            jnp.zeros((COUNT_ROWS, t), F32))
        return jnp.sum(acc, axis=0, keepdims=True)

    def search_step(st):
        lo, hi, c_lo, thr, n_thr, live = st
        mid = 0.5 * lo + 0.5 * hi
        stuck = (mid <= lo) | (mid >= hi)
        cnt = count(mid)
        finish = (live > 0.0) & (stuck | (cnt == topk))
        thr = jnp.where(finish, jnp.where(stuck, lo, mid), thr)
        n_thr = jnp.where(finish, jnp.where(stuck, c_lo, cnt), n_thr)
        live = jnp.where(finish, 0.0, live)
        above = cnt > topk
        lo = jnp.where(above, mid, lo)
        c_lo = jnp.where(above, cnt, c_lo)
        hi = jnp.where(cnt < topk, mid, hi)
        return lo, hi, c_lo, thr, n_thr, live

    def search_cond(st):
        return (st[-2] > 0.0) & (st[-1] < MAX_PASSES)

    def search_body(st):
        passes = st[-1]
        st = search_step(st[:-2])
        return st + (jnp.sum(st[-1]), passes + 1)

    hi0 = jnp.minimum(mx + jnp.maximum(jnp.abs(mx), ABOVE_MAX_FLOOR) * ABOVE_MAX_REL,
                      jnp.finfo(F32).max)
    live0 = jnp.where(done0, 0.0, 1.0)
    st = (mn, hi0, n_adm.astype(F32), jnp.full((1, t), -jnp.inf, F32),
          jnp.zeros((1, t), F32), live0)
    st = lax.fori_loop(0, UNCHECKED_PASSES, lambda _, s_: search_step(s_), st)
    st = lax.while_loop(search_cond, search_body,
                        st + (jnp.sum(st[-1]), jnp.int32(UNCHECKED_PASSES)))
    thr, n_thr = st[3], st[4]
    tied = n_thr > topk
    any_tied = jnp.sum(jnp.where(tied, 1.0, 0.0)) > 0.0

    @pl.when(jnp.logical_not(any_tied))
    def _():
        def mask_body(kt, carry):
            s = s_ref[kt]
            sel = (s >= thr) & (s > -jnp.inf)
            mask_ref[0, 0, kt] = jnp.where(sel, 1, 0).astype(jnp.int8)
            return carry

        lax.fori_loop(0, n_live, mask_body, 0)

    @pl.when(any_tied)
    def _():
        room = topk - count(thr, strict=True)
        tri = jnp.where(lax.broadcasted_iota(I32, (t, t), 0)
                        >= lax.broadcasted_iota(I32, (t, t), 1), 1.0, 0.0).astype(BF16)

        def tie_body(kt, seen):
            s = s_ref[kt]
            eq = (s == thr) & tied
            eq_f = jnp.where(eq, 1.0, 0.0)
            upto = jnp.dot(tri, eq_f.astype(BF16), preferred_element_type=F32)
            rank = seen + upto - eq_f
            keep = (eq & (rank < room)) | (jnp.logical_not(eq) & (s >= thr))
            sel = keep & (s > -jnp.inf)
            mask_ref[0, 0, kt] = jnp.where(sel, 1, 0).astype(jnp.int8)
            return seen + upto[t - 1:t, :]

        lax.fori_loop(0, n_live, tie_body, jnp.zeros((1, t), F32))

    def zero_body(kt, carry):
        mask_ref[0, 0, kt] = jnp.zeros((t, t), jnp.int8)
        return carry

    lax.fori_loop(n_live, n_tiles, zero_body, 0)


def _select(qi_t, wi_t, kcat, *, topk):
    b, heads, dh, s = qi_t.shape
    t = SEL_T
    nq = s // t
    return pl.pallas_call(
        functools.partial(_select_kernel, topk=topk),
        grid=(b, nq),
        in_specs=[
            pl.BlockSpec((1, heads, dh, t), lambda bi, i: (bi, 0, 0, i)),
            pl.BlockSpec((1, heads, t), lambda bi, i: (bi, 0, i)),
            pl.BlockSpec((1, s, 4 * IDX_DIM), lambda bi, i: (bi, 0, 0)),
        ],
        out_specs=pl.BlockSpec((1, 1, nq, t, t), lambda bi, i: (bi, i, 0, 0, 0)),
        out_shape=jax.ShapeDtypeStruct((b, nq, nq, t, t), jnp.int8),
        scratch_shapes=[
            pltpu.VMEM((heads, 4 * IDX_DIM, t), BF16),
            pltpu.VMEM((nq, t, t), F32),
        ],
        compiler_params=_params("parallel", "arbitrary"),
        name="index_select",
    )(qi_t, wi_t, kcat)


def _attn_kernel(qt_ref, k_ref, vt_ref, mask_ref, o_ref,
                 m_ref, acc_ref, s_ref):
    t = ATT_T
    i = pl.program_id(2)
    n = (i + 1) * (ATT_Q // t)
    m_ref[...] = jnp.full(m_ref.shape, -jnp.inf, F32)
    acc_ref[...] = jnp.zeros(acc_ref.shape, F32)
    q_t = qt_ref[0]

    def logits(kt, slot):
        rows = pl.ds(pl.multiple_of(kt * t, t), t)
        for hh in range(ATT_HEADS):
            hs = slice(hh * HEAD_DIM, (hh + 1) * HEAD_DIM)
            s_ref[slot, hh] = jnp.dot(k_ref[0, hh, rows, :], q_t[hs, :],
                                      preferred_element_type=F32)

    def update(kt, slot, off):
        v_t = vt_ref[0, kt]
        sel = jnp.concatenate([mask_ref[0, a, kt] for a in range(ATT_Q // t)],
                              axis=1).astype(I32) > off
        for hh in range(ATT_HEADS):
            hs = slice(hh * HEAD_DIM, (hh + 1) * HEAD_DIM)
            s = jnp.where(sel, s_ref[slot, hh], MASKED_LOGIT)
            m_prev = m_ref[hh]
            m_new = jnp.maximum(m_prev, jnp.max(s, axis=0, keepdims=True))
            p = jnp.exp2(s - m_new)
            alpha = jnp.exp2(m_prev - m_new)
            m_ref[hh] = m_new
            lhs = jnp.concatenate([v_t[hs, :], jnp.ones((DENOM_ROWS, t), BF16)], axis=0)
            pv = jnp.dot(lhs, p.astype(BF16), preferred_element_type=F32)
            acc_ref[hh] = alpha * acc_ref[hh] + pv

    logits(0, 0)

    def body(j, carry):
        first = 2 * j
        second = jnp.minimum(first + 1, n - 1)
        logits(second, 1)
        update(first, 0, 0)
        logits(jnp.minimum(first + 2, n - 1), 0)
        update(second, 1, jnp.where(first + 1 < n, 0, 1))
        return carry

    lax.fori_loop(0, (n + 1) // 2, body, 0)
    for hh in range(ATT_HEADS):
        acc = acc_ref[hh]
        o_ref[0, hh * HEAD_DIM:(hh + 1) * HEAD_DIM, :] = (
            acc[:HEAD_DIM] / acc[HEAD_DIM:HEAD_DIM + 1]).astype(BF16)


def _attn(q_t, k, v_t, mask):
    b, _, s, _ = k.shape
    t = ATT_T
    tq = ATT_Q
    nk = s // t
    width = ATT_HEADS * HEAD_DIM
    return pl.pallas_call(
        _attn_kernel,
        grid=(b, N_HEADS // ATT_HEADS, s // tq),
        in_specs=[
            pl.BlockSpec((1, width, tq), lambda bi, hp, i: (bi, hp, i)),
            pl.BlockSpec((1, ATT_HEADS, s, HEAD_DIM), lambda bi, hp, i: (bi, hp, 0, 0)),
            pl.BlockSpec((1, nk, width, t), lambda bi, hp, i: (bi, 0, hp, 0)),
            pl.BlockSpec((1, tq // t, nk, t, t), lambda bi, hp, i: (bi, i, 0, 0, 0)),
        ],
        out_specs=pl.BlockSpec((1, width, tq), lambda bi, hp, i: (bi, hp, i)),
        out_shape=jax.ShapeDtypeStruct((b, ATT_WIDTH, s), BF16),
        scratch_shapes=[
            pltpu.VMEM((ATT_HEADS, 1, tq), F32),
            pltpu.VMEM((ATT_HEADS, HEAD_DIM + DENOM_ROWS, tq), F32),
            pltpu.VMEM((2, ATT_HEADS, t, tq), F32),
        ],
        compiler_params=_params("parallel", "parallel", "arbitrary"),
        name="masked_attention",
    )(q_t, k, v_t, mask)


def _merge_kernel(ya_ref, yb_ref, g_ref, x_ref, wa_ref, wb_ref, wo_ref, o_ref):
    d = x_ref.shape[1]
    a = lax.dot_general(ya_ref[0], wa_ref[...], (((0,), (0,)), ((), ())),
                        preferred_element_type=F32)
    b = jnp.dot(yb_ref[...], wb_ref[...], preferred_element_type=F32)
    merged = g_ref[:, :d].astype(F32) * a + g_ref[:, d:].astype(F32) * b
    o_ref[...] = x_ref[...] + jnp.dot(merged.astype(BF16), wo_ref[...],
                                      preferred_element_type=F32)


def _merge(ya_t, yb, gates, x, wa, wb, wo, *, tm=256):
    n, d = x.shape
    batch, width, s = ya_t.shape
    tm = min(tm, s)
    per_b = s // tm
    return pl.pallas_call(
        _merge_kernel,
        grid=(n // tm,),
        in_specs=[
            pl.BlockSpec((1, width, tm), lambda i: (i // per_b, 0, i % per_b)),
            pl.BlockSpec((tm, yb.shape[1]), lambda i: (i, 0)),
            pl.BlockSpec((tm, 2 * d), lambda i: (i, 0)),
            pl.BlockSpec((tm, d), lambda i: (i, 0)),
            pl.BlockSpec(wa.shape, lambda i: (0, 0)),
            pl.BlockSpec(wb.shape, lambda i: (0, 0)),
            pl.BlockSpec(wo.shape, lambda i: (0, 0)),
        ],
        out_specs=pl.BlockSpec((tm, d), lambda i: (i, 0)),
        out_shape=jax.ShapeDtypeStruct((n, d), F32),
        compiler_params=_params("parallel"),
        name="merge_out",
    )(ya_t, yb, gates, x, wa, wb, wo)


def _inv_freq_lanes():
    def inv(rot_dim):
        f = ROPE_THETA ** (-jnp.arange(0, rot_dim, 2, dtype=F32) / rot_dim)
        return jnp.concatenate([f, f])
    lanes = jnp.concatenate([inv(ROT_DIM), inv(IDX_ROT_DIM),
                             jnp.zeros((LANES - ROT_DIM - IDX_ROT_DIM,), F32)])
    return lanes[None, :]


def _layer(x, tables_att, tables_idx, p):
    b, s, d = x.shape
    n = b * s
    topk = min(TOPK_MAX, s // 4)
    x = x.reshape(n, d)

    x1, h = _ffn(x, p["norm_ffn1"][None], p["ffn1_w_gate"].astype(BF16),
                 p["ffn1_w_up"].astype(BF16), p["ffn1_w_down"].astype(BF16),
                 p["norm_mix"][None], emit_norm=True)

    w_in = p["w_in"]
    c_qkv = 3 * ATT_WIDTH
    c_qi = c_qkv + IDX_HEADS * IDX_DIM
    c_idx = c_qi + IDX_DIM + IDX_HEADS
    c_sgu = c_idx + 2 * SGU_WIDTH
    w_q = w_in[:, :ATT_WIDTH].astype(BF16)
    w_k = w_in[:, ATT_WIDTH:2 * ATT_WIDTH].astype(BF16)
    w_v = w_in[:, 2 * ATT_WIDTH:c_qkv].astype(BF16)
    w_idx = jnp.pad(w_in[:, c_qkv:c_idx],
                    ((0, 0), (0, LANES - IDX_DIM - IDX_HEADS))).astype(BF16)
    w_sgu = w_in[:, c_idx:c_sgu].astype(BF16)
    w_gate = w_in[:, c_sgu:].astype(BF16)

    q_gain = (p["q_norm"] * (HEAD_DIM ** -0.5 * LOG2_E))[None]
    q_t = _head_proj(h, w_q, q_gain, tables_att, batch=b, transpose=True, name="q_proj")
    k = _head_proj(h, w_k, p["k_norm"][None], tables_att, batch=b, transpose=False,
                   name="k_proj")
    v_t = _v_proj(h, w_v, batch=b)
    kgain = jnp.pad(p["idx_k_norm"], (0, LANES - IDX_DIM))[None]
    qi_t, kcat, wi_t = _idx(h, w_idx, kgain, tables_idx, batch=b)
    yb = _sgu(h, w_sgu, p["sgu_v_norm"][None], p["sgu_w_s"], p["sgu_b_s"][:, :, None])
    gates = _gates(h, w_gate)

    mask = _select(qi_t, wi_t, kcat.reshape(b, s, 4 * IDX_DIM), topk=topk)
    ya_t = _attn(q_t, k, v_t, mask)

    x2 = _merge(ya_t, yb, gates, x1, p["w_up_attn"].astype(BF16),
                p["w_up_sgu"].astype(BF16), p["w_out"].astype(BF16))
    out = _ffn(x2, p["norm_ffn2"][None], p["ffn2_w_gate"].astype(BF16),
               p["ffn2_w_up"].astype(BF16), p["ffn2_w_down"].astype(BF16),
               p["norm_ffn2"][None], emit_norm=False)
    return out.reshape(b, s, d)


def kernel(x, positions, norm_ffn1, ffn1_w_gate, ffn1_w_up, ffn1_w_down, norm_mix, w_in,
           q_norm, k_norm, idx_k_norm, sgu_v_norm, sgu_w_s, sgu_b_s, w_up_attn, w_up_sgu,
           w_out, norm_ffn2, ffn2_w_gate, ffn2_w_up, ffn2_w_down):
    params = dict(
        norm_ffn1=norm_ffn1, ffn1_w_gate=ffn1_w_gate, ffn1_w_up=ffn1_w_up,
        ffn1_w_down=ffn1_w_down, norm_mix=norm_mix, w_in=w_in, q_norm=q_norm,
        k_norm=k_norm, idx_k_norm=idx_k_norm, sgu_v_norm=sgu_v_norm, sgu_w_s=sgu_w_s,
        sgu_b_s=sgu_b_s, w_up_attn=w_up_attn, w_up_sgu=w_up_sgu, w_out=w_out,
        norm_ffn2=norm_ffn2, ffn2_w_gate=ffn2_w_gate, ffn2_w_up=ffn2_w_up,
        ffn2_w_down=ffn2_w_down)
    pos = positions.reshape(-1, 1).astype(I32)
    tables_att, tables_idx = _rope_table(pos, _inv_freq_lanes())
    for l in range(w_in.shape[0]):
        x = _layer(x, tables_att, tables_idx, {k: v[l] for k, v in params.items()})
    return x
```

```python
import functools

import jax
import jax.numpy as jnp
from jax import lax
from jax.experimental import pallas as pl
from jax.experimental.pallas import tpu as pltpu

F32 = jnp.float32
BF16 = jnp.bfloat16
I32 = jnp.int32

EPS = 1e-6
CHUNK = 64
N_HEADS = 8
HEAD_DIM = 128
ATT_WIDTH = N_HEADS * HEAD_DIM
ROPE_THETA = 500000.0
ROT_DIM = HEAD_DIM // 4
IDX_HEADS = 16
IDX_DIM = 64
IDX_ROT_DIM = IDX_DIM // 4
TOPK_MAX = 256
SGU_LEN = 128
SGU_GROUPS = 8
SGU_GROUP_DIM = 128
SGU_WIDTH = SGU_GROUPS * SGU_GROUP_DIM

LANES = 128
MXU_DIM = 256
VMEM_LIMIT_BYTES = 56 * 1024 * 1024
MAX_TOKEN_TILE = 1024

SEL_T = MXU_DIM
COUNT_ROWS = 16
UNCHECKED_PASSES = 16
MAX_PASSES = 320
ATT_T = 256
ATT_Q = 256
ATT_HEADS = 4
DENOM_ROWS = 16
ABOVE_MAX_REL = 1e-6
ABOVE_MAX_FLOOR = 1e-30
MASKED_LOGIT = -1e30
LOG2_E = 1.4426950408889634


def _params(*sem):
    return pltpu.CompilerParams(dimension_semantics=sem,
                                vmem_limit_bytes=VMEM_LIMIT_BYTES)


def _rms(t, gain):
    ms = jnp.mean(t * t, axis=-1, keepdims=True)
    return t * lax.rsqrt(ms + EPS) * gain


def _rope(t, c, s1, s2, half):
    return (t * c + pltpu.roll(t, LANES - half, 1) * s1
            + pltpu.roll(t, half, 1) * s2)


def _ffn_kernel(x_ref, g_ref, wg_ref, wu_ref, wd_ref, g2_ref, o_ref, *rest,
                emit_norm):
    if emit_norm:
        h_out_ref, hn_ref = rest
    else:
        (hn_ref,) = rest
    j = pl.program_id(1)

    @pl.when(j == 0)
    def _():
        hn_ref[...] = _rms(x_ref[...], g_ref[...]).astype(BF16)
        o_ref[...] = jnp.zeros_like(o_ref)

    h = hn_ref[...]
    a = jnp.dot(h, wg_ref[...], preferred_element_type=F32)
    b = jnp.dot(h, wu_ref[...], preferred_element_type=F32)
    act = (a * jax.nn.sigmoid(a) * b).astype(BF16)
    o_ref[...] += jnp.dot(act, wd_ref[...], preferred_element_type=F32)

    @pl.when(j == pl.num_programs(1) - 1)
    def _():
        y = x_ref[...] + 0.5 * o_ref[...]
        o_ref[...] = y
        if emit_norm:
            h_out_ref[...] = _rms(y, g2_ref[...]).astype(BF16)


def _ffn(x, gain, w_gate, w_up, w_down, gain2, *, emit_norm, tm=512, tf=512):
    n, d = x.shape
    f = w_gate.shape[1]
    tm = min(tm, n)
    out_shape = [jax.ShapeDtypeStruct((n, d), F32)]
    out_specs = [pl.BlockSpec((tm, d), lambda i, j: (i, 0))]
    if emit_norm:
        out_shape.append(jax.ShapeDtypeStruct((n, d), BF16))
        out_specs.append(pl.BlockSpec((tm, d), lambda i, j: (i, 0)))
    res = pl.pallas_call(
        functools.partial(_ffn_kernel, emit_norm=emit_norm),
        grid=(n // tm, f // tf),
        in_specs=[
            pl.BlockSpec((tm, d), lambda i, j: (i, 0)),
            pl.BlockSpec((1, d), lambda i, j: (0, 0)),
            pl.BlockSpec((d, tf), lambda i, j: (0, j)),
            pl.BlockSpec((d, tf), lambda i, j: (0, j)),
            pl.BlockSpec((tf, d), lambda i, j: (j, 0)),
            pl.BlockSpec((1, d), lambda i, j: (0, 0)),
        ],
        out_specs=out_specs,
        out_shape=out_shape,
        scratch_shapes=[pltpu.VMEM((tm, d), BF16)],
        compiler_params=_params("parallel", "arbitrary"),
        name="ffn_norm" if emit_norm else "ffn",
    )(x, gain, w_gate, w_up, w_down, gain2)
    return res if emit_norm else res[0]


def _rope_table_kernel(pos_ref, invf_ref, ca_ref, s1a_ref, s2a_ref, ci_ref, s1i_ref, s2i_ref):
    ang = pos_ref[...].astype(F32) * invf_ref[...]
    c = jnp.cos(ang)
    s = jnp.sin(ang)
    lane = lax.broadcasted_iota(I32, ang.shape, 1)
    in_att = lane < ROT_DIM
    ca_ref[...] = jnp.where(in_att, c, 1.0)
    s1a_ref[...] = jnp.where(lane < ROT_DIM // 2, -s, 0.0)
    s2a_ref[...] = jnp.where((lane >= ROT_DIM // 2) & in_att, s, 0.0)
    first = lane < IDX_DIM
    c_i = jnp.where(first, pltpu.roll(c, LANES - ROT_DIM, 1), pltpu.roll(c, IDX_DIM - ROT_DIM, 1))
    s_i = jnp.where(first, pltpu.roll(s, LANES - ROT_DIM, 1), pltpu.roll(s, IDX_DIM - ROT_DIM, 1))
    j = lane & (IDX_DIM - 1)
    in_idx = j < IDX_ROT_DIM
    ci_ref[...] = jnp.where(in_idx, c_i, 1.0)
    s1i_ref[...] = jnp.where(j < IDX_ROT_DIM // 2, -s_i, 0.0)
    s2i_ref[...] = jnp.where((j >= IDX_ROT_DIM // 2) & in_idx, s_i, 0.0)


def _rope_table(pos, invf, *, tm=MAX_TOKEN_TILE):
    n = pos.shape[0]
    tm = min(tm, n)
    spec = pl.BlockSpec((tm, LANES), lambda i: (i, 0))
    out = pl.pallas_call(
        _rope_table_kernel,
        grid=(n // tm,),
        in_specs=[pl.BlockSpec((tm, 1), lambda i: (i, 0)),
                  pl.BlockSpec((1, LANES), lambda i: (0, 0))],
        out_specs=[spec] * 6,
        out_shape=[jax.ShapeDtypeStruct((n, LANES), F32)] * 6,
        compiler_params=_params("parallel"),
        name="rope_tables",
    )(pos, invf)
    return out[:3], out[3:]


def _head_proj_kernel(h_ref, w_ref, gain_ref, c_ref, s1_ref, s2_ref, o_ref, r_ref,
                      *, transpose):
    half = ROT_DIM // 2
    res = jnp.dot(h_ref[...], w_ref[...], preferred_element_type=F32)
    heads = res.shape[1] // HEAD_DIM
    for hh in range(heads):
        r_ref[hh] = res[:, hh * HEAD_DIM:(hh + 1) * HEAD_DIM]

    def head(hh, carry):
        t = _rope(_rms(r_ref[hh], gain_ref[...]), c_ref[...], s1_ref[...], s2_ref[...], half)
        if transpose:
            o_ref[0, pl.ds(pl.multiple_of(hh * HEAD_DIM, HEAD_DIM), HEAD_DIM), :] = (
                t.T.astype(BF16))
        else:
            o_ref[0, hh] = t.astype(BF16)
        return carry

    lax.fori_loop(0, heads, head, 0)


def _head_proj(h, w, gain, tables, *, batch, transpose, name, tm=1024, tn=512):
    n, d = h.shape
    cols = w.shape[1]
    s = n // batch
    tm = min(tm, s)
    per_b = s // tm
    heads = tn // HEAD_DIM
    table_spec = pl.BlockSpec((tm, LANES), lambda i, j: (i, 0))
    if transpose:
        out_spec = pl.BlockSpec((1, tn, tm), lambda i, j: (i // per_b, j, i % per_b))
        out_shape = jax.ShapeDtypeStruct((batch, cols, s), BF16)
    else:
        out_spec = pl.BlockSpec((1, heads, tm, HEAD_DIM),
                                lambda i, j: (i // per_b, j, i % per_b, 0))
        out_shape = jax.ShapeDtypeStruct((batch, cols // HEAD_DIM, s, HEAD_DIM), BF16)
    return pl.pallas_call(
        functools.partial(_head_proj_kernel, transpose=transpose),
        grid=(n // tm, cols // tn),
        in_specs=[
            pl.BlockSpec((tm, d), lambda i, j: (i, 0)),
            pl.BlockSpec((d, tn), lambda i, j: (0, j)),
            pl.BlockSpec((1, HEAD_DIM), lambda i, j: (0, 0)),
            table_spec, table_spec, table_spec,
        ],
        out_specs=out_spec,
        out_shape=out_shape,
        scratch_shapes=[pltpu.VMEM((heads, tm, HEAD_DIM), F32)],
        compiler_params=_params("parallel", "arbitrary"),
        name=name,
    )(h, w, gain, *tables)


def _v_proj_kernel(h_ref, w_ref, o_ref):
    res = jnp.dot(h_ref[...], w_ref[...], preferred_element_type=F32)
    for a in range(res.shape[0] // ATT_T):
        o_ref[0, a] = res[a * ATT_T:(a + 1) * ATT_T, :].T.astype(BF16)


def _v_proj(h, w, *, batch, tm=1024, tn=512):
    n, d = h.shape
    cols = w.shape[1]
    s = n // batch
    tm = min(tm, s)
    per_b = s // tm
    return pl.pallas_call(
        _v_proj_kernel,
        grid=(n // tm, cols // tn),
        in_specs=[
            pl.BlockSpec((tm, d), lambda i, j: (i, 0)),
            pl.BlockSpec((d, tn), lambda i, j: (0, j)),
        ],
        out_specs=pl.BlockSpec((1, tm // ATT_T, tn, ATT_T),
                               lambda i, j: (i // per_b, i % per_b, j, 0)),
        out_shape=jax.ShapeDtypeStruct((batch, s // ATT_T, cols, ATT_T), BF16),
        compiler_params=_params("parallel", "arbitrary"),
        name="v_proj",
    )(h, w)


def _idx_kernel(h_ref, w_ref, kgain_ref, c_ref, s1_ref, s2_ref,
                qi_ref, kcat_ref, wi_ref, *, w_scale):
    half = IDX_ROT_DIM // 2
    qw = IDX_HEADS * IDX_DIM
    res = jnp.dot(h_ref[...], w_ref[...], preferred_element_type=F32)
    c, s1, s2 = c_ref[...], s1_ref[...], s2_ref[...]
    for t in range(qw // LANES):
        sl = slice(t * LANES, (t + 1) * LANES)
        q_t = _rope(res[:, sl], c, s1, s2, half).T
        qi_ref[0, 2 * t] = q_t[:IDX_DIM]
        qi_ref[0, 2 * t + 1] = q_t[IDX_DIM:]

    r = res[:, qw:qw + LANES]
    lane = lax.broadcasted_iota(I32, r.shape, 1)
    is_k = lane < IDX_DIM
    ms = jnp.sum(jnp.where(is_k, r * r, 0.0), axis=-1, keepdims=True) * (1.0 / IDX_DIM)
    kn = r * lax.rsqrt(ms + EPS) * kgain_ref[...]
    kr = _rope(kn, jnp.where(is_k, c, 1.0), jnp.where(is_k, s1, 0.0),
               jnp.where(is_k, s2, 0.0), half)
    hi = kr.astype(BF16).astype(F32)
    lo = kr - hi
    hi_lo = jnp.where(is_k, hi, pltpu.roll(lo, IDX_DIM, 1)).astype(BF16)
    kcat_ref[:, 0:LANES] = hi_lo
    kcat_ref[:, LANES:2 * LANES] = hi_lo
    is_w = (lane >= IDX_DIM) & (lane < IDX_DIM + IDX_HEADS)
    w_t = pltpu.roll(jnp.where(is_w, r * w_scale, 0.0), LANES - IDX_DIM, 1).T
    wi_ref[0] = w_t[:IDX_HEADS]


def _idx(h, w_idx, kgain, tables, *, batch, tm=512):
    n, d = h.shape
    cols = w_idx.shape[1]
    s = n // batch
    tm = min(tm, s)
    per_b = s // tm
    w_scale = (IDX_HEADS ** -0.5) * (IDX_DIM ** -0.5)
    return pl.pallas_call(
        functools.partial(_idx_kernel, w_scale=w_scale),
        grid=(n // tm,),
        in_specs=[
            pl.BlockSpec((tm, d), lambda i: (i, 0)),
            pl.BlockSpec((d, cols), lambda i: (0, 0)),
            pl.BlockSpec((1, LANES), lambda i: (0, 0)),
            pl.BlockSpec((tm, LANES), lambda i: (i, 0)),
            pl.BlockSpec((tm, LANES), lambda i: (i, 0)),
            pl.BlockSpec((tm, LANES), lambda i: (i, 0)),
        ],
        out_specs=[
            pl.BlockSpec((1, IDX_HEADS, IDX_DIM, tm),
                         lambda i: (i // per_b, 0, 0, i % per_b)),
            pl.BlockSpec((tm, 2 * LANES), lambda i: (i, 0)),
            pl.BlockSpec((1, IDX_HEADS, tm), lambda i: (i // per_b, 0, i % per_b)),
        ],
        out_shape=[
            jax.ShapeDtypeStruct((batch, IDX_HEADS, IDX_DIM, s), F32),
            jax.ShapeDtypeStruct((n, 2 * LANES), BF16),
            jax.ShapeDtypeStruct((batch, IDX_HEADS, s), F32),
        ],
        compiler_params=_params("parallel"),
        name="idx_proj",
    )(h, w_idx, kgain, *tables)


def _sgu_kernel(h_ref, w_ref, gv_ref, ws_ref, bs_ref, o_ref):
    tm = h_ref.shape[0]
    groups = tm // SGU_LEN
    res = jnp.dot(h_ref[...], w_ref[...], preferred_element_type=F32)
    u = jax.nn.gelu(res[:, :SGU_WIDTH])
    v = _rms(jax.nn.gelu(res[:, SGU_WIDTH:]), gv_ref[...]).astype(BF16)
    row = lax.broadcasted_iota(I32, (SGU_LEN, SGU_LEN), 0)
    col = lax.broadcasted_iota(I32, (SGU_LEN, SGU_LEN), 1)
    causal = (col // CHUNK) <= (row // CHUNK)
    for g in range(SGU_GROUPS):
        cs = slice(g * SGU_GROUP_DIM, (g + 1) * SGU_GROUP_DIM)
        wg = jnp.where(causal, ws_ref[g], 0.0).astype(BF16)
        vg = jnp.concatenate(
            [v[n * SGU_LEN:(n + 1) * SGU_LEN, cs] for n in range(groups)], axis=1)
        mixed = jnp.dot(wg, vg, preferred_element_type=F32) + bs_ref[g]
        for n in range(groups):
            rs = slice(n * SGU_LEN, (n + 1) * SGU_LEN)
            m = mixed[:, n * SGU_GROUP_DIM:(n + 1) * SGU_GROUP_DIM]
            o_ref[rs, cs] = (u[rs, cs] * m).astype(BF16)


def _sgu(h, w_sgu, gv, w_s, b_s, *, tm=512):
    n, d = h.shape
    tm = min(tm, n)
    return pl.pallas_call(
        _sgu_kernel,
        grid=(n // tm,),
        in_specs=[
            pl.BlockSpec((tm, d), lambda i: (i, 0)),
            pl.BlockSpec((d, 2 * SGU_WIDTH), lambda i: (0, 0)),
            pl.BlockSpec((1, SGU_WIDTH), lambda i: (0, 0)),
            pl.BlockSpec((SGU_GROUPS, SGU_LEN, SGU_LEN), lambda i: (0, 0, 0)),
            pl.BlockSpec((SGU_GROUPS, SGU_LEN, 1), lambda i: (0, 0, 0)),
        ],
        out_specs=pl.BlockSpec((tm, SGU_WIDTH), lambda i: (i, 0)),
        out_shape=jax.ShapeDtypeStruct((n, SGU_WIDTH), BF16),
        compiler_params=_params("parallel"),
        name="sgu_branch",
    )(h, w_sgu, gv, w_s, b_s)


def _gates_kernel(h_ref, w_ref, o_ref):
    res = jnp.dot(h_ref[...], w_ref[...], preferred_element_type=F32)
    o_ref[...] = jax.nn.sigmoid(res).astype(BF16)


def _gates(h, w_g, *, tm=1024, tn=1024):
    n, d = h.shape
    cols = w_g.shape[1]
    tm = min(tm, n)
    return pl.pallas_call(
        _gates_kernel,
        grid=(n // tm, cols // tn),
        in_specs=[
            pl.BlockSpec((tm, d), lambda i, j: (i, 0)),
            pl.BlockSpec((d, tn), lambda i, j: (0, j)),
        ],
        out_specs=pl.BlockSpec((tm, tn), lambda i, j: (i, j)),
        out_shape=jax.ShapeDtypeStruct((n, cols), BF16),
        compiler_params=_params("parallel", "arbitrary"),
        name="gates_proj",
    )(h, w_g)


def _select_kernel(qi_ref, wi_ref, kcat_ref, mask_ref, qcat_ref, s_ref, *, topk):
    t = SEL_T
    n_tiles = s_ref.shape[0]
    i = pl.program_id(1)
    n_live = i + 1

    for h in range(IDX_HEADS):
        q = qi_ref[0, h]
        hi = q.astype(BF16)
        lo = (q - hi.astype(F32)).astype(BF16)
        qcat_ref[h] = jnp.concatenate([hi, hi, lo, lo], axis=0)
    w = wi_ref[0]

    def score_tile(kt, mx, mn, diagonal):
        kc = kcat_ref[0, pl.ds(pl.multiple_of(kt * t, t), t), :]
        acc = jnp.zeros((t, t), F32)
        for h in range(IDX_HEADS):
            d = jnp.dot(kc, qcat_ref[h], preferred_element_type=F32)
            acc = acc + jnp.maximum(d, 0.0) * w[h:h + 1, :]
        if diagonal:
            admissible = ((lax.broadcasted_iota(I32, (t, t), 0) // CHUNK)
                          <= (lax.broadcasted_iota(I32, (t, t), 1) // CHUNK))
            s = jnp.where(admissible, acc, -jnp.inf)
            low = jnp.where(admissible, acc, jnp.inf)
        else:
            s = low = acc
        s_ref[kt] = s
        mx = jnp.maximum(mx, jnp.max(s, axis=0, keepdims=True))
        mn = jnp.minimum(mn, jnp.min(low, axis=0, keepdims=True))
        return mx, mn

    def score_body(j, carry):
        mx, mn = score_tile(2 * j, *carry, diagonal=False)
        return score_tile(jnp.minimum(2 * j + 1, i - 1), mx, mn, diagonal=False)

    mx, mn = lax.fori_loop(
        0, (i + 1) // 2, score_body,
        (jnp.full((1, t), -jnp.inf, F32), jnp.full((1, t), jnp.inf, F32)))
    mx, mn = score_tile(i, mx, mn, diagonal=True)

    n_adm = ((i * t + lax.broadcasted_iota(I32, (1, t), 1)) // CHUNK + 1) * CHUNK
    done0 = n_adm <= topk

    def count(c, strict=False):
        def add_tile(kt, weight, acc):
            s = s_ref[kt]
            ind = jnp.where(s > c if strict else s >= c, weight, 0.0)
            for r in range(t // COUNT_ROWS):
                acc = acc + ind[r * COUNT_ROWS:(r + 1) * COUNT_ROWS]
            return acc

        def count_body(j, acc):
            acc = add_tile(2 * j, 1.0, acc)
            second = 2 * j + 1
            return add_tile(jnp.minimum(second, n_live - 1),
                            jnp.where(second < n_live, 1.0, 0.0), acc)

        acc = lax.fori_loop(0, (n_live + 1) // 2, count_body,
                            jnp.zeros((COUNT_ROWS, t), F32))
        return jnp.sum(acc, axis=0, keepdims=True)

    def search_step(st):
        lo, hi, c_lo, thr, n_thr, live = st
        mid = 0.5 * lo + 0.5 * hi
        stuck = (mid <= lo) | (mid >= hi)
        cnt = count(mid)
        finish = (live > 0.0) & (stuck | (cnt == topk))
        thr = jnp.where(finish, jnp.where(stuck, lo, mid), thr)
        n_thr = jnp.where(finish, jnp.where(stuck, c_lo, cnt), n_thr)
        live = jnp.where(finish, 0.0, live)
        above = cnt > topk
        lo = jnp.where(above, mid, lo)
        c_lo = jnp.where(above, cnt, c_lo)
        hi = jnp.where(cnt < topk, mid, hi)
        return lo, hi, c_lo, thr, n_thr, live

    def search_cond(st):
        return (st[-2] > 0.0) & (st[-1] < MAX_PASSES)

    def search_body(st):
        passes = st[-1]
        st = search_step(st[:-2])
        return st + (jnp.sum(st[-1]), passes + 1)

    hi0 = jnp.minimum(mx + jnp.maximum(jnp.abs(mx), ABOVE_MAX_FLOOR) * ABOVE_MAX_REL,
                      jnp.finfo(F32).max)
    live0 = jnp.where(done0, 0.0, 1.0)
    st = (mn, hi0, n_adm.astype(F32), jnp.full((1, t), -jnp.inf, F32),
          jnp.zeros((1, t), F32), live0)
    st = lax.fori_loop(0, UNCHECKED_PASSES, lambda _, s_: search_step(s_), st)
    st = lax.while_loop(search_cond, search_body,
                        st + (jnp.sum(st[-1]), jnp.int32(UNCHECKED_PASSES)))
    thr, n_thr = st[3], st[4]
    tied = n_thr > topk
    any_tied = jnp.sum(jnp.where(tied, 1.0, 0.0)) > 0.0

    @pl.when(jnp.logical_not(any_tied))
    def _():
        def mask_body(kt, carry):
            s = s_ref[kt]
            sel = (s >= thr) & (s > -jnp.inf)
            mask_ref[0, 0, kt] = jnp.where(sel, 1, 0).astype(jnp.int8)
            return carry

        lax.fori_loop(0, n_live, mask_body, 0)

    @pl.when(any_tied)
    def _():
        room = topk - count(thr, strict=True)
        tri = jnp.where(lax.broadcasted_iota(I32, (t, t), 0)
                        >= lax.broadcasted_iota(I32, (t, t), 1), 1.0, 0.0).astype(BF16)

        def tie_body(kt, seen):
            s = s_ref[kt]
            eq = (s == thr) & tied
            eq_f = jnp.where(eq, 1.0, 0.0)
            upto = jnp.dot(tri, eq_f.astype(BF16), preferred_element_type=F32)
            rank = seen + upto - eq_f
            keep = (eq & (rank < room)) | (jnp.logical_not(eq) & (s >= thr))
            sel = keep & (s > -jnp.inf)
            mask_ref[0, 0, kt] = jnp.where(sel, 1, 0).astype(jnp.int8)
            return seen + upto[t - 1:t, :]

        lax.fori_loop(0, n_live, tie_body, jnp.zeros((1, t), F32))

    def zero_body(kt, carry):
        mask_ref[0, 0, kt] = jnp.zeros((t, t), jnp.int8)
        return carry

    lax.fori_loop(n_live, n_tiles, zero_body, 0)


def _select(qi_t, wi_t, kcat, *, topk):
    b, heads, dh, s = qi_t.shape
    t = SEL_T
    nq = s // t
    return pl.pallas_call(
        functools.partial(_select_kernel, topk=topk),
        grid=(b, nq),
        in_specs=[
            pl.BlockSpec((1, heads, dh, t), lambda bi, i: (bi, 0, 0, i)),
            pl.BlockSpec((1, heads, t), lambda bi, i: (bi, 0, i)),
            pl.BlockSpec((1, s, 4 * IDX_DIM), lambda bi, i: (bi, 0, 0)),
        ],
        out_specs=pl.BlockSpec((1, 1, nq, t, t), lambda bi, i: (bi, i, 0, 0, 0)),
        out_shape=jax.ShapeDtypeStruct((b, nq, nq, t, t), jnp.int8),
        scratch_shapes=[
            pltpu.VMEM((heads, 4 * IDX_DIM, t), BF16),
            pltpu.VMEM((nq, t, t), F32),
        ],
        compiler_params=_params("parallel", "arbitrary"),
        name="index_select",
    )(qi_t, wi_t, kcat)


def _attn_kernel(qt_ref, k_ref, vt_ref, mask_ref, o_ref,
                 m_ref, acc_ref, s_ref):
    t = ATT_T
    i = pl.program_id(2)
    n = (i + 1) * (ATT_Q // t)
    m_ref[...] = jnp.full(m_ref.shape, -jnp.inf, F32)
    acc_ref[...] = jnp.zeros(acc_ref.shape, F32)
    q_t = qt_ref[0]

    def logits(kt, slot):
        rows = pl.ds(pl.multiple_of(kt * t, t), t)
        for hh in range(ATT_HEADS):
            hs = slice(hh * HEAD_DIM, (hh + 1) * HEAD_DIM)
            s_ref[slot, hh] = jnp.dot(k_ref[0, hh, rows, :], q_t[hs, :],
                                      preferred_element_type=F32)

    def update(kt, slot, off):
        v_t = vt_ref[0, kt]
        sel = jnp.concatenate([mask_ref[0, a, kt] for a in range(ATT_Q // t)],
                              axis=1).astype(I32) > off
        for hh in range(ATT_HEADS):
            hs = slice(hh * HEAD_DIM, (hh + 1) * HEAD_DIM)
            s = jnp.where(sel, s_ref[slot, hh], MASKED_LOGIT)
            m_prev = m_ref[hh]
            m_new = jnp.maximum(m_prev, jnp.max(s, axis=0, keepdims=True))
            p = jnp.exp2(s - m_new)
            alpha = jnp.exp2(m_prev - m_new)
            m_ref[hh] = m_new
            lhs = jnp.concatenate([v_t[hs, :], jnp.ones((DENOM_ROWS, t), BF16)], axis=0)
            pv = jnp.dot(lhs, p.astype(BF16), preferred_element_type=F32)
            acc_ref[hh] = alpha * acc_ref[hh] + pv

    logits(0, 0)

    def body(j, carry):
        first = 2 * j
        second = jnp.minimum(first + 1, n - 1)
        logits(second, 1)
        update(first, 0, 0)
        logits(jnp.minimum(first + 2, n - 1), 0)
        update(second, 1, jnp.where(first + 1 < n, 0, 1))
        return carry

    lax.fori_loop(0, (n + 1) // 2, body, 0)
    for hh in range(ATT_HEADS):
        acc = acc_ref[hh]
        o_ref[0, hh * HEAD_DIM:(hh + 1) * HEAD_DIM, :] = (
            acc[:HEAD_DIM] / acc[HEAD_DIM:HEAD_DIM + 1]).astype(BF16)


def _attn(q_t, k, v_t, mask):
    b, _, s, _ = k.shape
    t = ATT_T
    tq = ATT_Q
    nk = s // t
    width = ATT_HEADS * HEAD_DIM
    return pl.pallas_call(
        _attn_kernel,
        grid=(b, N_HEADS // ATT_HEADS, s // tq),
        in_specs=[
            pl.BlockSpec((1, width, tq), lambda bi, hp, i: (bi, hp, i)),
            pl.BlockSpec((1, ATT_HEADS, s, HEAD_DIM), lambda bi, hp, i: (bi, hp, 0, 0)),
            pl.BlockSpec((1, nk, width, t), lambda bi, hp, i: (bi, 0, hp, 0)),
            pl.BlockSpec((1, tq // t, nk, t, t), lambda bi, hp, i: (bi, i, 0, 0, 0)),
        ],
        out_specs=pl.BlockSpec((1, width, tq), lambda bi, hp, i: (bi, hp, i)),
        out_shape=jax.ShapeDtypeStruct((b, ATT_WIDTH, s), BF16),
        scratch_shapes=[
            pltpu.VMEM((ATT_HEADS, 1, tq), F32),
            pltpu.VMEM((ATT_HEADS, HEAD_DIM + DENOM_ROWS, tq), F32),
            pltpu.VMEM((2, ATT_HEADS, t, tq), F32),
        ],
        compiler_params=_params("parallel", "parallel", "arbitrary"),
        name="masked_attention",
    )(q_t, k, v_t, mask)


def _merge_kernel(ya_ref, yb_ref, g_ref, x_ref, wa_ref, wb_ref, wo_ref, o_ref):
    d = x_ref.shape[1]
    a = lax.dot_general(ya_ref[0], wa_ref[...], (((0,), (0,)), ((), ())),
                        preferred_element_type=F32)
    b = jnp.dot(yb_ref[...], wb_ref[...], preferred_element_type=F32)
    merged = g_ref[:, :d].astype(F32) * a + g_ref[:, d:].astype(F32) * b
    o_ref[...] = x_ref[...] + jnp.dot(merged.astype(BF16), wo_ref[...],
                                      preferred_element_type=F32)


def _merge(ya_t, yb, gates, x, wa, wb, wo, *, tm=256):
    n, d = x.shape
    batch, width, s = ya_t.shape
    tm = min(tm, s)
    per_b = s // tm
    return pl.pallas_call(
        _merge_kernel,
        grid=(n // tm,),
        in_specs=[
            pl.BlockSpec((1, width, tm), lambda i: (i // per_b, 0, i % per_b)),
            pl.BlockSpec((tm, yb.shape[1]), lambda i: (i, 0)),
            pl.BlockSpec((tm, 2 * d), lambda i: (i, 0)),
            pl.BlockSpec((tm, d), lambda i: (i, 0)),
            pl.BlockSpec(wa.shape, lambda i: (0, 0)),
            pl.BlockSpec(wb.shape, lambda i: (0, 0)),
            pl.BlockSpec(wo.shape, lambda i: (0, 0)),
        ],
        out_specs=pl.BlockSpec((tm, d), lambda i: (i, 0)),
        out_shape=jax.ShapeDtypeStruct((n, d), F32),
        compiler_params=_params("parallel"),
        name="merge_out",
    )(ya_t, yb, gates, x, wa, wb, wo)


def _inv_freq_lanes():
    def inv(rot_dim):
        f = ROPE_THETA ** (-jnp.arange(0, rot_dim, 2, dtype=F32) / rot_dim)
        return jnp.concatenate([f, f])
    lanes = jnp.concatenate([inv(ROT_DIM), inv(IDX_ROT_DIM),
                             jnp.zeros((LANES - ROT_DIM - IDX_ROT_DIM,), F32)])
    return lanes[None, :]


def _layer(x, tables_att, tables_idx, p):
    b, s, d = x.shape
    n = b * s
    topk = min(TOPK_MAX, s // 4)
    assert s % MAX_TOKEN_TILE == 0, "token tiles must not straddle a batch row"
    x = x.reshape(n, d)

    x1, h = _ffn(x, p["norm_ffn1"][None], p["ffn1_w_gate"].astype(BF16),
                 p["ffn1_w_up"].astype(BF16), p["ffn1_w_down"].astype(BF16),
                 p["norm_mix"][None], emit_norm=True)

    w_in = p["w_in"]
    c_qkv = 3 * ATT_WIDTH
    c_qi = c_qkv + IDX_HEADS * IDX_DIM
    c_idx = c_qi + IDX_DIM + IDX_HEADS
    c_sgu = c_idx + 2 * SGU_WIDTH
    w_q = w_in[:, :ATT_WIDTH].astype(BF16)
    w_k = w_in[:, ATT_WIDTH:2 * ATT_WIDTH].astype(BF16)
    w_v = w_in[:, 2 * ATT_WIDTH:c_qkv].astype(BF16)
    w_idx = jnp.pad(w_in[:, c_qkv:c_idx],
                    ((0, 0), (0, LANES - IDX_DIM - IDX_HEADS))).astype(BF16)
    w_sgu = w_in[:, c_idx:c_sgu].astype(BF16)
    w_gate = w_in[:, c_sgu:].astype(BF16)

    q_gain = (p["q_norm"] * (HEAD_DIM ** -0.5 * LOG2_E))[None]
    q_t = _head_proj(h, w_q, q_gain, tables_att, batch=b, transpose=True, name="q_proj")
    k = _head_proj(h, w_k, p["k_norm"][None], tables_att, batch=b, transpose=False,
                   name="k_proj")
    v_t = _v_proj(h, w_v, batch=b)
    kgain = jnp.pad(p["idx_k_norm"], (0, LANES - IDX_DIM))[None]
    qi_t, kcat, wi_t = _idx(h, w_idx, kgain, tables_idx, batch=b)
    yb = _sgu(h, w_sgu, p["sgu_v_norm"][None], p["sgu_w_s"], p["sgu_b_s"][:, :, None])
    gates = _gates(h, w_gate)

    mask = _select(qi_t, wi_t, kcat.reshape(b, s, 4 * IDX_DIM), topk=topk)
    ya_t = _attn(q_t, k, v_t, mask)

    x2 = _merge(ya_t, yb, gates, x1, p["w_up_attn"].astype(BF16),
                p["w_up_sgu"].astype(BF16), p["w_out"].astype(BF16))
    out = _ffn(x2, p["norm_ffn2"][None], p["ffn2_w_gate"].astype(BF16),
               p["ffn2_w_up"].astype(BF16), p["ffn2_w_down"].astype(BF16),
               p["norm_ffn2"][None], emit_norm=False)
    return out.reshape(b, s, d)


def kernel(x, positions, norm_ffn1, ffn1_w_gate, ffn1_w_up, ffn1_w_down, norm_mix, w_in,
           q_norm, k_norm, idx_k_norm, sgu_v_norm, sgu_w_s, sgu_b_s, w_up_attn, w_up_sgu,
           w_out, norm_ffn2, ffn2_w_gate, ffn2_w_up, ffn2_w_down):
    params = dict(
        norm_ffn1=norm_ffn1, ffn1_w_gate=ffn1_w_gate, ffn1_w_up=ffn1_w_up,
        ffn1_w_down=ffn1_w_down, norm_mix=norm_mix, w_in=w_in, q_norm=q_norm,
        k_norm=k_norm, idx_k_norm=idx_k_norm, sgu_v_norm=sgu_v_norm, sgu_w_s=sgu_w_s,
        sgu_b_s=sgu_b_s, w_up_attn=w_up_attn, w_up_sgu=w_up_sgu, w_out=w_out,
        norm_ffn2=norm_ffn2, ffn2_w_gate=ffn2_w_gate, ffn2_w_up=ffn2_w_up,
        ffn2_w_down=ffn2_w_down)
    pos = positions.reshape(-1, 1).astype(I32)
    tables_att, tables_idx = _rope_table(pos, _inv_freq_lanes())
    for l in range(w_in.shape[0]):
        x = _layer(x, tables_att, tables_idx, {k: v[l] for k, v in params.items()})
    return x
```

```python
import functools

import jax
import jax.numpy as jnp
from jax import lax
from jax.experimental import pallas as pl
from jax.experimental.pallas import tpu as pltpu

F32 = jnp.float32
BF16 = jnp.bfloat16
I32 = jnp.int32

EPS = 1e-6
CHUNK = 64
N_HEADS = 8
HEAD_DIM = 128
ATT_WIDTH = N_HEADS * HEAD_DIM
ROPE_THETA = 500000.0
ROT_DIM = HEAD_DIM // 4
IDX_HEADS = 16
IDX_DIM = 64
IDX_ROT_DIM = IDX_DIM // 4
TOPK_MAX = 256
SGU_LEN = 128
SGU_GROUPS = 8
SGU_GROUP_DIM = 128
SGU_WIDTH = SGU_GROUPS * SGU_GROUP_DIM

LANES = 128
MXU_DIM = 256
VMEM_LIMIT_BYTES = 56 * 1024 * 1024
MAX_TOKEN_TILE = 1024

SEL_T = MXU_DIM
COUNT_ROWS = 16
UNCHECKED_PASSES = 16
MAX_PASSES = 320
ATT_T = 256
ATT_Q = 256
ATT_HEADS = 4
DENOM_ROWS = 16
ABOVE_MAX_REL = 1e-6
ABOVE_MAX_FLOOR = 1e-30
MASKED_LOGIT = -1e30
LOG2_E = 1.4426950408889634


def _params(*sem):
    return pltpu.CompilerParams(dimension_semantics=sem,
                                vmem_limit_bytes=VMEM_LIMIT_BYTES)


def _rms(t, gain):
    ms = jnp.mean(t * t, axis=-1, keepdims=True)
    return t * lax.rsqrt(ms + EPS) * gain


def _rope(t, c, s1, s2, half):
    return (t * c + pltpu.roll(t, LANES - half, 1) * s1
            + pltpu.roll(t, half, 1) * s2)


def _ffn_up_kernel(x_ref, g_ref, wg_ref, wu_ref, o_ref, hn_ref):
    @pl.when(pl.program_id(1) == 0)
    def _():
        hn_ref[...] = _rms(x_ref[...], g_ref[...]).astype(BF16)

    h = hn_ref[...]
    a = jnp.dot(h, wg_ref[...], preferred_element_type=F32)
    b = jnp.dot(h, wu_ref[...], preferred_element_type=F32)
    o_ref[...] = (a * jax.nn.sigmoid(a) * b).astype(BF16)


def _ffn_down_kernel(act_ref, x_ref, wd_ref, g2_ref, o_ref, *rest, emit_norm):
    y = x_ref[...] + 0.5 * jnp.dot(act_ref[...], wd_ref[...], preferred_element_type=F32)
    o_ref[...] = y
    if emit_norm:
        rest[0][...] = _rms(y, g2_ref[...]).astype(BF16)


def _ffn(x, gain, w_gate, w_up, w_down, gain2, *, emit_norm, tm_up=2048, tf=512, tm_down=256):
    n, d = x.shape
    f = w_gate.shape[1]
    tm_up = min(tm_up, n)
    tm_down = min(tm_down, n)
    act = pl.pallas_call(
        _ffn_up_kernel,
        grid=(n // tm_up, f // tf),
        in_specs=[
            pl.BlockSpec((tm_up, d), lambda i, j: (i, 0), pipeline_mode=pl.Buffered(1)),
            pl.BlockSpec((1, d), lambda i, j: (0, 0)),
            pl.BlockSpec((d, tf), lambda i, j: (0, j)),
            pl.BlockSpec((d, tf), lambda i, j: (0, j)),
        ],
        out_specs=pl.BlockSpec((tm_up, tf), lambda i, j: (i, j)),
        out_shape=jax.ShapeDtypeStruct((n, f), BF16),
        scratch_shapes=[pltpu.VMEM((tm_up, d), BF16)],
        compiler_params=_params("parallel", "arbitrary"),
        name="ffn_up",
    )(x, gain, w_gate, w_up)
    out_shape = [jax.ShapeDtypeStruct((n, d), F32)]
    out_specs = [pl.BlockSpec((tm_down, d), lambda i: (i, 0))]
    if emit_norm:
        out_shape.append(jax.ShapeDtypeStruct((n, d), BF16))
        out_specs.append(pl.BlockSpec((tm_down, d), lambda i: (i, 0)))
    res = pl.pallas_call(
        functools.partial(_ffn_down_kernel, emit_norm=emit_norm),
        grid=(n // tm_down,),
        in_specs=[
            pl.BlockSpec((tm_down, f), lambda i: (i, 0)),
            pl.BlockSpec((tm_down, d), lambda i: (i, 0)),
            pl.BlockSpec((f, d), lambda i: (0, 0), pipeline_mode=pl.Buffered(1)),
            pl.BlockSpec((1, d), lambda i: (0, 0)),
        ],
        out_specs=out_specs,
        out_shape=out_shape,
        compiler_params=_params("parallel"),
        name="ffn_down_norm" if emit_norm else "ffn_down",
    )(act, x, w_down, gain2)
    return res if emit_norm else res[0]


def _rope_table_kernel(pos_ref, invf_ref, ca_ref, s1a_ref, s2a_ref, ci_ref, s1i_ref, s2i_ref):
    ang = pos_ref[...].astype(F32) * invf_ref[...]
    c = jnp.cos(ang)
    s = jnp.sin(ang)
    lane = lax.broadcasted_iota(I32, ang.shape, 1)
    in_att = lane < ROT_DIM
    ca_ref[...] = jnp.where(in_att, c, 1.0)
    s1a_ref[...] = jnp.where(lane < ROT_DIM // 2, -s, 0.0)
    s2a_ref[...] = jnp.where((lane >= ROT_DIM // 2) & in_att, s, 0.0)
    first = lane < IDX_DIM
    c_i = jnp.where(first, pltpu.roll(c, LANES - ROT_DIM, 1), pltpu.roll(c, IDX_DIM - ROT_DIM, 1))
    s_i = jnp.where(first, pltpu.roll(s, LANES - ROT_DIM, 1), pltpu.roll(s, IDX_DIM - ROT_DIM, 1))
    j = lane & (IDX_DIM - 1)
    in_idx = j < IDX_ROT_DIM
    ci_ref[...] = jnp.where(in_idx, c_i, 1.0)
    s1i_ref[...] = jnp.where(j < IDX_ROT_DIM // 2, -s_i, 0.0)
    s2i_ref[...] = jnp.where((j >= IDX_ROT_DIM // 2) & in_idx, s_i, 0.0)


def _rope_table(pos, invf, *, tm=MAX_TOKEN_TILE):
    n = pos.shape[0]
    tm = min(tm, n)
    spec = pl.BlockSpec((tm, LANES), lambda i: (i, 0))
    out = pl.pallas_call(
        _rope_table_kernel,
        grid=(n // tm,),
        in_specs=[pl.BlockSpec((tm, 1), lambda i: (i, 0)),
                  pl.BlockSpec((1, LANES), lambda i: (0, 0))],
        out_specs=[spec] * 6,
        out_shape=[jax.ShapeDtypeStruct((n, LANES), F32)] * 6,
        compiler_params=_params("parallel"),
        name="rope_tables",
    )(pos, invf)
    return out[:3], out[3:]


def _head_proj_kernel(h_ref, w_ref, gain_ref, c_ref, s1_ref, s2_ref, o_ref, r_ref,
                      *, transpose):
    half = ROT_DIM // 2
    res = jnp.dot(h_ref[...], w_ref[...], preferred_element_type=F32)
    heads = res.shape[1] // HEAD_DIM
    for hh in range(heads):
        r_ref[hh] = res[:, hh * HEAD_DIM:(hh + 1) * HEAD_DIM]

    def head(hh, carry):
        t = _rope(_rms(r_ref[hh], gain_ref[...]), c_ref[...], s1_ref[...], s2_ref[...], half)
        if transpose:
            o_ref[0, pl.ds(pl.multiple_of(hh * HEAD_DIM, HEAD_DIM), HEAD_DIM), :] = (
                t.T.astype(BF16))
        else:
            o_ref[0, hh] = t.astype(BF16)
        return carry

    lax.fori_loop(0, heads, head, 0)


def _head_proj(h, w, gain, tables, *, batch, transpose, name, tm=1024, tn=512):
    n, d = h.shape
    cols = w.shape[1]
    s = n // batch
    tm = min(tm, s)
    per_b = s // tm
    heads = tn // HEAD_DIM
    table_spec = pl.BlockSpec((tm, LANES), lambda i, j: (i, 0))
    if transpose:
        out_spec = pl.BlockSpec((1, tn, tm), lambda i, j: (i // per_b, j, i % per_b))
        out_shape = jax.ShapeDtypeStruct((batch, cols, s), BF16)
    else:
        out_spec = pl.BlockSpec((1, heads, tm, HEAD_DIM),
                                lambda i, j: (i // per_b, j, i % per_b, 0))
        out_shape = jax.ShapeDtypeStruct((batch, cols // HEAD_DIM, s, HEAD_DIM), BF16)
    return pl.pallas_call(
        functools.partial(_head_proj_kernel, transpose=transpose),
        grid=(n // tm, cols // tn),
        in_specs=[
            pl.BlockSpec((tm, d), lambda i, j: (i, 0)),
            pl.BlockSpec((d, tn), lambda i, j: (0, j)),
            pl.BlockSpec((1, HEAD_DIM), lambda i, j: (0, 0)),
            table_spec, table_spec, table_spec,
        ],
        out_specs=out_spec,
        out_shape=out_shape,
        scratch_shapes=[pltpu.VMEM((heads, tm, HEAD_DIM), F32)],
        compiler_params=_params("parallel", "arbitrary"),
        name=name,
    )(h, w, gain, *tables)


def _v_proj_kernel(h_ref, w_ref, o_ref):
    res = jnp.dot(h_ref[...], w_ref[...], preferred_element_type=F32)
    for a in range(res.shape[0] // ATT_T):
        o_ref[0, a] = res[a * ATT_T:(a + 1) * ATT_T, :].T.astype(BF16)


def _v_proj(h, w, *, batch, tm=1024, tn=512):
    n, d = h.shape
    cols = w.shape[1]
    s = n // batch
    tm = min(tm, s)
    per_b = s // tm
    return pl.pallas_call(
        _v_proj_kernel,
        grid=(n // tm, cols // tn),
        in_specs=[
            pl.BlockSpec((tm, d), lambda i, j: (i, 0)),
            pl.BlockSpec((d, tn), lambda i, j: (0, j)),
        ],
        out_specs=pl.BlockSpec((1, tm // ATT_T, tn, ATT_T),
                               lambda i, j: (i // per_b, i % per_b, j, 0)),
        out_shape=jax.ShapeDtypeStruct((batch, s // ATT_T, cols, ATT_T), BF16),
        compiler_params=_params("parallel", "arbitrary"),
        name="v_proj",
    )(h, w)


def _idx_kernel(h_ref, w_ref, kgain_ref, c_ref, s1_ref, s2_ref,
                qi_ref, kcat_ref, wi_ref, *, w_scale):
    half = IDX_ROT_DIM // 2
    qw = IDX_HEADS * IDX_DIM
    res = jnp.dot(h_ref[...], w_ref[...], preferred_element_type=F32)
    c, s1, s2 = c_ref[...], s1_ref[...], s2_ref[...]
    for t in range(qw // LANES):
        sl = slice(t * LANES, (t + 1) * LANES)
        q_t = _rope(res[:, sl], c, s1, s2, half).T
        qi_ref[0, 2 * t] = q_t[:IDX_DIM]
        qi_ref[0, 2 * t + 1] = q_t[IDX_DIM:]

    r = res[:, qw:qw + LANES]
    lane = lax.broadcasted_iota(I32, r.shape, 1)
    is_k = lane < IDX_DIM
    ms = jnp.sum(jnp.where(is_k, r * r, 0.0), axis=-1, keepdims=True) * (1.0 / IDX_DIM)
    kn = r * lax.rsqrt(ms + EPS) * kgain_ref[...]
    kr = _rope(kn, jnp.where(is_k, c, 1.0), jnp.where(is_k, s1, 0.0),
               jnp.where(is_k, s2, 0.0), half)
    hi = kr.astype(BF16).astype(F32)
    lo = kr - hi
    hi_lo = jnp.where(is_k, hi, pltpu.roll(lo, IDX_DIM, 1)).astype(BF16)
    kcat_ref[:, 0:LANES] = hi_lo
    kcat_ref[:, LANES:2 * LANES] = hi_lo
    is_w = (lane >= IDX_DIM) & (lane < IDX_DIM + IDX_HEADS)
    w_t = pltpu.roll(jnp.where(is_w, r * w_scale, 0.0), LANES - IDX_DIM, 1).T
    wi_ref[0] = w_t[:IDX_HEADS]


def _idx(h, w_idx, kgain, tables, *, batch, tm=512):
    n, d = h.shape
    cols = w_idx.shape[1]
    s = n // batch
    tm = min(tm, s)
    per_b = s // tm
    w_scale = (IDX_HEADS ** -0.5) * (IDX_DIM ** -0.5)
    return pl.pallas_call(
        functools.partial(_idx_kernel, w_scale=w_scale),
        grid=(n // tm,),
        in_specs=[
            pl.BlockSpec((tm, d), lambda i: (i, 0)),
            pl.BlockSpec((d, cols), lambda i: (0, 0)),
            pl.BlockSpec((1, LANES), lambda i: (0, 0)),
            pl.BlockSpec((tm, LANES), lambda i: (i, 0)),
            pl.BlockSpec((tm, LANES), lambda i: (i, 0)),
            pl.BlockSpec((tm, LANES), lambda i: (i, 0)),
        ],
        out_specs=[
            pl.BlockSpec((1, IDX_HEADS, IDX_DIM, tm),
                         lambda i: (i // per_b, 0, 0, i % per_b)),
            pl.BlockSpec((tm, 2 * LANES), lambda i: (i, 0)),
            pl.BlockSpec((1, IDX_HEADS, tm), lambda i: (i // per_b, 0, i % per_b)),
        ],
        out_shape=[
            jax.ShapeDtypeStruct((batch, IDX_HEADS, IDX_DIM, s), F32),
            jax.ShapeDtypeStruct((n, 2 * LANES), BF16),
            jax.ShapeDtypeStruct((batch, IDX_HEADS, s), F32),
        ],
        compiler_params=_params("parallel"),
        name="idx_proj",
    )(h, w_idx, kgain, *tables)


def _sgu_kernel(h_ref, w_ref, gv_ref, ws_ref, bs_ref, o_ref):
    tm = h_ref.shape[0]
    groups = tm // SGU_LEN
    res = jnp.dot(h_ref[...], w_ref[...], preferred_element_type=F32)
    u = jax.nn.gelu(res[:, :SGU_WIDTH])
    v = _rms(jax.nn.gelu(res[:, SGU_WIDTH:]), gv_ref[...]).astype(BF16)
    row = lax.broadcasted_iota(I32, (SGU_LEN, SGU_LEN), 0)
    col = lax.broadcasted_iota(I32, (SGU_LEN, SGU_LEN), 1)
    causal = (col // CHUNK) <= (row // CHUNK)
    for g in range(SGU_GROUPS):
        cs = slice(g * SGU_GROUP_DIM, (g + 1) * SGU_GROUP_DIM)
        wg = jnp.where(causal, ws_ref[g], 0.0).astype(BF16)
        vg = jnp.concatenate(
            [v[n * SGU_LEN:(n + 1) * SGU_LEN, cs] for n in range(groups)], axis=1)
        mixed = jnp.dot(wg, vg, preferred_element_type=F32) + bs_ref[g]
        for n in range(groups):
            rs = slice(n * SGU_LEN, (n + 1) * SGU_LEN)
            m = mixed[:, n * SGU_GROUP_DIM:(n + 1) * SGU_GROUP_DIM]
            o_ref[rs, cs] = (u[rs, cs] * m).astype(BF16)


def _sgu(h, w_sgu, gv, w_s, b_s, *, tm=512):
    n, d = h.shape
    tm = min(tm, n)
    return pl.pallas_call(
        _sgu_kernel,
        grid=(n // tm,),
        in_specs=[
            pl.BlockSpec((tm, d), lambda i: (i, 0)),
            pl.BlockSpec((d, 2 * SGU_WIDTH), lambda i: (0, 0)),
            pl.BlockSpec((1, SGU_WIDTH), lambda i: (0, 0)),
            pl.BlockSpec((SGU_GROUPS, SGU_LEN, SGU_LEN), lambda i: (0, 0, 0)),
            pl.BlockSpec((SGU_GROUPS, SGU_LEN, 1), lambda i: (0, 0, 0)),
        ],
        out_specs=pl.BlockSpec((tm, SGU_WIDTH), lambda i: (i, 0)),
        out_shape=jax.ShapeDtypeStruct((n, SGU_WIDTH), BF16),
        compiler_params=_params("parallel"),
        name="sgu_branch",
    )(h, w_sgu, gv, w_s, b_s)


def _gates_kernel(h_ref, w_ref, o_ref):
    res = jnp.dot(h_ref[...], w_ref[...], preferred_element_type=F32)
    o_ref[...] = jax.nn.sigmoid(res).astype(BF16)


def _gates(h, w_g, *, tm=1024, tn=1024):
    n, d = h.shape
    cols = w_g.shape[1]
    tm = min(tm, n)
    return pl.pallas_call(
        _gates_kernel,
        grid=(n // tm, cols // tn),
        in_specs=[
            pl.BlockSpec((tm, d), lambda i, j: (i, 0)),
            pl.BlockSpec((d, tn), lambda i, j: (0, j)),
        ],
        out_specs=pl.BlockSpec((tm, tn), lambda i, j: (i, j)),
        out_shape=jax.ShapeDtypeStruct((n, cols), BF16),
        compiler_params=_params("parallel", "arbitrary"),
        name="gates_proj",
    )(h, w_g)


def _select_kernel(qi_ref, wi_ref, kcat_ref, mask_ref, qcat_ref, s_ref, *, topk):
    t = SEL_T
    n_tiles = s_ref.shape[0]
    i = pl.program_id(1)
    n_live = i + 1

    for h in range(IDX_HEADS):
        q = qi_ref[0, h]
        hi = q.astype(BF16)
        lo = (q - hi.astype(F32)).astype(BF16)
        qcat_ref[h] = jnp.concatenate([hi, hi, lo, lo], axis=0)
    w = wi_ref[0]

    def score_tile(kt, mx, mn, diagonal):
        kc = kcat_ref[0, pl.ds(pl.multiple_of(kt * t, t), t), :]
        acc = jnp.zeros((t, t), F32)
        for h in range(IDX_HEADS):
            d = jnp.dot(kc, qcat_ref[h], preferred_element_type=F32)
            acc = acc + jnp.maximum(d, 0.0) * w[h:h + 1, :]
        if diagonal:
            admissible = ((lax.broadcasted_iota(I32, (t, t), 0) // CHUNK)
                          <= (lax.broadcasted_iota(I32, (t, t), 1) // CHUNK))
            s = jnp.where(admissible, acc, -jnp.inf)
            low = jnp.where(admissible, acc, jnp.inf)
        else:
            s = low = acc
        s_ref[kt] = s
        mx = jnp.maximum(mx, jnp.max(s, axis=0, keepdims=True))
        mn = jnp.minimum(mn, jnp.min(low, axis=0, keepdims=True))
        return mx, mn

    def score_body(j, carry):
        mx, mn = score_tile(2 * j, *carry, diagonal=False)
        return score_tile(jnp.minimum(2 * j + 1, i - 1), mx, mn, diagonal=False)

    mx, mn = lax.fori_loop(
        0, (i + 1) // 2, score_body,
        (jnp.full((1, t), -jnp.inf, F32), jnp.full((1, t), jnp.inf, F32)))
    mx, mn = score_tile(i, mx, mn, diagonal=True)

    n_adm = ((i * t + lax.broadcasted_iota(I32, (1, t), 1)) // CHUNK + 1) * CHUNK
    done0 = n_adm <= topk

    def count(c, strict=False):
        def add_tile(kt, weight, acc):
            s = s_ref[kt]
            ind = jnp.where(s > c if strict else s >= c, weight, 0.0)
            for r in range(t // COUNT_ROWS):
                acc = acc + ind[r * COUNT_ROWS:(r + 1) * COUNT_ROWS]
            return acc

        def count_body(j, acc):
            acc = add_tile(2 * j, 1.0, acc)
            second = 2 * j + 1
            return add_tile(jnp.minimum(second, n_live - 1),
                            jnp.where(second < n_live, 1.0, 0.0), acc)

        acc = lax.fori_loop(0, (n_live + 1) // 2, count_body,
                            jnp.zeros((COUNT_ROWS, t), F32))
        return jnp.sum(acc, axis=0, keepdims=True)

    def search_step(st):
        lo, hi, c_lo, thr, n_thr, live = st
        mid = 0.5 * lo + 0.5 * hi
        stuck = (mid <= lo) | (mid >= hi)
        cnt = count(mid)
        finish = (live > 0.0) & (stuck | (cnt == topk))
        thr = jnp.where(finish, jnp.where(stuck, lo, mid), thr)
        n_thr = jnp.where(finish, jnp.where(stuck, c_lo, cnt), n_thr)
        live = jnp.where(finish, 0.0, live)
        above = cnt > topk
        lo = jnp.where(above, mid, lo)
        c_lo = jnp.where(above, cnt, c_lo)
        hi = jnp.where(cnt < topk, mid, hi)
        return lo, hi, c_lo, thr, n_thr, live

    def search_cond(st):
        return (st[-2] > 0.0) & (st[-1] < MAX_PASSES)

    def search_body(st):
        passes = st[-1]
        st = search_step(st[:-2])
        return st + (jnp.sum(st[-1]), passes + 1)

    hi0 = jnp.minimum(mx + jnp.maximum(jnp.abs(mx), ABOVE_MAX_FLOOR) * ABOVE_MAX_REL,
                      jnp.finfo(F32).max)
    live0 = jnp.where(done0, 0.0, 1.0)
    st = (mn, hi0, n_adm.astype(F32), jnp.full((1, t), -jnp.inf, F32),
          jnp.zeros((1, t), F32), live0)
    st = lax.fori_loop(0, UNCHECKED_PASSES, lambda _, s_: search_step(s_), st)
    st = lax.while_loop(search_cond, search_body,
                        st + (jnp.sum(st[-1]), jnp.int32(UNCHECKED_PASSES)))
    thr, n_thr = st[3], st[4]
    tied = n_thr > topk
    any_tied = jnp.sum(jnp.where(tied, 1.0, 0.0)) > 0.0

    @pl.when(jnp.logical_not(any_tied))
    def _():
        def mask_body(kt, carry):
            s = s_ref[kt]
            sel = (s >= thr) & (s > -jnp.inf)
            mask_ref[0, 0, kt] = jnp.where(sel, 1, 0).astype(jnp.int8)
            return carry

        lax.fori_loop(0, n_live, mask_body, 0)

    @pl.when(any_tied)
    def _():
        room = topk - count(thr, strict=True)
        tri = jnp.where(lax.broadcasted_iota(I32, (t, t), 0)
                        >= lax.broadcasted_iota(I32, (t, t), 1), 1.0, 0.0).astype(BF16)

        def tie_body(kt, seen):
            s = s_ref[kt]
            eq = (s == thr) & tied
            eq_f = jnp.where(eq, 1.0, 0.0)
            upto = jnp.dot(tri, eq_f.astype(BF16), preferred_element_type=F32)
            rank = seen + upto - eq_f
            keep = (eq & (rank < room)) | (jnp.logical_not(eq) & (s >= thr))
            sel = keep & (s > -jnp.inf)
            mask_ref[0, 0, kt] = jnp.where(sel, 1, 0).astype(jnp.int8)
            return seen + upto[t - 1:t, :]

        lax.fori_loop(0, n_live, tie_body, jnp.zeros((1, t), F32))

    def zero_body(kt, carry):
        mask_ref[0, 0, kt] = jnp.zeros((t, t), jnp.int8)
        return carry

    lax.fori_loop(n_live, n_tiles, zero_body, 0)


def _select(qi_t, wi_t, kcat, *, topk):
    b, heads, dh, s = qi_t.shape
    t = SEL_T
    nq = s // t
    return pl.pallas_call(
        functools.partial(_select_kernel, topk=topk),
        grid=(b, nq),
        in_specs=[
            pl.BlockSpec((1, heads, dh, t), lambda bi, i: (bi, 0, 0, i)),
            pl.BlockSpec((1, heads, t), lambda bi, i: (bi, 0, i)),
            pl.BlockSpec((1, s, 4 * IDX_DIM), lambda bi, i: (bi, 0, 0)),
        ],
        out_specs=pl.BlockSpec((1, 1, nq, t, t), lambda bi, i: (bi, i, 0, 0, 0)),
        out_shape=jax.ShapeDtypeStruct((b, nq, nq, t, t), jnp.int8),
        scratch_shapes=[
            pltpu.VMEM((heads, 4 * IDX_DIM, t), BF16),
            pltpu.VMEM((nq, t, t), F32),
        ],
        compiler_params=_params("parallel", "arbitrary"),
        name="index_select",
    )(qi_t, wi_t, kcat)


def _attn_kernel(qt_ref, k_ref, vt_ref, mask_ref, o_ref,
                 m_ref, acc_ref, s_ref):
    t = ATT_T
    i = pl.program_id(2)
    n = (i + 1) * (ATT_Q // t)
    m_ref[...] = jnp.full(m_ref.shape, -jnp.inf, F32)
    acc_ref[...] = jnp.zeros(acc_ref.shape, F32)
    q_t = qt_ref[0]

    def logits(kt, slot):
        rows = pl.ds(pl.multiple_of(kt * t, t), t)
        for hh in range(ATT_HEADS):
            hs = slice(hh * HEAD_DIM, (hh + 1) * HEAD_DIM)
            s_ref[slot, hh] = jnp.dot(k_ref[0, hh, rows, :], q_t[hs, :],
                                      preferred_element_type=F32)

    def update(kt, slot, off):
        v_t = vt_ref[0, kt]
        sel = jnp.concatenate([mask_ref[0, a, kt] for a in range(ATT_Q // t)],
                              axis=1).astype(I32) > off
        for hh in range(ATT_HEADS):
            hs = slice(hh * HEAD_DIM, (hh + 1) * HEAD_DIM)
            s = jnp.where(sel, s_ref[slot, hh], MASKED_LOGIT)
            m_prev = m_ref[hh]
            m_new = jnp.maximum(m_prev, jnp.max(s, axis=0, keepdims=True))
            p = jnp.exp2(s - m_new)
            alpha = jnp.exp2(m_prev - m_new)
            m_ref[hh] = m_new
            lhs = jnp.concatenate([v_t[hs, :], jnp.ones((DENOM_ROWS, t), BF16)], axis=0)
            pv = jnp.dot(lhs, p.astype(BF16), preferred_element_type=F32)
            acc_ref[hh] = alpha * acc_ref[hh] + pv

    logits(0, 0)

    def body(j, carry):
        first = 2 * j
        second = jnp.minimum(first + 1, n - 1)
        logits(second, 1)
        update(first, 0, 0)
        logits(jnp.minimum(first + 2, n - 1), 0)
        update(second, 1, jnp.where(first + 1 < n, 0, 1))
        return carry

    lax.fori_loop(0, (n + 1) // 2, body, 0)
    for hh in range(ATT_HEADS):
        acc = acc_ref[hh]
        o_ref[0, hh * HEAD_DIM:(hh + 1) * HEAD_DIM, :] = (
            acc[:HEAD_DIM] / acc[HEAD_DIM:HEAD_DIM + 1]).astype(BF16)


def _attn(q_t, k, v_t, mask):
    b, _, s, _ = k.shape
    t = ATT_T
    tq = ATT_Q
    nk = s // t
    width = ATT_HEADS * HEAD_DIM
    return pl.pallas_call(
        _attn_kernel,
        grid=(b, N_HEADS // ATT_HEADS, s // tq),
        in_specs=[
            pl.BlockSpec((1, width, tq), lambda bi, hp, i: (bi, hp, i)),
            pl.BlockSpec((1, ATT_HEADS, s, HEAD_DIM), lambda bi, hp, i: (bi, hp, 0, 0)),
            pl.BlockSpec((1, nk, width, t), lambda bi, hp, i: (bi, 0, hp, 0)),
            pl.BlockSpec((1, tq // t, nk, t, t), lambda bi, hp, i: (bi, i, 0, 0, 0)),
        ],
        out_specs=pl.BlockSpec((1, width, tq), lambda bi, hp, i: (bi, hp, i)),
        out_shape=jax.ShapeDtypeStruct((b, ATT_WIDTH, s), BF16),
        scratch_shapes=[
            pltpu.VMEM((ATT_HEADS, 1, tq), F32),
            pltpu.VMEM((ATT_HEADS, HEAD_DIM + DENOM_ROWS, tq), F32),
            pltpu.VMEM((2, ATT_HEADS, t, tq), F32),
        ],
        compiler_params=_params("parallel", "parallel", "arbitrary"),
        name="masked_attention",
    )(q_t, k, v_t, mask)


def _merge_kernel(ya_ref, yb_ref, g_ref, x_ref, wa_ref, wb_ref, wo_ref, o_ref):
    d = x_ref.shape[1]
    a = lax.dot_general(ya_ref[0], wa_ref[...], (((0,), (0,)), ((), ())),
                        preferred_element_type=F32)
    b = jnp.dot(yb_ref[...], wb_ref[...], preferred_element_type=F32)
    merged = g_ref[:, :d].astype(F32) * a + g_ref[:, d:].astype(F32) * b
    o_ref[...] = x_ref[...] + jnp.dot(merged.astype(BF16), wo_ref[...],
                                      preferred_element_type=F32)


def _merge(ya_t, yb, gates, x, wa, wb, wo, *, tm=256):
    n, d = x.shape
    batch, width, s = ya_t.shape
    tm = min(tm, s)
    per_b = s // tm
    return pl.pallas_call(
        _merge_kernel,
        grid=(n // tm,),
        in_specs=[
            pl.BlockSpec((1, width, tm), lambda i: (i // per_b, 0, i % per_b)),
            pl.BlockSpec((tm, yb.shape[1]), lambda i: (i, 0)),
            pl.BlockSpec((tm, 2 * d), lambda i: (i, 0)),
            pl.BlockSpec((tm, d), lambda i: (i, 0)),
            pl.BlockSpec(wa.shape, lambda i: (0, 0)),
            pl.BlockSpec(wb.shape, lambda i: (0, 0)),
            pl.BlockSpec(wo.shape, lambda i: (0, 0)),
        ],
        out_specs=pl.BlockSpec((tm, d), lambda i: (i, 0)),
        out_shape=jax.ShapeDtypeStruct((n, d), F32),
        compiler_params=_params("parallel"),
        name="merge_out",
    )(ya_t, yb, gates, x, wa, wb, wo)


def _inv_freq_lanes():
    def inv(rot_dim):
        f = ROPE_THETA ** (-jnp.arange(0, rot_dim, 2, dtype=F32) / rot_dim)
        return jnp.concatenate([f, f])
    lanes = jnp.concatenate([inv(ROT_DIM), inv(IDX_ROT_DIM),
                             jnp.zeros((LANES - ROT_DIM - IDX_ROT_DIM,), F32)])
    return lanes[None, :]


def _layer(x, tables_att, tables_idx, p):
    b, s, d = x.shape
    n = b * s
    topk = min(TOPK_MAX, s // 4)
    assert s % MAX_TOKEN_TILE == 0, "token tiles must not straddle a batch row"
    x = x.reshape(n, d)

    x1, h = _ffn(x, p["norm_ffn1"][None], p["ffn1_w_gate"].astype(BF16),
                 p["ffn1_w_up"].astype(BF16), p["ffn1_w_down"].astype(BF16),
                 p["norm_mix"][None], emit_norm=True)

    w_in = p["w_in"]
    c_qkv = 3 * ATT_WIDTH
    c_qi = c_qkv + IDX_HEADS * IDX_DIM
    c_idx = c_qi + IDX_DIM + IDX_HEADS
    c_sgu = c_idx + 2 * SGU_WIDTH
    w_q = w_in[:, :ATT_WIDTH].astype(BF16)
    w_k = w_in[:, ATT_WIDTH:2 * ATT_WIDTH].astype(BF16)
    w_v = w_in[:, 2 * ATT_WIDTH:c_qkv].astype(BF16)
    w_idx = jnp.pad(w_in[:, c_qkv:c_idx],
                    ((0, 0), (0, LANES - IDX_DIM - IDX_HEADS))).astype(BF16)
    w_sgu = w_in[:, c_idx:c_sgu].astype(BF16)
    w_gate = w_in[:, c_sgu:].astype(BF16)

    q_gain = (p["q_norm"] * (HEAD_DIM ** -0.5 * LOG2_E))[None]
    q_t = _head_proj(h, w_q, q_gain, tables_att, batch=b, transpose=True, name="q_proj")
    k = _head_proj(h, w_k, p["k_norm"][None], tables_att, batch=b, transpose=False,
                   name="k_proj")
    v_t = _v_proj(h, w_v, batch=b)
    kgain = jnp.pad(p["idx_k_norm"], (0, LANES - IDX_DIM))[None]
    qi_t, kcat, wi_t = _idx(h, w_idx, kgain, tables_idx, batch=b)
    yb = _sgu(h, w_sgu, p["sgu_v_norm"][None], p["sgu_w_s"], p["sgu_b_s"][:, :, None])
    gates = _gates(h, w_gate)

    mask = _select(qi_t, wi_t, kcat.reshape(b, s, 4 * IDX_DIM), topk=topk)
    ya_t = _attn(q_t, k, v_t, mask)

    x2 = _merge(ya_t, yb, gates, x1, p["w_up_attn"].astype(BF16),
                p["w_up_sgu"].astype(BF16), p["w_out"].astype(BF16))
    out = _ffn(x2, p["norm_ffn2"][None], p["ffn2_w_gate"].astype(BF16),
               p["ffn2_w_up"].astype(BF16), p["ffn2_w_down"].astype(BF16),
               p["norm_ffn2"][None], emit_norm=False)
    return out.reshape(b, s, d)


def kernel(x, positions, norm_ffn1, ffn1_w_gate, ffn1_w_up, ffn1_w_down, norm_mix, w_in,
           q_norm, k_norm, idx_k_norm, sgu_v_norm, sgu_w_s, sgu_b_s, w_up_attn, w_up_sgu,
           w_out, norm_ffn2, ffn2_w_gate, ffn2_w_up, ffn2_w_down):
    params = dict(
        norm_ffn1=norm_ffn1, ffn1_w_gate=ffn1_w_gate, ffn1_w_up=ffn1_w_up,
        ffn1_w_down=ffn1_w_down, norm_mix=norm_mix, w_in=w_in, q_norm=q_norm,
        k_norm=k_norm, idx_k_norm=idx_k_norm, sgu_v_norm=sgu_v_norm, sgu_w_s=sgu_w_s,
        sgu_b_s=sgu_b_s, w_up_attn=w_up_attn, w_up_sgu=w_up_sgu, w_out=w_out,
        norm_ffn2=norm_ffn2, ffn2_w_gate=ffn2_w_gate, ffn2_w_up=ffn2_w_up,
        ffn2_w_down=ffn2_w_down)
    pos = positions.reshape(-1, 1).astype(I32)
    tables_att, tables_idx = _rope_table(pos, _inv_freq_lanes())
    for l in range(w_in.shape[0]):
        x = _layer(x, tables_att, tables_idx, {k: v[l] for k, v in params.items()})
    return x
```
